```python
import jax, jax.numpy as jnp
from jax import lax
import numpy as np

D_MODEL = 2048
BATCH = 2
SEQ = 8192
DEPTH = 1

RET_HEADS = 8
RET_QK_DIM = 128
RET_V_DIM = 256
RET_QK_WIDTH = RET_HEADS * RET_QK_DIM
RET_V_WIDTH = RET_HEADS * RET_V_DIM
RET_CHUNK = 128
ROPE_THETA = 10000.0
SGU_CHUNK = 128
SGU_GROUPS = 16
SGU_WIDTH = D_MODEL
SGU_GROUP_DIM = SGU_WIDTH // SGU_GROUPS
N_BRANCH = 2
N_GROUPS = 4
EXPERTS_PER_GROUP = 8
N_EXPERTS = N_GROUPS * EXPERTS_PER_GROUP
TOP_K = 2
D_EXPERT = 512
LN_EPS = 1e-5
DN_ALPHA = (2 * DEPTH) ** 0.25
DN_BETA = (8 * DEPTH) ** -0.25

SPLIT_SIZES = (RET_QK_WIDTH, RET_QK_WIDTH, RET_V_WIDTH, RET_V_WIDTH,
               SGU_WIDTH, SGU_WIDTH, D_MODEL, D_MODEL)
SPLIT_POINTS = tuple(int(p) for p in np.cumsum(SPLIT_SIZES)[:-1])
IN_WIDTH = int(sum(SPLIT_SIZES))
COL_SCALES = (1.0, 1.0, DN_BETA, 1.0, DN_BETA, 1.0, 1.0, 1.0)

kernel_name = "hybrid_retention_sgu_hiermoe_deepnorm"


def layer_norm(x, g, b):
    xf = x.astype(jnp.float32)
    mu = jnp.mean(xf, axis=-1, keepdims=True)
    var = jnp.mean(jnp.square(xf - mu), axis=-1, keepdims=True)
    return ((xf - mu) * lax.rsqrt(var + LN_EPS) * g + b).astype(x.dtype)


def group_norm_heads(y, g):
    yf = y.astype(jnp.float32)
    mu = jnp.mean(yf, axis=-1, keepdims=True)
    var = jnp.mean(jnp.square(yf - mu), axis=-1, keepdims=True)
    yn = (yf - mu) * lax.rsqrt(var + LN_EPS) * g.reshape(y.shape[2], y.shape[3]).astype(jnp.float32)
    return yn.reshape(y.shape[0], y.shape[1], -1)


def rotary(x, positions):
    half = x.shape[-1] // 2
    freq = ROPE_THETA ** (-jnp.arange(half, dtype=jnp.float32) / half)
    ang = positions.astype(jnp.float32)[..., None] * freq
    cos = jnp.cos(ang)[:, :, None, :]
    sin = jnp.sin(ang)[:, :, None, :]
    xf = x.astype(jnp.float32)
    x1, x2 = xf[..., :half], xf[..., half:]
    return jnp.concatenate([x1 * cos - x2 * sin, x2 * cos + x1 * sin], axis=-1)


def retention(q, k, v):
    b, s, h, dk = q.shape
    dv = v.shape[-1]
    c = RET_CHUNK
    nc = s // c
    log_gamma = jnp.log1p(-jnp.exp2(-5.0 - jnp.arange(h, dtype=jnp.float32)))
    idx = jnp.arange(c, dtype=jnp.float32)
    rel = idx[:, None] - idx[None, :]
    decay_mask = jnp.where(rel >= 0,
                           jnp.exp(log_gamma[:, None, None] * jnp.maximum(rel, 0.0)),
                           0.0)
    xi = jnp.exp(log_gamma[:, None] * (idx + 1.0))
    zeta = jnp.exp(log_gamma[:, None] * (c - 1.0 - idx))
    chunk_decay = jnp.exp(log_gamma * c)

    def to_chunks(t):
        return t.astype(jnp.float32).reshape(b, nc, c, h, -1).transpose(1, 0, 3, 2, 4)

    def step(state, inp):
        qc, kc, vc = inp
        scores = jnp.einsum('bhnd,bhmd->bhnm', qc, kc) * decay_mask
        inner = jnp.einsum('bhnm,bhme->bhne', scores, vc)
        cross = jnp.einsum('bhnd,bhde->bhne', qc, state) * xi[:, :, None]
        new_state = (state * chunk_decay[:, None, None]
                     + jnp.einsum('bhmd,bhme->bhde', kc * zeta[:, :, None], vc))
        return new_state, inner + cross

    state0 = jnp.zeros((b, h, dk, dv), jnp.float32)
    _, out = lax.scan(step, state0, (to_chunks(q), to_chunks(k), to_chunks(v)))
    return out.transpose(1, 0, 3, 2, 4).reshape(b, s, h, dv)


def spatial_gating(u, v, ln_g, ln_b, w_s, b_s):
    b, s, w = v.shape
    nc = s // SGU_CHUNK
    vn = layer_norm(v, ln_g, ln_b).reshape(b, nc, SGU_CHUNK, SGU_GROUPS, SGU_GROUP_DIM)
    causal = jnp.tril(jnp.ones((SGU_CHUNK, SGU_CHUNK), dtype=w_s.dtype))
    mix = jnp.einsum('gts,bnsgd->bntgd', w_s * causal, vn)
    mix = mix + jnp.transpose(b_s)[None, None, :, :, None]
    return u * mix.reshape(b, s, w)


def hier_moe(x, w_group, b_group, w_er, b_er, w1, w3, w2):
    b, s, d = x.shape
    xt = x.reshape(b * s, d)
    g_logits = (xt @ w_group + b_group).astype(jnp.float32)
    g_prob = jax.nn.softmax(g_logits, axis=-1)
    g_idx = jnp.argmax(g_logits, axis=-1)
    p_group = jnp.take_along_axis(g_prob, g_idx[:, None], axis=1)[:, 0]
    e_logits_all = (xt @ w_er + b_er).astype(jnp.float32).reshape(-1, N_GROUPS, EXPERTS_PER_GROUP)
    e_logits = jnp.take_along_axis(e_logits_all, g_idx[:, None, None], axis=1)[:, 0]
    top_val, top_idx = lax.top_k(e_logits, TOP_K)
    p_expert = jax.nn.softmax(top_val, axis=-1)
    expert_ids = g_idx[:, None] * EXPERTS_PER_GROUP + top_idx
    combine = p_group[:, None] * jnp.sum(
        jax.nn.one_hot(expert_ids, N_EXPERTS, dtype=jnp.float32) * p_expert[..., None], axis=1)
    combine = combine.astype(x.dtype)
    y = jnp.zeros_like(xt)
    for e in range(N_EXPERTS):
        hdn = jax.nn.silu(xt @ w1[e]) * (xt @ w3[e])
        y = y + combine[:, e:e + 1] * (hdn @ w2[e])
    return y.reshape(b, s, d)


def setup_inputs(seed: int = 0) -> dict:
    key = jax.random.key(seed)
    ks = jax.random.split(key, 24)
    f32 = jnp.float32
    D = D_MODEL
    nrm = lambda k, shp: jax.random.normal(k, shp, f32)
    col_scale = jnp.concatenate([jnp.full((sz,), sc, f32) for sz, sc in zip(SPLIT_SIZES, COL_SCALES)])
    inputs = {
        "x": nrm(ks[0], (BATCH, SEQ, D)),
        "positions": jnp.broadcast_to(jnp.arange(SEQ, dtype=jnp.int32), (BATCH, SEQ)),
        "w_in": nrm(ks[1], (DEPTH, D, IN_WIDTH)) * (D ** -0.5) * col_scale,
        "b_gate": 0.1 * nrm(ks[2], (DEPTH, N_BRANCH, D)),
        "ret_gn_g": 1.0 + 0.1 * nrm(ks[3], (DEPTH, RET_V_WIDTH)),
        "sgu_ln_g": 1.0 + 0.1 * nrm(ks[4], (DEPTH, SGU_WIDTH)),
        "sgu_ln_b": 0.02 * nrm(ks[5], (DEPTH, SGU_WIDTH)),
        "sgu_w": nrm(ks[6], (DEPTH, SGU_GROUPS, SGU_CHUNK, SGU_CHUNK)) * (SGU_CHUNK ** -0.5),
        "sgu_b": 1.0 + 0.1 * nrm(ks[7], (DEPTH, SGU_GROUPS, SGU_CHUNK)),
        "w_proj_ret": nrm(ks[8], (DEPTH, RET_V_WIDTH, D)) * (RET_V_WIDTH ** -0.5) * DN_BETA,
        "w_proj_sgu": nrm(ks[9], (DEPTH, SGU_WIDTH, D)) * (SGU_WIDTH ** -0.5) * DN_BETA,
        "w_out": nrm(ks[10], (DEPTH, D, D)) * (D ** -0.5) * DN_BETA,
        "ln1_g": 1.0 + 0.1 * nrm(ks[11], (DEPTH, D)),
        "ln1_b": 0.02 * nrm(ks[12], (DEPTH, D)),
        "w_group": nrm(ks[13], (DEPTH, D, N_GROUPS)) * (D ** -0.5),
        "b_group": 0.01 * nrm(ks[14], (DEPTH, N_GROUPS)),
        "w_er": nrm(ks[15], (DEPTH, D, N_EXPERTS)) * (D ** -0.5),
        "b_er": 0.01 * nrm(ks[16], (DEPTH, N_EXPERTS)),
        "w1": nrm(ks[17], (DEPTH, N_EXPERTS, D, D_EXPERT)) * (D ** -0.5) * DN_BETA,
        "w3": nrm(ks[18], (DEPTH, N_EXPERTS, D, D_EXPERT)) * (D ** -0.5) * DN_BETA,
        "w2": nrm(ks[19], (DEPTH, N_EXPERTS, D_EXPERT, D)) * (D_EXPERT ** -0.5) * DN_BETA,
        "ln2_g": 1.0 + 0.1 * nrm(ks[20], (DEPTH, D)),
        "ln2_b": 0.02 * nrm(ks[21], (DEPTH, D)),
    }
    return inputs


def reference(x, positions, w_in, b_gate, ret_gn_g, sgu_ln_g, sgu_ln_b, sgu_w, sgu_b,
              w_proj_ret, w_proj_sgu, w_out, ln1_g, ln1_b, w_group, b_group, w_er, b_er,
              w1, w3, w2, ln2_g, ln2_b):
    b, s, _ = x.shape
    for l in range(DEPTH):
        h = x @ w_in[l]
        q, k, v, g_ret, u, vs, g_a, g_b = jnp.split(h, SPLIT_POINTS, axis=-1)
        q = rotary(q.reshape(b, s, RET_HEADS, RET_QK_DIM), positions)
        k = rotary(k.reshape(b, s, RET_HEADS, RET_QK_DIM), positions) * (RET_QK_DIM ** -0.5)
        v = v.reshape(b, s, RET_HEADS, RET_V_DIM)
        y_ret = group_norm_heads(retention(q, k, v), ret_gn_g[l]).astype(x.dtype)
        y_a = (jax.nn.silu(g_ret) * y_ret) @ w_proj_ret[l]
        y_sgu = spatial_gating(jax.nn.gelu(u, approximate=False), jax.nn.gelu(vs, approximate=False),
                               sgu_ln_g[l], sgu_ln_b[l], sgu_w[l], sgu_b[l])
        y_b = y_sgu @ w_proj_sgu[l]
        merged = (jax.nn.sigmoid(g_a + b_gate[l, 0]) * y_a
                  + jax.nn.sigmoid(g_b + b_gate[l, 1]) * y_b)
        x = layer_norm(DN_ALPHA * x + merged @ w_out[l], ln1_g[l], ln1_b[l])
        moe = hier_moe(x, w_group[l], b_group[l], w_er[l], b_er[l], w1[l], w3[l], w2[l])
        x = layer_norm(DN_ALPHA * x + moe, ln2_g[l], ln2_b[l])
    return x
```

```python
import functools

import numpy as np
import jax
import jax.numpy as jnp
from jax import lax
from jax.experimental import pallas as pl
from jax.experimental.pallas import tpu as pltpu

F32 = jnp.float32
BF16 = jnp.bfloat16

RET_HEADS = 8
RET_QK_DIM = 128
RET_V_DIM = 256
CHUNK = 128
ROPE_THETA = 10000.0
SGU_GROUPS = 16
N_GROUPS = 4
EXPERTS_PER_GROUP = 8
N_EXPERTS = N_GROUPS * EXPERTS_PER_GROUP
ROUTER_ROWS = 40
LN_EPS = 1e-5
DEPTH = 1
DN_ALPHA = (2 * DEPTH) ** 0.25
SQRT_HALF = np.sqrt(0.5).astype(np.float32)

LANES = 128
VMEM_LIMIT = 52 * 1024 * 1024
EXPERT_TILE = 256


def _params(sem, vmem=VMEM_LIMIT):
    return pltpu.CompilerParams(dimension_semantics=sem, vmem_limit_bytes=vmem)


def _dot(a, b):
    return jnp.dot(a, b, preferred_element_type=F32)


def _dot_nt(a, b):
    return lax.dot_general(a, b, (((1,), (1,)), ((), ())), preferred_element_type=F32)


def _dot_tn(a, b):
    return lax.dot_general(a, b, (((0,), (0,)), ((), ())), preferred_element_type=F32)


def _layer_norm(r, g, b):
    mu = jnp.mean(r, axis=-1, keepdims=True)
    d = r - mu
    var = jnp.mean(d * d, axis=-1, keepdims=True)
    return d * lax.rsqrt(var + LN_EPS) * g + b


def _gelu(x):
    return 0.5 * x * (1.0 + lax.erf(x * SQRT_HALF))


def _rope_body(pos_ref, freq_ref, sign_ref, cos_ref, sin_ref):
    ang = pos_ref[...].astype(F32) * freq_ref[...]
    cos_ref[...] = jnp.cos(ang)
    sin_ref[...] = jnp.sin(ang) * sign_ref[...]


def _rope_table(pos_col, freq2, sign2):
    n = pos_col.shape[0]
    tm = min(1024, n)
    row = pl.BlockSpec((1, LANES), lambda i: (0, 0))
    out = pl.BlockSpec((tm, LANES), lambda i: (i, 0))
    return pl.pallas_call(
        _rope_body, grid=(n // tm,),
        in_specs=[pl.BlockSpec((tm, 1), lambda i: (i, 0)), row, row],
        out_specs=[out, out],
        out_shape=[jax.ShapeDtypeStruct((n, LANES), F32)] * 2,
        compiler_params=_params(("arbitrary",)), name="rope_table",
    )(pos_col, freq2, sign2)


def _inproj_body(x_ref, w_ref, o_ref, xb_ref):
    @pl.when(pl.program_id(1) == 0)
    def _():
        xb_ref[...] = x_ref[...].astype(BF16)

    o_ref[...] = _dot(xb_ref[...], w_ref[...]).astype(BF16)


def _in_proj(x2, w_bf):
    n, d = x2.shape
    width = w_bf.shape[1]
    tm, tn = min(1024, n), 1024
    return pl.pallas_call(
        _inproj_body, grid=(n // tm, width // tn),
        in_specs=[pl.BlockSpec((tm, d), lambda i, j: (i, 0)),
                  pl.BlockSpec((d, tn), lambda i, j: (0, j))],
        out_specs=pl.BlockSpec((tm, tn), lambda i, j: (i, j)),
        out_shape=jax.ShapeDtypeStruct((n, width), BF16),
        scratch_shapes=[pltpu.VMEM((tm, d), BF16)],
        compiler_params=_params(("arbitrary", "arbitrary")), name="in_proj",
    )(x2, w_bf)


def _retention_body(q_ref, k_ref, v_ref, g_ref, cos_ref, sin_ref, mask_ref, xi_ref, zeta_ref,
                    cd_ref, gain_ref, o_ref, state_ref, *, n_chunks):
    @pl.when(pl.program_id(2) == 0)
    def _():
        state_ref[...] = jnp.zeros_like(state_ref)

    mask, xi, zeta, cd, gain = mask_ref[0], xi_ref[0], zeta_ref[0], cd_ref[0], gain_ref[...]
    scale = RET_QK_DIM ** -0.5
    half = RET_QK_DIM // 2
    for c in range(n_chunks):
        rows = pl.ds(c * CHUNK, CHUNK)
        cos, sin = cos_ref[rows, :], sin_ref[rows, :]
        q = q_ref[rows, :].astype(F32)
        k = k_ref[rows, :].astype(F32)
        qr = q * cos + pltpu.roll(q, half, 1) * sin
        kr = (k * cos + pltpu.roll(k, half, 1) * sin) * scale
        vb = v_ref[rows, :]
        scores = _dot_nt(qr.astype(BF16), kr.astype(BF16)) * mask
        inner = _dot(scores.astype(BF16), vb)
        state = state_ref[...]
        cross = _dot((qr * xi).astype(BF16), state.astype(BF16))
        state_ref[...] = state * cd + _dot_tn((kr * zeta).astype(BF16), vb)
        out = inner + cross
        mu = jnp.mean(out, axis=-1, keepdims=True)
        d = out - mu
        var = jnp.mean(d * d, axis=-1, keepdims=True)
        yn = d * lax.rsqrt(var + LN_EPS) * gain
        g = g_ref[rows, :].astype(F32)
        o_ref[rows, :] = (g * jax.nn.sigmoid(g) * yn).astype(BF16)


def _retention(h, cos2, sin2, consts, gain, batch, seq):
    n = h.shape[0]
    tr = min(512, seq)
    nr = seq // tr
    mask, xi_b, zeta_b, cd_b = consts
    qk_blocks = RET_HEADS
    v_block0 = 2 * RET_HEADS * RET_QK_DIM // RET_V_DIM
    g_block0 = v_block0 + RET_HEADS

    def rowmap(off):
        return lambda b, hd, r: (b * nr + r, off + hd)

    def headmap(b, hd, r):
        return (hd, 0, 0)

    return pl.pallas_call(
        functools.partial(_retention_body, n_chunks=tr // CHUNK),
        grid=(batch, RET_HEADS, nr),
        in_specs=[pl.BlockSpec((tr, RET_QK_DIM), rowmap(0)),
                  pl.BlockSpec((tr, RET_QK_DIM), rowmap(qk_blocks)),
                  pl.BlockSpec((tr, RET_V_DIM), rowmap(v_block0)),
                  pl.BlockSpec((tr, RET_V_DIM), rowmap(g_block0)),
                  pl.BlockSpec((tr, LANES), lambda b, hd, r: (b * nr + r, 0)),
                  pl.BlockSpec((tr, LANES), lambda b, hd, r: (b * nr + r, 0)),
                  pl.BlockSpec((1, CHUNK, CHUNK), headmap),
                  pl.BlockSpec((1, CHUNK, RET_QK_DIM), headmap),
                  pl.BlockSpec((1, CHUNK, RET_QK_DIM), headmap),
                  pl.BlockSpec((1, 1, RET_V_DIM), headmap),
                  pl.BlockSpec((1, RET_V_DIM), lambda b, hd, r: (0, hd))],
        out_specs=pl.BlockSpec((tr, RET_V_DIM), rowmap(0)),
        out_shape=jax.ShapeDtypeStruct((n, RET_HEADS * RET_V_DIM), BF16),
        scratch_shapes=[pltpu.VMEM((RET_QK_DIM, RET_V_DIM), F32)],
        compiler_params=_params(("arbitrary", "arbitrary", "arbitrary")), name="retention",
    )(h, h, h, h, cos2, sin2, mask, xi_b, zeta_b, cd_b, gain)


def _sgu_body(u_ref, v_ref, lng_ref, lnb_ref, ws_ref, bs_ref, o_ref, wtril_ref, *, n_chunks):
    @pl.when(pl.program_id(0) == 0)
    def _():
        r = lax.broadcasted_iota(jnp.int32, (CHUNK, CHUNK), 0)
        c = lax.broadcasted_iota(jnp.int32, (CHUNK, CHUNK), 1)
        for g in range(SGU_GROUPS):
            wtril_ref[g] = jnp.where(r >= c, ws_ref[g], 0.0).astype(BF16)

    v = _gelu(v_ref[...].astype(F32))
    vn = _layer_norm(v, lng_ref[...], lnb_ref[...]).astype(BF16)
    for g in range(SGU_GROUPS):
        cols = pl.ds(g * CHUNK, CHUNK)
        lo = g * CHUNK
        rhs = jnp.concatenate([vn[c * CHUNK:(c + 1) * CHUNK, lo:lo + CHUNK] for c in range(n_chunks)], axis=1)
        mix = _dot(wtril_ref[g], rhs)
        bias = bs_ref[g]
        for c in range(n_chunks):
            rows = pl.ds(c * CHUNK, CHUNK)
            u = _gelu(u_ref[rows, cols].astype(F32))
            o_ref[rows, cols] = (u * (mix[:, c * CHUNK:(c + 1) * CHUNK] + bias)).astype(BF16)


def _spatial_gating(h, ln_g, ln_b, w_s, b_s_b, u_block, v_block):
    n = h.shape[0]
    width = SGU_GROUPS * CHUNK
    ts = min(256, n)
    row = pl.BlockSpec((1, width), lambda i: (0, 0))
    full3 = pl.BlockSpec((SGU_GROUPS, CHUNK, CHUNK), lambda i: (0, 0, 0))
    return pl.pallas_call(
        functools.partial(_sgu_body, n_chunks=ts // CHUNK),
        grid=(n // ts,),
        in_specs=[pl.BlockSpec((ts, width), lambda i: (i, u_block)),
                  pl.BlockSpec((ts, width), lambda i: (i, v_block)),
                  row, row, full3, full3],
        out_specs=pl.BlockSpec((ts, width), lambda i: (i, 0)),
        out_shape=jax.ShapeDtypeStruct((n, width), BF16),
        scratch_shapes=[pltpu.VMEM((SGU_GROUPS, CHUNK, CHUNK), BF16)],
        compiler_params=_params(("arbitrary",)), name="spatial_gating",
    )(h, h, ln_g, ln_b, w_s, b_s_b)


def _merge_body(yr_ref, ys_ref, ga_ref, gb_ref, bg_ref, wr_ref, ws_ref, o_ref):
    ya = _dot(yr_ref[...], wr_ref[...])
    yb = _dot(ys_ref[...], ws_ref[...])
    sa = jax.nn.sigmoid(ga_ref[...].astype(F32) + bg_ref[0:1, :])
    sb = jax.nn.sigmoid(gb_ref[...].astype(F32) + bg_ref[1:2, :])
    o_ref[...] = (sa * ya + sb * yb).astype(BF16)


def _merge(y_ret, y_sgu, h, b_gate, wr_bf, ws_bf, ga_block, gb_block):
    n, d = y_ret.shape
    tm = min(512, n)
    tile = lambda blk: pl.BlockSpec((tm, d), lambda i: (i, blk))
    resident = pl.BlockSpec((d, d), lambda i: (0, 0), pipeline_mode=pl.Buffered(1))
    return pl.pallas_call(
        _merge_body, grid=(n // tm,),
        in_specs=[tile(0), tile(0), tile(ga_block), tile(gb_block),
                  pl.BlockSpec((2, d), lambda i: (0, 0)), resident, resident],
        out_specs=tile(0),
        out_shape=jax.ShapeDtypeStruct((n, d), BF16),
        compiler_params=_params(("arbitrary",)), name="merge",
    )(y_ret, y_sgu, h, h, b_gate, wr_bf, ws_bf)


def _outproj_body(m_ref, x_ref, w_ref, g_ref, b_ref, o_ref):
    r = DN_ALPHA * x_ref[...] + _dot(m_ref[...], w_ref[...])
    o_ref[...] = _layer_norm(r, g_ref[...], b_ref[...])


def _out_proj(merged, x2, wo_bf, ln_g, ln_b):
    n, d = x2.shape
    tm = min(512, n)
    tile = pl.BlockSpec((tm, d), lambda i: (i, 0))
    row = pl.BlockSpec((1, d), lambda i: (0, 0))
    return pl.pallas_call(
        _outproj_body, grid=(n // tm,),
        in_specs=[tile, tile, pl.BlockSpec((d, d), lambda i: (0, 0), pipeline_mode=pl.Buffered(1)), row, row],
        out_specs=tile,
        out_shape=jax.ShapeDtypeStruct((n, d), F32),
        compiler_params=_params(("arbitrary",)), name="out_proj_ln1",
    )(merged, x2, wo_bf, ln_g, ln_b)


def _router_body(x_ref, whi_ref, wlo_ref, b_ref, ids_ref, cw_ref, cnt_ref, carry_ref, tri_ref, *, tm):
    @pl.when(pl.program_id(0) == 0)
    def _():
        carry_ref[...] = jnp.zeros_like(carry_ref)
        r = lax.broadcasted_iota(jnp.int32, (tm, tm), 0)
        c = lax.broadcasted_iota(jnp.int32, (tm, tm), 1)
        tri_ref[...] = jnp.where(r < c, 1.0, 0.0).astype(BF16)

    x = x_ref[...]
    hi = x.astype(BF16)
    lo = (x - hi.astype(F32)).astype(BF16)
    whi = whi_ref[...]
    logits = _dot_nt(whi, hi) + _dot_nt(whi, lo) + _dot_nt(wlo_ref[...], hi) + b_ref[...]
    el = logits[0:N_EXPERTS, :]
    gl = logits[N_EXPERTS:N_EXPERTS + N_GROUPS, :]

    gi = lax.broadcasted_iota(jnp.int32, gl.shape, 0)
    gmax = jnp.max(gl, axis=0, keepdims=True)
    gidx = jnp.min(jnp.where(gl == gmax, gi, N_GROUPS), axis=0, keepdims=True)
    p_group = 1.0 / jnp.sum(jnp.exp(gl - gmax), axis=0, keepdims=True)

    ei = lax.broadcasted_iota(jnp.int32, el.shape, 0)
    first = gidx * EXPERTS_PER_GROUP
    in_group = (ei >= first) & (ei < first + EXPERTS_PER_GROUP)
    m1 = jnp.where(in_group, el, -jnp.inf)
    v1 = jnp.max(m1, axis=0, keepdims=True)
    i1 = jnp.min(jnp.where(in_group & (m1 == v1), ei, N_EXPERTS), axis=0, keepdims=True)
    rest = in_group & (ei != i1)
    m2 = jnp.where(rest, el, -jnp.inf)
    v2 = jnp.max(m2, axis=0, keepdims=True)
    i2 = jnp.min(jnp.where(rest & (m2 == v2), ei, N_EXPERTS), axis=0, keepdims=True)
    t = jnp.exp(v2 - v1)
    p1 = 1.0 / (1.0 + t)
    p2 = t * p1

    sel1, sel2 = ei == i1, ei == i2
    onehot = jnp.where(sel1 | sel2, 1.0, 0.0)
    before = _dot(onehot.astype(BF16), tri_ref[...]) + carry_ref[:, 0:1]
    r1 = jnp.sum(jnp.where(sel1, before, 0.0), axis=0, keepdims=True)
    r2 = jnp.sum(jnp.where(sel2, before, 0.0), axis=0, keepdims=True)
    carry_ref[...] = carry_ref[...] + jnp.sum(onehot, axis=1, keepdims=True)

    ids_ref[0:1, :] = i1
    ids_ref[1:2, :] = i2
    ids_ref[2:3, :] = r1.astype(jnp.int32)
    ids_ref[3:4, :] = r2.astype(jnp.int32)
    cw_ref[0:1, :] = p_group * p1
    cw_ref[1:2, :] = p_group * p2
    cnt_ref[...] = carry_ref[...]


def _router(x1, w_hi, w_lo, bias_col):
    n, d = x1.shape
    tm = min(512, n)
    full = pl.BlockSpec((ROUTER_ROWS, d), lambda i: (0, 0))
    return pl.pallas_call(
        functools.partial(_router_body, tm=tm), grid=(n // tm,),
        in_specs=[pl.BlockSpec((tm, d), lambda i: (i, 0)), full, full,
                  pl.BlockSpec((ROUTER_ROWS, 1), lambda i: (0, 0))],
        out_specs=[pl.BlockSpec((4, tm), lambda i: (0, i)),
                   pl.BlockSpec((2, tm), lambda i: (0, i)),
                   pl.BlockSpec((N_EXPERTS, LANES), lambda i: (0, 0))],
        out_shape=[jax.ShapeDtypeStruct((4, n), jnp.int32),
                   jax.ShapeDtypeStruct((2, n), F32),
                   jax.ShapeDtypeStruct((N_EXPERTS, LANES), F32)],
        scratch_shapes=[pltpu.VMEM((N_EXPERTS, LANES), F32), pltpu.VMEM((tm, tm), BF16)],
        compiler_params=_params(("arbitrary",)), name="router",
    )(x1, w_hi, w_lo, bias_col)


def _routing_tables(ids, counts, max_tiles):
    te = EXPERT_TILE
    tiles_e = (counts + te - 1) // te
    tile_end = jnp.cumsum(tiles_e)
    row_start = (tile_end - tiles_e) * te
    total = tile_end[-1]
    dest1 = row_start[ids[0]] + ids[2]
    dest2 = row_start[ids[1]] + ids[3]
    ti = jnp.arange(max_tiles, dtype=jnp.int32)
    tile_src = jnp.minimum(ti, total - 1)
    tile_exp = jnp.minimum(jnp.searchsorted(tile_end, tile_src, side="right"), N_EXPERTS - 1).astype(jnp.int32)
    zero_fill = ((ti >= total) | (ti + 1 == tile_end[tile_exp])).astype(jnp.int32)
    return dest1, dest2, tile_exp, tile_src.astype(jnp.int32), total.reshape(1).astype(jnp.int32), zero_fill


def _row_copy(src, dst, sem):
    return pltpu.make_async_copy(src, dst, sem)


def _dispatch_body(d1_ref, d2_ref, zf_ref, x_ref, xs_hbm, zbuf, sem, zsem, *, tm, max_tiles):
    te = EXPERT_TILE

    @pl.when(pl.program_id(0) == 0)
    def _():
        zbuf[...] = jnp.zeros_like(zbuf)

        def fill(t):
            return _row_copy(zbuf, xs_hbm.at[pl.ds(t * te, te)], zsem)

        def start(t, c):
            @pl.when(zf_ref[t] == 1)
            def _():
                fill(t).start()
            return c

        def wait(t, c):
            @pl.when(zf_ref[t] == 1)
            def _():
                fill(t).wait()
            return c

        lax.fori_loop(0, max_tiles, start, 0)
        lax.fori_loop(0, max_tiles, wait, 0)

    base = pl.program_id(0) * tm

    def issue(j, c):
        src = x_ref.at[pl.ds(j, 1)]
        _row_copy(src, xs_hbm.at[pl.ds(d1_ref[base + j], 1)], sem).start()
        _row_copy(src, xs_hbm.at[pl.ds(d2_ref[base + j], 1)], sem).start()
        return c

    lax.fori_loop(0, tm, issue, 0, unroll=8)
    for _ in range(2):
        _row_copy(x_ref, xs_hbm.at[pl.ds(0, tm)], sem).wait()


def _dispatch(x1, dest1, dest2, zero_fill, max_tiles):
    n, d = x1.shape
    tm = min(512, n)
    return pl.pallas_call(
        functools.partial(_dispatch_body, tm=tm, max_tiles=max_tiles),
        grid_spec=pltpu.PrefetchScalarGridSpec(
            num_scalar_prefetch=3, grid=(n // tm,),
            in_specs=[pl.BlockSpec((tm, d), lambda i, *_: (i, 0))],
            out_specs=pl.BlockSpec(memory_space=pl.ANY),
            scratch_shapes=[pltpu.VMEM((EXPERT_TILE, d), F32),
                            pltpu.SemaphoreType.DMA(()), pltpu.SemaphoreType.DMA(())]),
        out_shape=jax.ShapeDtypeStruct((max_tiles * EXPERT_TILE, d), F32),
        compiler_params=_params(("arbitrary",)), name="dispatch",
    )(dest1, dest2, zero_fill, x1)


def _expert_body(te_ref, ts_ref, nv_ref, xs_ref, w1_ref, w3_ref, w2_ref, ys_ref, w1b, w3b, w2b):
    i = pl.program_id(0)
    valid = i < nv_ref[0]
    new_expert = (i == 0) | (te_ref[i] != te_ref[jnp.maximum(i - 1, 0)])

    @pl.when(valid & new_expert)
    def _():
        w1b[...] = w1_ref[0].astype(BF16)
        w3b[...] = w3_ref[0].astype(BF16)
        w2b[...] = w2_ref[0].astype(BF16)

    @pl.when(valid)
    def _():
        xb = xs_ref[...].astype(BF16)
        a = _dot(xb, w1b[...])
        b = _dot(xb, w3b[...])
        hdn = (a * jax.nn.sigmoid(a) * b).astype(BF16)
        ys_ref[...] = _dot(hdn, w2b[...])

    @pl.when(jnp.logical_not(valid))
    def _():
        ys_ref[...] = jnp.zeros_like(ys_ref)


def _experts(xs, tile_exp, tile_src, n_valid, w1, w3, w2):
    rows, d = xs.shape
    de = w1.shape[2]
    te = EXPERT_TILE
    return pl.pallas_call(
        _expert_body,
        grid_spec=pltpu.PrefetchScalarGridSpec(
            num_scalar_prefetch=3, grid=(rows // te,),
            in_specs=[pl.BlockSpec((te, d), lambda i, e, s, nv: (s[i], 0)),
                      pl.BlockSpec((1, d, de), lambda i, e, s, nv: (e[i], 0, 0)),
                      pl.BlockSpec((1, d, de), lambda i, e, s, nv: (e[i], 0, 0)),
                      pl.BlockSpec((1, de, d), lambda i, e, s, nv: (e[i], 0, 0))],
            out_specs=pl.BlockSpec((te, d), lambda i, e, s, nv: (i, 0)),
            scratch_shapes=[pltpu.VMEM((d, de), BF16), pltpu.VMEM((d, de), BF16), pltpu.VMEM((de, d), BF16)]),
        out_shape=jax.ShapeDtypeStruct((rows, d), F32),
        compiler_params=_params(("arbitrary",)), name="experts",
    )(tile_exp, tile_src, n_valid, xs, w1, w3, w2)


def _combine_body(d1_ref, d2_ref, x_ref, cw_ref, g_ref, b_ref, ys_hbm, o_ref, buf, sem, *, tm):
    i = pl.program_id(0)

    def issue(tile, slot):
        base = tile * tm

        def one(j, c):
            _row_copy(ys_hbm.at[pl.ds(d1_ref[base + j], 1)], buf.at[slot, 0, pl.ds(j, 1)], sem.at[slot]).start()
            _row_copy(ys_hbm.at[pl.ds(d2_ref[base + j], 1)], buf.at[slot, 1, pl.ds(j, 1)], sem.at[slot]).start()
            return c

        lax.fori_loop(0, tm, one, 0, unroll=8)

    @pl.when(i == 0)
    def _():
        issue(0, 0)

    @pl.when(i + 1 < pl.num_programs(0))
    def _():
        issue(i + 1, (i + 1) % 2)

    slot = i % 2
    for k in range(2):
        _row_copy(ys_hbm.at[pl.ds(0, tm)], buf.at[slot, k], sem.at[slot]).wait()
    y = cw_ref[:, 0:1] * buf[slot, 0] + cw_ref[:, 1:2] * buf[slot, 1]
    o_ref[...] = _layer_norm(DN_ALPHA * x_ref[...] + y, g_ref[...], b_ref[...])


def _combine(ys, x1, dest1, dest2, cw_col, ln_g, ln_b):
    n, d = x1.shape
    tm = min(256, n)
    tile = pl.BlockSpec((tm, d), lambda i, *_: (i, 0))
    row = pl.BlockSpec((1, d), lambda i, *_: (0, 0))
    return pl.pallas_call(
        functools.partial(_combine_body, tm=tm),
        grid_spec=pltpu.PrefetchScalarGridSpec(
            num_scalar_prefetch=2, grid=(n // tm,),
            in_specs=[tile, pl.BlockSpec((tm, 2), lambda i, *_: (i, 0)), row, row,
                      pl.BlockSpec(memory_space=pl.ANY)],
            out_specs=tile,
            scratch_shapes=[pltpu.VMEM((2, 2, tm, d), F32), pltpu.SemaphoreType.DMA((2,))]),
        out_shape=jax.ShapeDtypeStruct((n, d), F32),
        compiler_params=_params(("arbitrary",)), name="combine_ln2",
    )(dest1, dest2, x1, cw_col, ln_g, ln_b, ys)


def _retention_consts():
    h, c = RET_HEADS, CHUNK
    log_gamma = jnp.log1p(-jnp.exp2(-5.0 - jnp.arange(h, dtype=F32)))
    idx = jnp.arange(c, dtype=F32)
    rel = idx[:, None] - idx[None, :]
    mask = jnp.where(rel >= 0, jnp.exp(log_gamma[:, None, None] * jnp.maximum(rel, 0.0)), 0.0)
    xi = jnp.exp(log_gamma[:, None] * (idx + 1.0))
    zeta = jnp.exp(log_gamma[:, None] * (c - 1.0 - idx))
    chunk_decay = jnp.exp(log_gamma * c)
    xi_b = jnp.broadcast_to(xi[:, :, None], (h, c, RET_QK_DIM))
    zeta_b = jnp.broadcast_to(zeta[:, :, None], (h, c, RET_QK_DIM))
    cd_b = jnp.broadcast_to(chunk_decay[:, None, None], (h, 1, RET_V_DIM))
    return mask, xi_b, zeta_b, cd_b


def _rope_consts():
    half = RET_QK_DIM // 2
    freq = ROPE_THETA ** (-jnp.arange(half, dtype=F32) / half)
    freq2 = jnp.concatenate([freq, freq])[None, :]
    sign2 = jnp.concatenate([-jnp.ones((half,), F32), jnp.ones((half,), F32)])[None, :]
    return freq2, sign2


def kernel(x, positions, w_in, b_gate, ret_gn_g, sgu_ln_g, sgu_ln_b, sgu_w, sgu_b, w_proj_ret, w_proj_sgu,
           w_out, ln1_g, ln1_b, w_group, b_group, w_er, b_er, w1, w3, w2, ln2_g, ln2_b):
    batch, seq, d = x.shape
    n = batch * seq
    qk_w, v_w = RET_HEADS * RET_QK_DIM, RET_HEADS * RET_V_DIM
    assert d == v_w == SGU_GROUPS * CHUNK and seq % CHUNK == 0
    u_block = (2 * qk_w + 2 * v_w) // d
    max_tiles = (2 * n) // EXPERT_TILE + N_EXPERTS

    freq2, sign2 = _rope_consts()
    cos2, sin2 = _rope_table(positions.reshape(n, 1), freq2, sign2)
    ret_consts = _retention_consts()

    xc = x.reshape(n, d)
    for l in range(w_in.shape[0]):
        h = _in_proj(xc, w_in[l].astype(BF16))
        y_ret = _retention(h, cos2, sin2, ret_consts, ret_gn_g[l][None, :], batch, seq)
        bias_b = jnp.broadcast_to(sgu_b[l][:, :, None], (SGU_GROUPS, CHUNK, CHUNK))
        y_sgu = _spatial_gating(h, sgu_ln_g[l][None, :], sgu_ln_b[l][None, :], sgu_w[l], bias_b,
                                u_block, u_block + 1)
        merged = _merge(y_ret, y_sgu, h, b_gate[l], w_proj_ret[l].astype(BF16), w_proj_sgu[l].astype(BF16),
                        u_block + 2, u_block + 3)
        x1 = _out_proj(merged, xc, w_out[l].astype(BF16), ln1_g[l][None, :], ln1_b[l][None, :])

        w_r = jnp.concatenate([w_er[l], w_group[l]], axis=1).T
        w_r = jnp.pad(w_r, ((0, ROUTER_ROWS - w_r.shape[0]), (0, 0)))
        w_hi = w_r.astype(BF16)
        w_lo = (w_r - w_hi.astype(F32)).astype(BF16)
        bias = jnp.pad(jnp.concatenate([b_er[l], b_group[l]]), (0, ROUTER_ROWS - N_EXPERTS - N_GROUPS))[:, None]
        ids, cw, cnt = _router(x1, w_hi, w_lo, bias)

        counts = cnt[:, 0].astype(jnp.int32)
        dest1, dest2, tile_exp, tile_src, n_valid, zero_fill = _routing_tables(ids, counts, max_tiles)
        xs = _dispatch(x1, dest1, dest2, zero_fill, max_tiles)
        ys = _experts(xs, tile_exp, tile_src, n_valid, w1[l], w3[l], w2[l])
        xc = _combine(ys, x1, dest1, dest2, cw.T, ln2_g[l][None, :], ln2_b[l][None, :])
    return xc.reshape(batch, seq, d)
```

```python
import functools

import numpy as np
import jax
import jax.numpy as jnp
from jax import lax
from jax.experimental import pallas as pl
from jax.experimental.pallas import tpu as pltpu

F32 = jnp.float32
BF16 = jnp.bfloat16

RET_HEADS = 8
RET_QK_DIM = 128
RET_V_DIM = 256
CHUNK = 128
ROPE_THETA = 10000.0
SGU_GROUPS = 16
N_GROUPS = 4
EXPERTS_PER_GROUP = 8
N_EXPERTS = N_GROUPS * EXPERTS_PER_GROUP
ROUTER_ROWS = 40
LN_EPS = 1e-5
DEPTH = 1
DN_ALPHA = (2 * DEPTH) ** 0.25
SQRT_HALF = np.sqrt(0.5).astype(np.float32)

LANES = 128
VMEM_LIMIT = 52 * 1024 * 1024
EXPERT_TILE = 256


def _params(sem, vmem=VMEM_LIMIT):
    return pltpu.CompilerParams(dimension_semantics=sem, vmem_limit_bytes=vmem)


def _dot(a, b):
    return jnp.dot(a, b, preferred_element_type=F32)


def _dot_nt(a, b):
    return lax.dot_general(a, b, (((1,), (1,)), ((), ())), preferred_element_type=F32)


def _dot_tn(a, b):
    return lax.dot_general(a, b, (((0,), (0,)), ((), ())), preferred_element_type=F32)


def _layer_norm(r, g, b):
    mu = jnp.mean(r, axis=-1, keepdims=True)
    d = r - mu
    var = jnp.mean(d * d, axis=-1, keepdims=True)
    return d * lax.rsqrt(var + LN_EPS) * g + b


def _gelu(x):
    return 0.5 * x * (1.0 + lax.erf(x * SQRT_HALF))


def _rope_body(pos_ref, freq_ref, sign_ref, cos_ref, sin_ref):
    ang = pos_ref[...].astype(F32) * freq_ref[...]
    cos_ref[...] = jnp.cos(ang)
    sin_ref[...] = jnp.sin(ang) * sign_ref[...]


def _rope_table(pos_col, freq2, sign2):
    n = pos_col.shape[0]
    tm = min(1024, n)
    row = pl.BlockSpec((1, LANES), lambda i: (0, 0))
    out = pl.BlockSpec((tm, LANES), lambda i: (i, 0))
    return pl.pallas_call(
        _rope_body, grid=(n // tm,),
        in_specs=[pl.BlockSpec((tm, 1), lambda i: (i, 0)), row, row],
        out_specs=[out, out],
        out_shape=[jax.ShapeDtypeStruct((n, LANES), F32)] * 2,
        compiler_params=_params(("arbitrary",)), name="rope_table",
    )(pos_col, freq2, sign2)


def _inproj_body(x_ref, w_ref, o_ref, xb_ref):
    @pl.when(pl.program_id(1) == 0)
    def _():
        xb_ref[...] = x_ref[...].astype(BF16)

    o_ref[...] = _dot(xb_ref[...], w_ref[...]).astype(BF16)


def _in_proj(x2, w_bf):
    n, d = x2.shape
    width = w_bf.shape[1]
    tm, tn = min(1024, n), 1024
    return pl.pallas_call(
        _inproj_body, grid=(n // tm, width // tn),
        in_specs=[pl.BlockSpec((tm, d), lambda i, j: (i, 0)),
                  pl.BlockSpec((d, tn), lambda i, j: (0, j))],
        out_specs=pl.BlockSpec((tm, tn), lambda i, j: (i, j)),
        out_shape=jax.ShapeDtypeStruct((n, width), BF16),
        scratch_shapes=[pltpu.VMEM((tm, d), BF16)],
        compiler_params=_params(("arbitrary", "arbitrary")), name="in_proj",
    )(x2, w_bf)


HEAD_COLS = 2 * RET_QK_DIM + 2 * RET_V_DIM


def _retention_body(x_ref, cos_ref, sin_ref, wa_ref, wp_ref, mask_ref, xi_ref, zeta_ref, cd_ref, gain_ref,
                    o_ref, state_ref, yg_ref, *, n_chunks):
    @pl.when(pl.program_id(1) == 0)
    def _():
        state_ref[...] = jnp.zeros_like(state_ref)

    scale = RET_QK_DIM ** -0.5
    half = RET_QK_DIM // 2
    dk, dv = RET_QK_DIM, RET_V_DIM
    xb = x_ref[...].astype(BF16)
    for hd in range(RET_HEADS):
        proj = _dot(xb, wa_ref[hd])
        gain = gain_ref[:, hd * dv:(hd + 1) * dv]
        for c in range(n_chunks):
            lo, hi = c * CHUNK, (c + 1) * CHUNK
            cos, sin = cos_ref[lo:hi, :], sin_ref[lo:hi, :]
            q, k = proj[lo:hi, 0:dk], proj[lo:hi, dk:2 * dk]
            vb = proj[lo:hi, 2 * dk:2 * dk + dv].astype(BF16)
            g = proj[lo:hi, 2 * dk + dv:]
            qr = q * cos + pltpu.roll(q, half, 1) * sin
            kr = (k * cos + pltpu.roll(k, half, 1) * sin) * scale
            scores = _dot_nt(qr.astype(BF16), kr.astype(BF16)) * mask_ref[hd]
            state = state_ref[hd]
            lhs = jnp.concatenate([scores.astype(BF16), (qr * xi_ref[hd]).astype(BF16)], axis=1)
            rhs = jnp.concatenate([vb, state.astype(BF16)], axis=0)
            out = _dot(lhs, rhs)
            state_ref[hd] = state * cd_ref[hd] + _dot_tn((kr * zeta_ref[hd]).astype(BF16), vb)
            mu = jnp.mean(out, axis=-1, keepdims=True)
            d = out - mu
            var = jnp.mean(d * d, axis=-1, keepdims=True)
            yn = d * lax.rsqrt(var + LN_EPS) * gain
            yg_ref[lo:hi, hd * dv:(hd + 1) * dv] = (g * jax.nn.sigmoid(g) * yn).astype(BF16)
    o_ref[...] = _dot(yg_ref[...], wp_ref[...]).astype(BF16)


def _retention_branch(x2, cos2, sin2, wa_bf, wp_bf, consts, gain, batch, seq):
    n, d = x2.shape
    tr = min(256, seq)
    nr = seq // tr
    mask, xi_b, zeta_b, cd_b = consts
    vw = RET_HEADS * RET_V_DIM

    def once(shape):
        return pl.BlockSpec(shape, lambda b, r: (0,) * len(shape), pipeline_mode=pl.Buffered(1))

    rows = lambda w: pl.BlockSpec((tr, w), lambda b, r: (b * nr + r, 0))
    return pl.pallas_call(
        functools.partial(_retention_body, n_chunks=tr // CHUNK),
        grid=(batch, nr),
        in_specs=[rows(d), rows(LANES), rows(LANES),
                  once((RET_HEADS, d, HEAD_COLS)), once((vw, d)),
                  once((RET_HEADS, CHUNK, CHUNK)), once((RET_HEADS, CHUNK, RET_QK_DIM)),
                  once((RET_HEADS, CHUNK, RET_QK_DIM)), once((RET_HEADS, 1, RET_V_DIM)), once((1, vw))],
        out_specs=rows(d),
        out_shape=jax.ShapeDtypeStruct((n, d), BF16),
        scratch_shapes=[pltpu.VMEM((RET_HEADS, RET_QK_DIM, RET_V_DIM), F32), pltpu.VMEM((tr, vw), BF16)],
        compiler_params=_params(("arbitrary", "arbitrary")), name="retention_branch",
    )(x2, cos2, sin2, wa_bf, wp_bf, mask, xi_b, zeta_b, cd_b, gain)


def _sgu_body(u_ref, v_ref, lng_ref, lnb_ref, ws_ref, bs_ref, o_ref, wtril_ref, *, n_chunks):
    @pl.when(pl.program_id(0) == 0)
    def _():
        r = lax.broadcasted_iota(jnp.int32, (CHUNK, CHUNK), 0)
        c = lax.broadcasted_iota(jnp.int32, (CHUNK, CHUNK), 1)
        for g in range(SGU_GROUPS):
            wtril_ref[g] = jnp.where(r >= c, ws_ref[g], 0.0).astype(BF16)

    v = _gelu(v_ref[...].astype(F32))
    vn = _layer_norm(v, lng_ref[...], lnb_ref[...]).astype(BF16)
    for g in range(SGU_GROUPS):
        cols = pl.ds(g * CHUNK, CHUNK)
        lo = g * CHUNK
        rhs = jnp.concatenate([vn[c * CHUNK:(c + 1) * CHUNK, lo:lo + CHUNK] for c in range(n_chunks)], axis=1)
        mix = _dot(wtril_ref[g], rhs)
        bias = bs_ref[g]
        for c in range(n_chunks):
            rows = pl.ds(c * CHUNK, CHUNK)
            u = _gelu(u_ref[rows, cols].astype(F32))
            o_ref[rows, cols] = (u * (mix[:, c * CHUNK:(c + 1) * CHUNK] + bias)).astype(BF16)


def _spatial_gating(h, ln_g, ln_b, w_s, b_s_b, u_block, v_block):
    n = h.shape[0]
    width = SGU_GROUPS * CHUNK
    ts = min(256, n)
    row = pl.BlockSpec((1, width), lambda i: (0, 0))
    full3 = pl.BlockSpec((SGU_GROUPS, CHUNK, CHUNK), lambda i: (0, 0, 0))
    return pl.pallas_call(
        functools.partial(_sgu_body, n_chunks=ts // CHUNK),
        grid=(n // ts,),
        in_specs=[pl.BlockSpec((ts, width), lambda i: (i, u_block)),
                  pl.BlockSpec((ts, width), lambda i: (i, v_block)),
                  row, row, full3, full3],
        out_specs=pl.BlockSpec((ts, width), lambda i: (i, 0)),
        out_shape=jax.ShapeDtypeStruct((n, width), BF16),
        scratch_shapes=[pltpu.VMEM((SGU_GROUPS, CHUNK, CHUNK), BF16)],
        compiler_params=_params(("arbitrary",)), name="spatial_gating",
    )(h, h, ln_g, ln_b, w_s, b_s_b)


def _merge_body(ya_ref, ys_ref, ga_ref, gb_ref, bg_ref, ws_ref, o_ref):
    yb = _dot(ys_ref[...], ws_ref[...])
    sa = jax.nn.sigmoid(ga_ref[...].astype(F32) + bg_ref[0:1, :])
    sb = jax.nn.sigmoid(gb_ref[...].astype(F32) + bg_ref[1:2, :])
    o_ref[...] = (sa * ya_ref[...].astype(F32) + sb * yb).astype(BF16)


def _merge(y_a, y_sgu, h, b_gate, ws_bf, ga_block, gb_block):
    n, d = y_a.shape
    tm = min(512, n)
    tile = lambda blk: pl.BlockSpec((tm, d), lambda i: (i, blk))
    resident = pl.BlockSpec((d, d), lambda i: (0, 0), pipeline_mode=pl.Buffered(1))
    return pl.pallas_call(
        _merge_body, grid=(n // tm,),
        in_specs=[tile(0), tile(0), tile(ga_block), tile(gb_block),
                  pl.BlockSpec((2, d), lambda i: (0, 0)), resident],
        out_specs=tile(0),
        out_shape=jax.ShapeDtypeStruct((n, d), BF16),
        compiler_params=_params(("arbitrary",)), name="merge",
    )(y_a, y_sgu, h, h, b_gate, ws_bf)


def _outproj_body(m_ref, x_ref, w_ref, g_ref, b_ref, o_ref):
    r = DN_ALPHA * x_ref[...] + _dot(m_ref[...], w_ref[...])
    o_ref[...] = _layer_norm(r, g_ref[...], b_ref[...])


def _out_proj(merged, x2, wo_bf, ln_g, ln_b):
    n, d = x2.shape
    tm = min(512, n)
    tile = pl.BlockSpec((tm, d), lambda i: (i, 0))
    row = pl.BlockSpec((1, d), lambda i: (0, 0))
    return pl.pallas_call(
        _outproj_body, grid=(n // tm,),
        in_specs=[tile, tile, pl.BlockSpec((d, d), lambda i: (0, 0), pipeline_mode=pl.Buffered(1)), row, row],
        out_specs=tile,
        out_shape=jax.ShapeDtypeStruct((n, d), F32),
        compiler_params=_params(("arbitrary",)), name="out_proj_ln1",
    )(merged, x2, wo_bf, ln_g, ln_b)


def _router_body(x_ref, whi_ref, wlo_ref, b_ref, ids_ref, cw_ref, cnt_ref, carry_ref, tri_ref, *, tm):
    @pl.when(pl.program_id(0) == 0)
    def _():
        carry_ref[...] = jnp.zeros_like(carry_ref)
        r = lax.broadcasted_iota(jnp.int32, (tm, tm), 0)
        c = lax.broadcasted_iota(jnp.int32, (tm, tm), 1)
        tri_ref[...] = jnp.where(r < c, 1.0, 0.0).astype(BF16)

    x = x_ref[...]
    hi = x.astype(BF16)
    lo = (x - hi.astype(F32)).astype(BF16)
    whi = whi_ref[...]
    logits = _dot_nt(whi, hi) + _dot_nt(whi, lo) + _dot_nt(wlo_ref[...], hi) + b_ref[...]
    el = logits[0:N_EXPERTS, :]
    gl = logits[N_EXPERTS:N_EXPERTS + N_GROUPS, :]

    gi = lax.broadcasted_iota(jnp.int32, gl.shape, 0)
    gmax = jnp.max(gl, axis=0, keepdims=True)
    gidx = jnp.min(jnp.where(gl == gmax, gi, N_GROUPS), axis=0, keepdims=True)
    p_group = 1.0 / jnp.sum(jnp.exp(gl - gmax), axis=0, keepdims=True)

    ei = lax.broadcasted_iota(jnp.int32, el.shape, 0)
    first = gidx * EXPERTS_PER_GROUP
    in_group = (ei >= first) & (ei < first + EXPERTS_PER_GROUP)
    m1 = jnp.where(in_group, el, -jnp.inf)
    v1 = jnp.max(m1, axis=0, keepdims=True)
    i1 = jnp.min(jnp.where(in_group & (m1 == v1), ei, N_EXPERTS), axis=0, keepdims=True)
    rest = in_group & (ei != i1)
    m2 = jnp.where(rest, el, -jnp.inf)
    v2 = jnp.max(m2, axis=0, keepdims=True)
    i2 = jnp.min(jnp.where(rest & (m2 == v2), ei, N_EXPERTS), axis=0, keepdims=True)
    t = jnp.exp(v2 - v1)
    p1 = 1.0 / (1.0 + t)
    p2 = t * p1

    sel1, sel2 = ei == i1, ei == i2
    onehot = jnp.where(sel1 | sel2, 1.0, 0.0)
    before = _dot(onehot.astype(BF16), tri_ref[...]) + carry_ref[:, 0:1]
    r1 = jnp.sum(jnp.where(sel1, before, 0.0), axis=0, keepdims=True)
    r2 = jnp.sum(jnp.where(sel2, before, 0.0), axis=0, keepdims=True)
    carry_ref[...] = carry_ref[...] + jnp.sum(onehot, axis=1, keepdims=True)

    ids_ref[0:1, :] = i1
    ids_ref[1:2, :] = i2
    ids_ref[2:3, :] = r1.astype(jnp.int32)
    ids_ref[3:4, :] = r2.astype(jnp.int32)
    cw_ref[0:1, :] = p_group * p1
    cw_ref[1:2, :] = p_group * p2
    cnt_ref[...] = carry_ref[...]


def _router(x1, w_hi, w_lo, bias_col):
    n, d = x1.shape
    tm = min(512, n)
    full = pl.BlockSpec((ROUTER_ROWS, d), lambda i: (0, 0))
    return pl.pallas_call(
        functools.partial(_router_body, tm=tm), grid=(n // tm,),
        in_specs=[pl.BlockSpec((tm, d), lambda i: (i, 0)), full, full,
                  pl.BlockSpec((ROUTER_ROWS, 1), lambda i: (0, 0))],
        out_specs=[pl.BlockSpec((4, tm), lambda i: (0, i)),
                   pl.BlockSpec((2, tm), lambda i: (0, i)),
                   pl.BlockSpec((N_EXPERTS, LANES), lambda i: (0, 0))],
        out_shape=[jax.ShapeDtypeStruct((4, n), jnp.int32),
                   jax.ShapeDtypeStruct((2, n), F32),
                   jax.ShapeDtypeStruct((N_EXPERTS, LANES), F32)],
        scratch_shapes=[pltpu.VMEM((N_EXPERTS, LANES), F32), pltpu.VMEM((tm, tm), BF16)],
        compiler_params=_params(("arbitrary",)), name="router",
    )(x1, w_hi, w_lo, bias_col)


def _routing_tables(ids, counts, max_tiles):
    te = EXPERT_TILE
    tiles_e = (counts + te - 1) // te
    tile_end = jnp.cumsum(tiles_e)
    row_start = (tile_end - tiles_e) * te
    total = tile_end[-1]
    dest1 = row_start[ids[0]] + ids[2]
    dest2 = row_start[ids[1]] + ids[3]
    ti = jnp.arange(max_tiles, dtype=jnp.int32)
    tile_src = jnp.minimum(ti, total - 1)
    tile_exp = jnp.sum((tile_end[None, :] <= tile_src[:, None]).astype(jnp.int32), axis=1)
    tile_exp = jnp.minimum(tile_exp, N_EXPERTS - 1)
    zero_fill = ((ti >= total) | (ti + 1 == tile_end[tile_exp])).astype(jnp.int32)
    return dest1, dest2, tile_exp, tile_src.astype(jnp.int32), total.reshape(1).astype(jnp.int32), zero_fill


def _row_copy(src, dst, sem):
    return pltpu.make_async_copy(src, dst, sem)


def _dispatch_body(d1_ref, d2_ref, zf_ref, x_ref, xs_hbm, zbuf, sem, zsem, *, tm, max_tiles):
    te = EXPERT_TILE

    @pl.when(pl.program_id(0) == 0)
    def _():
        zbuf[...] = jnp.zeros_like(zbuf)

        def fill(t):
            return _row_copy(zbuf, xs_hbm.at[pl.ds(t * te, te)], zsem)

        def start(t, c):
            @pl.when(zf_ref[t] == 1)
            def _():
                fill(t).start()
            return c

        def wait(t, c):
            @pl.when(zf_ref[t] == 1)
            def _():
                fill(t).wait()
            return c

        lax.fori_loop(0, max_tiles, start, 0)
        lax.fori_loop(0, max_tiles, wait, 0)

    base = pl.program_id(0) * tm

    def issue(j, c):
        src = x_ref.at[pl.ds(j, 1)]
        _row_copy(src, xs_hbm.at[pl.ds(d1_ref[base + j], 1)], sem).start()
        _row_copy(src, xs_hbm.at[pl.ds(d2_ref[base + j], 1)], sem).start()
        return c

    lax.fori_loop(0, tm, issue, 0, unroll=8)
    for _ in range(2):
        _row_copy(x_ref, xs_hbm.at[pl.ds(0, tm)], sem).wait()


def _dispatch(x1, dest1, dest2, zero_fill, max_tiles):
    n, d = x1.shape
    tm = min(512, n)
    return pl.pallas_call(
        functools.partial(_dispatch_body, tm=tm, max_tiles=max_tiles),
        grid_spec=pltpu.PrefetchScalarGridSpec(
            num_scalar_prefetch=3, grid=(n // tm,),
            in_specs=[pl.BlockSpec((tm, d), lambda i, *_: (i, 0))],
            out_specs=pl.BlockSpec(memory_space=pl.ANY),
            scratch_shapes=[pltpu.VMEM((EXPERT_TILE, d), F32),
                            pltpu.SemaphoreType.DMA(()), pltpu.SemaphoreType.DMA(())]),
        out_shape=jax.ShapeDtypeStruct((max_tiles * EXPERT_TILE, d), F32),
        compiler_params=_params(("arbitrary",)), name="dispatch",
    )(dest1, dest2, zero_fill, x1)


def _expert_body(te_ref, ts_ref, nv_ref, xs_ref, w1_ref, w3_ref, w2_ref, ys_ref, w1b, w3b, w2b):
    i = pl.program_id(0)
    valid = i < nv_ref[0]
    new_expert = (i == 0) | (te_ref[i] != te_ref[jnp.maximum(i - 1, 0)])

    @pl.when(valid & new_expert)
    def _():
        w1b[...] = w1_ref[0].astype(BF16)
        w3b[...] = w3_ref[0].astype(BF16)
        w2b[...] = w2_ref[0].astype(BF16)

    @pl.when(valid)
    def _():
        xb = xs_ref[...].astype(BF16)
        a = _dot(xb, w1b[...])
        b = _dot(xb, w3b[...])
        hdn = (a * jax.nn.sigmoid(a) * b).astype(BF16)
        ys_ref[...] = _dot(hdn, w2b[...])

    @pl.when(jnp.logical_not(valid))
    def _():
        ys_ref[...] = jnp.zeros_like(ys_ref)


def _experts(xs, tile_exp, tile_src, n_valid, w1, w3, w2):
    rows, d = xs.shape
    de = w1.shape[2]
    te = EXPERT_TILE
    return pl.pallas_call(
        _expert_body,
        grid_spec=pltpu.PrefetchScalarGridSpec(
            num_scalar_prefetch=3, grid=(rows // te,),
            in_specs=[pl.BlockSpec((te, d), lambda i, e, s, nv: (s[i], 0)),
                      pl.BlockSpec((1, d, de), lambda i, e, s, nv: (e[i], 0, 0)),
                      pl.BlockSpec((1, d, de), lambda i, e, s, nv: (e[i], 0, 0)),
                      pl.BlockSpec((1, de, d), lambda i, e, s, nv: (e[i], 0, 0))],
            out_specs=pl.BlockSpec((te, d), lambda i, e, s, nv: (i, 0)),
            scratch_shapes=[pltpu.VMEM((d, de), BF16), pltpu.VMEM((d, de), BF16), pltpu.VMEM((de, d), BF16)]),
        out_shape=jax.ShapeDtypeStruct((rows, d), F32),
        compiler_params=_params(("arbitrary",)), name="experts",
    )(tile_exp, tile_src, n_valid, xs, w1, w3, w2)


def _combine_body(d1_ref, d2_ref, x_ref, cw_ref, g_ref, b_ref, ys_hbm, o_ref, buf, sem, *, tm):
    i = pl.program_id(0)

    def issue(tile, slot):
        base = tile * tm

        def one(j, c):
            _row_copy(ys_hbm.at[pl.ds(d1_ref[base + j], 1)], buf.at[slot, 0, pl.ds(j, 1)], sem.at[slot]).start()
            _row_copy(ys_hbm.at[pl.ds(d2_ref[base + j], 1)], buf.at[slot, 1, pl.ds(j, 1)], sem.at[slot]).start()
            return c

        lax.fori_loop(0, tm, one, 0, unroll=8)

    @pl.when(i == 0)
    def _():
        issue(0, 0)

    @pl.when(i + 1 < pl.num_programs(0))
    def _():
        issue(i + 1, (i + 1) % 2)

    slot = i % 2
    for k in range(2):
        _row_copy(ys_hbm.at[pl.ds(0, tm)], buf.at[slot, k], sem.at[slot]).wait()
    y = cw_ref[:, 0:1] * buf[slot, 0] + cw_ref[:, 1:2] * buf[slot, 1]
    o_ref[...] = _layer_norm(DN_ALPHA * x_ref[...] + y, g_ref[...], b_ref[...])


def _combine(ys, x1, dest1, dest2, cw_col, ln_g, ln_b):
    n, d = x1.shape
    tm = min(256, n)
    tile = pl.BlockSpec((tm, d), lambda i, *_: (i, 0))
    row = pl.BlockSpec((1, d), lambda i, *_: (0, 0))
    return pl.pallas_call(
        functools.partial(_combine_body, tm=tm),
        grid_spec=pltpu.PrefetchScalarGridSpec(
            num_scalar_prefetch=2, grid=(n // tm,),
            in_specs=[tile, pl.BlockSpec((tm, 2), lambda i, *_: (i, 0)), row, row,
                      pl.BlockSpec(memory_space=pl.ANY)],
            out_specs=tile,
            scratch_shapes=[pltpu.VMEM((2, 2, tm, d), F32), pltpu.SemaphoreType.DMA((2,))]),
        out_shape=jax.ShapeDtypeStruct((n, d), F32),
        compiler_params=_params(("arbitrary",)), name="combine_ln2",
    )(dest1, dest2, x1, cw_col, ln_g, ln_b, ys)


def _retention_consts():
    h, c = RET_HEADS, CHUNK
    log_gamma = jnp.log1p(-jnp.exp2(-5.0 - jnp.arange(h, dtype=F32)))
    idx = jnp.arange(c, dtype=F32)
    rel = idx[:, None] - idx[None, :]
    mask = jnp.where(rel >= 0, jnp.exp(log_gamma[:, None, None] * jnp.maximum(rel, 0.0)), 0.0)
    xi = jnp.exp(log_gamma[:, None] * (idx + 1.0))
    zeta = jnp.exp(log_gamma[:, None] * (c - 1.0 - idx))
    chunk_decay = jnp.exp(log_gamma * c)
    xi_b = jnp.broadcast_to(xi[:, :, None], (h, c, RET_QK_DIM))
    zeta_b = jnp.broadcast_to(zeta[:, :, None], (h, c, RET_QK_DIM))
    cd_b = jnp.broadcast_to(chunk_decay[:, None, None], (h, 1, RET_V_DIM))
    return mask, xi_b, zeta_b, cd_b


def _rope_consts():
    half = RET_QK_DIM // 2
    freq = ROPE_THETA ** (-jnp.arange(half, dtype=F32) / half)
    freq2 = jnp.concatenate([freq, freq])[None, :]
    sign2 = jnp.concatenate([-jnp.ones((half,), F32), jnp.ones((half,), F32)])[None, :]
    return freq2, sign2


def kernel(x, positions, w_in, b_gate, ret_gn_g, sgu_ln_g, sgu_ln_b, sgu_w, sgu_b, w_proj_ret, w_proj_sgu,
           w_out, ln1_g, ln1_b, w_group, b_group, w_er, b_er, w1, w3, w2, ln2_g, ln2_b):
    batch, seq, d = x.shape
    n = batch * seq
    qk_w, v_w = RET_HEADS * RET_QK_DIM, RET_HEADS * RET_V_DIM
    assert d == v_w == SGU_GROUPS * CHUNK and seq % CHUNK == 0
    ret_cols = 2 * qk_w + 2 * v_w
    max_tiles = (2 * n) // EXPERT_TILE + N_EXPERTS

    freq2, sign2 = _rope_consts()
    cos2, sin2 = _rope_table(positions.reshape(n, 1), freq2, sign2)
    ret_consts = _retention_consts()

    xc = x.reshape(n, d)
    for l in range(w_in.shape[0]):
        wl = w_in[l]
        parts = [wl[:, 0:qk_w].reshape(d, RET_HEADS, RET_QK_DIM),
                 wl[:, qk_w:2 * qk_w].reshape(d, RET_HEADS, RET_QK_DIM),
                 wl[:, 2 * qk_w:2 * qk_w + v_w].reshape(d, RET_HEADS, RET_V_DIM),
                 wl[:, 2 * qk_w + v_w:ret_cols].reshape(d, RET_HEADS, RET_V_DIM)]
        wa = jnp.concatenate(parts, axis=2).transpose(1, 0, 2).astype(BF16)
        y_a = _retention_branch(xc, cos2, sin2, wa, w_proj_ret[l].astype(BF16), ret_consts,
                                ret_gn_g[l][None, :], batch, seq)
        h = _in_proj(xc, wl[:, ret_cols:].astype(BF16))
        bias_b = jnp.broadcast_to(sgu_b[l][:, :, None], (SGU_GROUPS, CHUNK, CHUNK))
        y_sgu = _spatial_gating(h, sgu_ln_g[l][None, :], sgu_ln_b[l][None, :], sgu_w[l], bias_b, 0, 1)
        merged = _merge(y_a, y_sgu, h, b_gate[l], w_proj_sgu[l].astype(BF16), 2, 3)
        x1 = _out_proj(merged, xc, w_out[l].astype(BF16), ln1_g[l][None, :], ln1_b[l][None, :])

        w_r = jnp.concatenate([w_er[l], w_group[l]], axis=1).T
        w_r = jnp.pad(w_r, ((0, ROUTER_ROWS - w_r.shape[0]), (0, 0)))
        w_hi = w_r.astype(BF16)
        w_lo = (w_r - w_hi.astype(F32)).astype(BF16)
        bias = jnp.pad(jnp.concatenate([b_er[l], b_group[l]]), (0, ROUTER_ROWS - N_EXPERTS - N_GROUPS))[:, None]
        ids, cw, cnt = _router(x1, w_hi, w_lo, bias)

        counts = cnt[:, 0].astype(jnp.int32)
        dest1, dest2, tile_exp, tile_src, n_valid, zero_fill = _routing_tables(ids, counts, max_tiles)
        xs = _dispatch(x1, dest1, dest2, zero_fill, max_tiles)
        ys = _experts(xs, tile_exp, tile_src, n_valid, w1[l], w3[l], w2[l])
        xc = _combine(ys, x1, dest1, dest2, cw.T, ln2_g[l][None, :], ln2_b[l][None, :])
    return xc.reshape(batch, seq, d)
```

```python
import functools

import numpy as np
import jax
import jax.numpy as jnp
from jax import lax
from jax.experimental import pallas as pl
from jax.experimental.pallas import tpu as pltpu

F32 = jnp.float32
BF16 = jnp.bfloat16

RET_HEADS = 8
RET_QK_DIM = 128
RET_V_DIM = 256
CHUNK = 128
ROPE_THETA = 10000.0
SGU_GROUPS = 16
N_GROUPS = 4
EXPERTS_PER_GROUP = 8
N_EXPERTS = N_GROUPS * EXPERTS_PER_GROUP
ROUTER_ROWS = 40
LN_EPS = 1e-5
DEPTH = 1
DN_ALPHA = (2 * DEPTH) ** 0.25
SQRT_HALF = np.sqrt(0.5).astype(np.float32)

LANES = 128
VMEM_LIMIT = 52 * 1024 * 1024
EXPERT_TILE = 256


def _params(sem, vmem=VMEM_LIMIT):
    return pltpu.CompilerParams(dimension_semantics=sem, vmem_limit_bytes=vmem)


def _dot(a, b):
    return jnp.dot(a, b, preferred_element_type=F32)


def _dot_nt(a, b):
    return lax.dot_general(a, b, (((1,), (1,)), ((), ())), preferred_element_type=F32)


def _dot_tn(a, b):
    return lax.dot_general(a, b, (((0,), (0,)), ((), ())), preferred_element_type=F32)


def _layer_norm(r, g, b):
    mu = jnp.mean(r, axis=-1, keepdims=True)
    d = r - mu
    var = jnp.mean(d * d, axis=-1, keepdims=True)
    return d * lax.rsqrt(var + LN_EPS) * g + b


def _gelu(x):
    return 0.5 * x * (1.0 + lax.erf(x * SQRT_HALF))


def _rope_body(pos_ref, freq_ref, sign_ref, cos_ref, sin_ref):
    ang = pos_ref[...].astype(F32) * freq_ref[...]
    cos_ref[...] = jnp.cos(ang)
    sin_ref[...] = jnp.sin(ang) * sign_ref[...]


def _rope_table(pos_col, freq2, sign2):
    n = pos_col.shape[0]
    tm = min(1024, n)
    row = pl.BlockSpec((1, LANES), lambda i: (0, 0))
    out = pl.BlockSpec((tm, LANES), lambda i: (i, 0))
    return pl.pallas_call(
        _rope_body, grid=(n // tm,),
        in_specs=[pl.BlockSpec((tm, 1), lambda i: (i, 0)), row, row],
        out_specs=[out, out],
        out_shape=[jax.ShapeDtypeStruct((n, LANES), F32)] * 2,
        compiler_params=_params(("arbitrary",)), name="rope_table",
    )(pos_col, freq2, sign2)


def _inproj_body(x_ref, w_ref, o_ref, xb_ref):
    @pl.when(pl.program_id(1) == 0)
    def _():
        xb_ref[...] = x_ref[...].astype(BF16)

    o_ref[...] = _dot(xb_ref[...], w_ref[...]).astype(BF16)


def _in_proj(x2, w_bf):
    n, d = x2.shape
    width = w_bf.shape[1]
    tm, tn = min(1024, n), 1024
    return pl.pallas_call(
        _inproj_body, grid=(n // tm, width // tn),
        in_specs=[pl.BlockSpec((tm, d), lambda i, j: (i, 0)),
                  pl.BlockSpec((d, tn), lambda i, j: (0, j))],
        out_specs=pl.BlockSpec((tm, tn), lambda i, j: (i, j)),
        out_shape=jax.ShapeDtypeStruct((n, width), BF16),
        scratch_shapes=[pltpu.VMEM((tm, d), BF16)],
        compiler_params=_params(("arbitrary", "arbitrary")), name="in_proj",
    )(x2, w_bf)


HEAD_COLS = 2 * RET_QK_DIM + 2 * RET_V_DIM


def _retention_body(x_ref, cos_ref, sin_ref, wa_ref, wp_ref, mask_ref, xi_ref, zeta_ref, cd_ref, gain_ref,
                    o_ref, state_ref, yg_ref, *, n_chunks):
    @pl.when(pl.program_id(1) == 0)
    def _():
        state_ref[...] = jnp.zeros_like(state_ref)

    scale = RET_QK_DIM ** -0.5
    half = RET_QK_DIM // 2
    dk, dv = RET_QK_DIM, RET_V_DIM
    xb = x_ref[...].astype(BF16)
    for hd in range(RET_HEADS):
        proj = _dot(xb, wa_ref[hd])
        gain = gain_ref[:, hd * dv:(hd + 1) * dv]
        for c in range(n_chunks):
            lo, hi = c * CHUNK, (c + 1) * CHUNK
            cos, sin = cos_ref[lo:hi, :], sin_ref[lo:hi, :]
            q, k = proj[lo:hi, 0:dk], proj[lo:hi, dk:2 * dk]
            vb = proj[lo:hi, 2 * dk:2 * dk + dv].astype(BF16)
            g = proj[lo:hi, 2 * dk + dv:]
            qr = q * cos + pltpu.roll(q, half, 1) * sin
            kr = (k * cos + pltpu.roll(k, half, 1) * sin) * scale
            scores = _dot_nt(qr.astype(BF16), kr.astype(BF16)) * mask_ref[hd]
            state = state_ref[hd]
            lhs = jnp.concatenate([scores.astype(BF16), (qr * xi_ref[hd]).astype(BF16)], axis=1)
            rhs = jnp.concatenate([vb, state.astype(BF16)], axis=0)
            out = _dot(lhs, rhs)
            state_ref[hd] = state * cd_ref[hd] + _dot_tn((kr * zeta_ref[hd]).astype(BF16), vb)
            mu = jnp.mean(out, axis=-1, keepdims=True)
            d = out - mu
            var = jnp.mean(d * d, axis=-1, keepdims=True)
            yn = d * lax.rsqrt(var + LN_EPS) * gain
            yg_ref[lo:hi, hd * dv:(hd + 1) * dv] = (g * jax.nn.sigmoid(g) * yn).astype(BF16)
    o_ref[...] = _dot(yg_ref[...], wp_ref[...]).astype(BF16)


def _retention_branch(x2, cos2, sin2, wa_bf, wp_bf, consts, gain, batch, seq):
    n, d = x2.shape
    tr = min(256, seq)
    nr = seq // tr
    mask, xi_b, zeta_b, cd_b = consts
    vw = RET_HEADS * RET_V_DIM

    def once(shape):
        return pl.BlockSpec(shape, lambda b, r: (0,) * len(shape), pipeline_mode=pl.Buffered(1))

    rows = lambda w: pl.BlockSpec((tr, w), lambda b, r: (b * nr + r, 0))
    return pl.pallas_call(
        functools.partial(_retention_body, n_chunks=tr // CHUNK),
        grid=(batch, nr),
        in_specs=[rows(d), rows(LANES), rows(LANES),
                  once((RET_HEADS, d, HEAD_COLS)), once((vw, d)),
                  once((RET_HEADS, CHUNK, CHUNK)), once((RET_HEADS, CHUNK, RET_QK_DIM)),
                  once((RET_HEADS, CHUNK, RET_QK_DIM)), once((RET_HEADS, 1, RET_V_DIM)), once((1, vw))],
        out_specs=rows(d),
        out_shape=jax.ShapeDtypeStruct((n, d), BF16),
        scratch_shapes=[pltpu.VMEM((RET_HEADS, RET_QK_DIM, RET_V_DIM), F32), pltpu.VMEM((tr, vw), BF16)],
        compiler_params=_params(("arbitrary", "arbitrary")), name="retention_branch",
    )(x2, cos2, sin2, wa_bf, wp_bf, mask, xi_b, zeta_b, cd_b, gain)


def _sgu_body(u_ref, v_ref, lng_ref, lnb_ref, ws_ref, bs_ref, o_ref, wtril_ref, *, n_chunks):
    @pl.when(pl.program_id(0) == 0)
    def _():
        r = lax.broadcasted_iota(jnp.int32, (CHUNK, CHUNK), 0)
        c = lax.broadcasted_iota(jnp.int32, (CHUNK, CHUNK), 1)
        for g in range(SGU_GROUPS):
            wtril_ref[g] = jnp.where(r >= c, ws_ref[g], 0.0).astype(BF16)

    v = _gelu(v_ref[...].astype(F32))
    vn = _layer_norm(v, lng_ref[...], lnb_ref[...]).astype(BF16)
    for g in range(SGU_GROUPS):
        cols = pl.ds(g * CHUNK, CHUNK)
        lo = g * CHUNK
        rhs = jnp.concatenate([vn[c * CHUNK:(c + 1) * CHUNK, lo:lo + CHUNK] for c in range(n_chunks)], axis=1)
        mix = _dot(wtril_ref[g], rhs)
        bias = bs_ref[g]
        for c in range(n_chunks):
            rows = pl.ds(c * CHUNK, CHUNK)
            u = _gelu(u_ref[rows, cols].astype(F32))
            o_ref[rows, cols] = (u * (mix[:, c * CHUNK:(c + 1) * CHUNK] + bias)).astype(BF16)


def _spatial_gating(h, ln_g, ln_b, w_s, b_s_b, u_block, v_block):
    n = h.shape[0]
    width = SGU_GROUPS * CHUNK
    ts = min(256, n)
    row = pl.BlockSpec((1, width), lambda i: (0, 0))
    full3 = pl.BlockSpec((SGU_GROUPS, CHUNK, CHUNK), lambda i: (0, 0, 0))
    return pl.pallas_call(
        functools.partial(_sgu_body, n_chunks=ts // CHUNK),
        grid=(n // ts,),
        in_specs=[pl.BlockSpec((ts, width), lambda i: (i, u_block)),
                  pl.BlockSpec((ts, width), lambda i: (i, v_block)),
                  row, row, full3, full3],
        out_specs=pl.BlockSpec((ts, width), lambda i: (i, 0)),
        out_shape=jax.ShapeDtypeStruct((n, width), BF16),
        scratch_shapes=[pltpu.VMEM((SGU_GROUPS, CHUNK, CHUNK), BF16)],
        compiler_params=_params(("arbitrary",)), name="spatial_gating",
    )(h, h, ln_g, ln_b, w_s, b_s_b)


def _merge_body(ya_ref, ys_ref, ga_ref, gb_ref, bg_ref, ws_ref, o_ref):
    yb = _dot(ys_ref[...], ws_ref[...])
    sa = jax.nn.sigmoid(ga_ref[...].astype(F32) + bg_ref[0:1, :])
    sb = jax.nn.sigmoid(gb_ref[...].astype(F32) + bg_ref[1:2, :])
    o_ref[...] = (sa * ya_ref[...].astype(F32) + sb * yb).astype(BF16)


def _merge(y_a, y_sgu, h, b_gate, ws_bf, ga_block, gb_block):
    n, d = y_a.shape
    tm = min(512, n)
    tile = lambda blk: pl.BlockSpec((tm, d), lambda i: (i, blk))
    resident = pl.BlockSpec((d, d), lambda i: (0, 0), pipeline_mode=pl.Buffered(1))
    return pl.pallas_call(
        _merge_body, grid=(n // tm,),
        in_specs=[tile(0), tile(0), tile(ga_block), tile(gb_block),
                  pl.BlockSpec((2, d), lambda i: (0, 0)), resident],
        out_specs=tile(0),
        out_shape=jax.ShapeDtypeStruct((n, d), BF16),
        compiler_params=_params(("arbitrary",)), name="merge",
    )(y_a, y_sgu, h, h, b_gate, ws_bf)


def _outproj_body(m_ref, x_ref, w_ref, g_ref, b_ref, o_ref):
    r = DN_ALPHA * x_ref[...] + _dot(m_ref[...], w_ref[...])
    o_ref[...] = _layer_norm(r, g_ref[...], b_ref[...])


def _out_proj(merged, x2, wo_bf, ln_g, ln_b):
    n, d = x2.shape
    tm = min(512, n)
    tile = pl.BlockSpec((tm, d), lambda i: (i, 0))
    row = pl.BlockSpec((1, d), lambda i: (0, 0))
    return pl.pallas_call(
        _outproj_body, grid=(n // tm,),
        in_specs=[tile, tile, pl.BlockSpec((d, d), lambda i: (0, 0), pipeline_mode=pl.Buffered(1)), row, row],
        out_specs=tile,
        out_shape=jax.ShapeDtypeStruct((n, d), F32),
        compiler_params=_params(("arbitrary",)), name="out_proj_ln1",
    )(merged, x2, wo_bf, ln_g, ln_b)


def _router_body(x_ref, whi_ref, wlo_ref, b_ref, ids_ref, cw_ref, cnt_ref, carry_ref, tri_ref, *, tm):
    @pl.when(pl.program_id(0) == 0)
    def _():
        carry_ref[...] = jnp.zeros_like(carry_ref)
        r = lax.broadcasted_iota(jnp.int32, (tm, tm), 0)
        c = lax.broadcasted_iota(jnp.int32, (tm, tm), 1)
        tri_ref[...] = jnp.where(r < c, 1.0, 0.0).astype(BF16)

    x = x_ref[...]
    hi = x.astype(BF16)
    lo = (x - hi.astype(F32)).astype(BF16)
    whi = whi_ref[...]
    logits = _dot_nt(whi, hi) + _dot_nt(whi, lo) + _dot_nt(wlo_ref[...], hi) + b_ref[...]
    el = logits[0:N_EXPERTS, :]
    gl = logits[N_EXPERTS:N_EXPERTS + N_GROUPS, :]

    gi = lax.broadcasted_iota(jnp.int32, gl.shape, 0)
    gmax = jnp.max(gl, axis=0, keepdims=True)
    gidx = jnp.min(jnp.where(gl == gmax, gi, N_GROUPS), axis=0, keepdims=True)
    p_group = 1.0 / jnp.sum(jnp.exp(gl - gmax), axis=0, keepdims=True)

    ei = lax.broadcasted_iota(jnp.int32, el.shape, 0)
    first = gidx * EXPERTS_PER_GROUP
    in_group = (ei >= first) & (ei < first + EXPERTS_PER_GROUP)
    m1 = jnp.where(in_group, el, -jnp.inf)
    v1 = jnp.max(m1, axis=0, keepdims=True)
    i1 = jnp.min(jnp.where(in_group & (m1 == v1), ei, N_EXPERTS), axis=0, keepdims=True)
    rest = in_group & (ei != i1)
    m2 = jnp.where(rest, el, -jnp.inf)
    v2 = jnp.max(m2, axis=0, keepdims=True)
    i2 = jnp.min(jnp.where(rest & (m2 == v2), ei, N_EXPERTS), axis=0, keepdims=True)
    t = jnp.exp(v2 - v1)
    p1 = 1.0 / (1.0 + t)
    p2 = t * p1

    sel1, sel2 = ei == i1, ei == i2
    onehot = jnp.where(sel1 | sel2, 1.0, 0.0)
    before = _dot(onehot.astype(BF16), tri_ref[...]) + carry_ref[:, 0:1]
    r1 = jnp.sum(jnp.where(sel1, before, 0.0), axis=0, keepdims=True)
    r2 = jnp.sum(jnp.where(sel2, before, 0.0), axis=0, keepdims=True)
    carry_ref[...] = carry_ref[...] + jnp.sum(onehot, axis=1, keepdims=True)

    ids_ref[0:1, :] = i1
    ids_ref[1:2, :] = i2
    ids_ref[2:3, :] = r1.astype(jnp.int32)
    ids_ref[3:4, :] = r2.astype(jnp.int32)
    cw_ref[0:1, :] = p_group * p1
    cw_ref[1:2, :] = p_group * p2
    cnt_ref[...] = carry_ref[...]


def _router(x1, w_hi, w_lo, bias_col):
    n, d = x1.shape
    tm = min(512, n)
    full = pl.BlockSpec((ROUTER_ROWS, d), lambda i: (0, 0))
    return pl.pallas_call(
        functools.partial(_router_body, tm=tm), grid=(n // tm,),
        in_specs=[pl.BlockSpec((tm, d), lambda i: (i, 0)), full, full,
                  pl.BlockSpec((ROUTER_ROWS, 1), lambda i: (0, 0))],
        out_specs=[pl.BlockSpec((4, tm), lambda i: (0, i)),
                   pl.BlockSpec((2, tm), lambda i: (0, i)),
                   pl.BlockSpec((N_EXPERTS, LANES), lambda i: (0, 0))],
        out_shape=[jax.ShapeDtypeStruct((4, n), jnp.int32),
                   jax.ShapeDtypeStruct((2, n), F32),
                   jax.ShapeDtypeStruct((N_EXPERTS, LANES), F32)],
        scratch_shapes=[pltpu.VMEM((N_EXPERTS, LANES), F32), pltpu.VMEM((tm, tm), BF16)],
        compiler_params=_params(("arbitrary",)), name="router",
    )(x1, w_hi, w_lo, bias_col)


def _routing_tables(ids, counts, n, max_pairs):
    te = EXPERT_TILE
    n_tiles = 2 * n // te
    i32 = jnp.int32
    row_end = jnp.cumsum(counts)
    row_start = row_end - counts
    dest1 = row_start[ids[0]] + ids[2]
    dest2 = row_start[ids[1]] + ids[3]
    ne_cum = jnp.cumsum((counts > 0).astype(i32))
    tile_lo = jnp.arange(n_tiles, dtype=i32) * te
    first_e = jnp.sum((row_end[None, :] <= tile_lo[:, None]).astype(i32), axis=1)
    last_e = jnp.sum((row_end[None, :] <= (tile_lo + te - 1)[:, None]).astype(i32), axis=1)
    pairs_t = ne_cum[last_e] - ne_cum[first_e] + 1
    pair_end = jnp.cumsum(pairs_t)
    pair_start = pair_end - pairs_t
    n_pairs = pair_end[-1]
    p = jnp.minimum(jnp.arange(max_pairs, dtype=i32), n_pairs - 1)
    p_tile = jnp.sum((pair_end[None, :] <= p[:, None]).astype(i32), axis=1)
    rank = ne_cum[first_e[p_tile]] - 1 + (p - pair_start[p_tile])
    p_exp = jnp.sum((ne_cum[None, :] <= rank[:, None]).astype(i32), axis=1)
    p_lo = jnp.maximum(row_start[p_exp], p_tile * te) - p_tile * te
    p_hi = jnp.minimum(row_end[p_exp], (p_tile + 1) * te) - p_tile * te
    return (dest1, dest2, p_tile.astype(i32), p_exp.astype(i32), p_lo.astype(i32), p_hi.astype(i32),
            n_pairs.reshape(1).astype(i32))


def _row_copy(src, dst, sem):
    return pltpu.make_async_copy(src, dst, sem)


def _expert_body(ptile_ref, pexp_ref, plo_ref, phi_ref, np_ref, d1_ref, d2_ref,
                 x1_hbm, w1_ref, w3_ref, w2_ref, y2_hbm,
                 w1b, w3b, w2b, xbuf0, xbuf1, ybuf0, ybuf1, tok_s, inv_s, gsem, ssem, *, n, n_tiles):
    te = EXPERT_TILE
    p = pl.program_id(0)
    valid = p < np_ref[0]
    t = ptile_ref[p]
    pm1 = jnp.maximum(p - 1, 0)
    first = (p == 0) | (ptile_ref[pm1] != t)
    new_expert = (p == 0) | (pexp_ref[pm1] != pexp_ref[p])
    lo, hi = plo_ref[p], phi_ref[p]

    def gather(tile, dst, j):
        _row_copy(x1_hbm.at[pl.ds(tok_s[tile * te + j], 1)], dst.at[pl.ds(j, 1)], gsem).start()

    def scatter(tile, src, j):
        _row_copy(src.at[pl.ds(j, 1)], y2_hbm.at[pl.ds(inv_s[tile * te + j], 1)], ssem).start()

    def wait_gather(dst):
        _row_copy(x1_hbm.at[pl.ds(0, te)], dst, gsem).wait()

    def wait_scatter(src):
        _row_copy(src, y2_hbm.at[pl.ds(0, te)], ssem).wait()

    def issue_loop(fn, tile, buf):
        def one(j, c):
            fn(tile, buf, j)
            return c
        lax.fori_loop(0, te, one, 0, unroll=8)

    @pl.when(p == 0)
    def _():
        def invert(tk, c):
            a, b = d1_ref[tk], d2_ref[tk]
            tok_s[a] = tk
            inv_s[a] = tk
            tok_s[b] = tk
            inv_s[b] = tk + n
            return c
        lax.fori_loop(0, n, invert, 0, unroll=8)
        ybuf0[...] = jnp.zeros_like(ybuf0)
        ybuf1[...] = jnp.zeros_like(ybuf1)
        issue_loop(gather, 0, xbuf0)

    @pl.when(valid & new_expert)
    def _():
        w1b[...] = w1_ref[0].astype(BF16)
        w3b[...] = w3_ref[0].astype(BF16)
        w2b[...] = w2_ref[0].astype(BF16)

    def ffn(xb_ref, yb_ref):
        xb = xb_ref[...].astype(BF16)
        a = _dot(xb, w1b[...])
        b = _dot(xb, w3b[...])
        y = _dot((a * jax.nn.sigmoid(a) * b).astype(BF16), w2b[...])
        rows = lax.broadcasted_iota(jnp.int32, (te, 1), 0)
        yb_ref[...] = jnp.where((rows >= lo) & (rows < hi), y, yb_ref[...])

    for par, (xb, yb, xo, yo) in enumerate(((xbuf0, ybuf0, xbuf1, ybuf1), (xbuf1, ybuf1, xbuf0, ybuf0))):
        mine = valid & ((t & 1) == par)
        interior = first & (t >= 1) & (t < n_tiles - 1)

        @pl.when(mine & interior)
        def _():
            wait_gather(xb)

            @pl.when(t >= 2)
            def _():
                wait_scatter(yb)

            for j in range(te):
                gather(t + 1, xo, j)
                scatter(t - 1, yo, j)
            ffn(xb, yb)

        @pl.when(mine & jnp.logical_not(interior))
        def _():
            @pl.when(first & (t == 0))
            def _():
                wait_gather(xb)
                if n_tiles > 1:
                    issue_loop(gather, 1, xo)

            @pl.when(first & (t == n_tiles - 1) & (t >= 1))
            def _():
                wait_gather(xb)

                @pl.when(t >= 2)
                def _():
                    wait_scatter(yb)

                issue_loop(scatter, t - 1, yo)

            ffn(xb, yb)

    @pl.when(p == pl.num_programs(0) - 1)
    def _():
        last = n_tiles - 1
        y_last, y_prev = (ybuf0, ybuf1) if last % 2 == 0 else (ybuf1, ybuf0)
        if n_tiles >= 2:
            wait_scatter(y_prev)
        issue_loop(scatter, last, y_last)
        wait_scatter(y_last)


def _experts(x1, tables, w1, w3, w2):
    n, d = x1.shape
    de = w1.shape[2]
    te = EXPERT_TILE
    n_tiles = 2 * n // te
    max_pairs = n_tiles + N_EXPERTS - 1
    wmap = lambda p, pt, pe, *_: (pe[p], 0, 0)
    return pl.pallas_call(
        functools.partial(_expert_body, n=n, n_tiles=n_tiles),
        grid_spec=pltpu.PrefetchScalarGridSpec(
            num_scalar_prefetch=7, grid=(max_pairs,),
            in_specs=[pl.BlockSpec(memory_space=pl.ANY),
                      pl.BlockSpec((1, d, de), wmap), pl.BlockSpec((1, d, de), wmap), pl.BlockSpec((1, de, d), wmap)],
            out_specs=pl.BlockSpec(memory_space=pl.ANY),
            scratch_shapes=[pltpu.VMEM((d, de), BF16), pltpu.VMEM((d, de), BF16), pltpu.VMEM((de, d), BF16),
                            pltpu.VMEM((te, d), F32), pltpu.VMEM((te, d), F32),
                            pltpu.VMEM((te, d), F32), pltpu.VMEM((te, d), F32),
                            pltpu.SMEM((2 * n,), jnp.int32), pltpu.SMEM((2 * n,), jnp.int32),
                            pltpu.SemaphoreType.DMA(()), pltpu.SemaphoreType.DMA(())]),
        out_shape=jax.ShapeDtypeStruct((2 * n, d), F32),
        compiler_params=_params(("arbitrary",)), name="experts",
    )(*tables, x1, w1, w3, w2)


def _combine_body(x_ref, ya_ref, yb_ref, cw_ref, g_ref, b_ref, o_ref):
    y = cw_ref[:, 0:1] * ya_ref[...] + cw_ref[:, 1:2] * yb_ref[...]
    o_ref[...] = _layer_norm(DN_ALPHA * x_ref[...] + y, g_ref[...], b_ref[...])


def _combine(y2, x1, cw_col, ln_g, ln_b):
    n, d = x1.shape
    tm = min(512, n)
    nt = n // tm
    tile = pl.BlockSpec((tm, d), lambda i: (i, 0))
    row = pl.BlockSpec((1, d), lambda i: (0, 0))
    return pl.pallas_call(
        _combine_body, grid=(nt,),
        in_specs=[tile, tile, pl.BlockSpec((tm, d), lambda i: (i + nt, 0)),
                  pl.BlockSpec((tm, 2), lambda i: (i, 0)), row, row],
        out_specs=tile,
        out_shape=jax.ShapeDtypeStruct((n, d), F32),
        compiler_params=_params(("arbitrary",)), name="combine_ln2",
    )(x1, y2, y2, cw_col, ln_g, ln_b)


def _retention_consts():
    h, c = RET_HEADS, CHUNK
    log_gamma = jnp.log1p(-jnp.exp2(-5.0 - jnp.arange(h, dtype=F32)))
    idx = jnp.arange(c, dtype=F32)
    rel = idx[:, None] - idx[None, :]
    mask = jnp.where(rel >= 0, jnp.exp(log_gamma[:, None, None] * jnp.maximum(rel, 0.0)), 0.0)
    xi = jnp.exp(log_gamma[:, None] * (idx + 1.0))
    zeta = jnp.exp(log_gamma[:, None] * (c - 1.0 - idx))
    chunk_decay = jnp.exp(log_gamma * c)
    xi_b = jnp.broadcast_to(xi[:, :, None], (h, c, RET_QK_DIM))
    zeta_b = jnp.broadcast_to(zeta[:, :, None], (h, c, RET_QK_DIM))
    cd_b = jnp.broadcast_to(chunk_decay[:, None, None], (h, 1, RET_V_DIM))
    return mask, xi_b, zeta_b, cd_b


def _rope_consts():
    half = RET_QK_DIM // 2
    freq = ROPE_THETA ** (-jnp.arange(half, dtype=F32) / half)
    freq2 = jnp.concatenate([freq, freq])[None, :]
    sign2 = jnp.concatenate([-jnp.ones((half,), F32), jnp.ones((half,), F32)])[None, :]
    return freq2, sign2


def kernel(x, positions, w_in, b_gate, ret_gn_g, sgu_ln_g, sgu_ln_b, sgu_w, sgu_b, w_proj_ret, w_proj_sgu,
           w_out, ln1_g, ln1_b, w_group, b_group, w_er, b_er, w1, w3, w2, ln2_g, ln2_b):
    batch, seq, d = x.shape
    n = batch * seq
    qk_w, v_w = RET_HEADS * RET_QK_DIM, RET_HEADS * RET_V_DIM
    assert d == v_w == SGU_GROUPS * CHUNK and seq % CHUNK == 0
    ret_cols = 2 * qk_w + 2 * v_w
    max_pairs = (2 * n) // EXPERT_TILE + N_EXPERTS - 1

    freq2, sign2 = _rope_consts()
    cos2, sin2 = _rope_table(positions.reshape(n, 1), freq2, sign2)
    ret_consts = _retention_consts()

    xc = x.reshape(n, d)
    for l in range(w_in.shape[0]):
        wl = w_in[l]
        parts = [wl[:, 0:qk_w].reshape(d, RET_HEADS, RET_QK_DIM),
                 wl[:, qk_w:2 * qk_w].reshape(d, RET_HEADS, RET_QK_DIM),
                 wl[:, 2 * qk_w:2 * qk_w + v_w].reshape(d, RET_HEADS, RET_V_DIM),
                 wl[:, 2 * qk_w + v_w:ret_cols].reshape(d, RET_HEADS, RET_V_DIM)]
        wa = jnp.concatenate(parts, axis=2).transpose(1, 0, 2).astype(BF16)
        y_a = _retention_branch(xc, cos2, sin2, wa, w_proj_ret[l].astype(BF16), ret_consts,
                                ret_gn_g[l][None, :], batch, seq)
        h = _in_proj(xc, wl[:, ret_cols:].astype(BF16))
        bias_b = jnp.broadcast_to(sgu_b[l][:, :, None], (SGU_GROUPS, CHUNK, CHUNK))
        y_sgu = _spatial_gating(h, sgu_ln_g[l][None, :], sgu_ln_b[l][None, :], sgu_w[l], bias_b, 0, 1)
        merged = _merge(y_a, y_sgu, h, b_gate[l], w_proj_sgu[l].astype(BF16), 2, 3)
        x1 = _out_proj(merged, xc, w_out[l].astype(BF16), ln1_g[l][None, :], ln1_b[l][None, :])

        w_r = jnp.concatenate([w_er[l], w_group[l]], axis=1).T
        w_r = jnp.pad(w_r, ((0, ROUTER_ROWS - w_r.shape[0]), (0, 0)))
        w_hi = w_r.astype(BF16)
        w_lo = (w_r - w_hi.astype(F32)).astype(BF16)
        bias = jnp.pad(jnp.concatenate([b_er[l], b_group[l]]), (0, ROUTER_ROWS - N_EXPERTS - N_GROUPS))[:, None]
        ids, cw, cnt = _router(x1, w_hi, w_lo, bias)

        counts = cnt[:, 0].astype(jnp.int32)
        dest1, dest2, p_tile, p_exp, p_lo, p_hi, n_pairs = _routing_tables(ids, counts, n, max_pairs)
        y2 = _experts(x1, (p_tile, p_exp, p_lo, p_hi, n_pairs, dest1, dest2), w1[l], w3[l], w2[l])
        xc = _combine(y2, x1, cw.T, ln2_g[l][None, :], ln2_b[l][None, :])
    return xc.reshape(batch, seq, d)
```

```python
import functools

import numpy as np
import jax
import jax.numpy as jnp
from jax import lax
from jax.experimental import pallas as pl
from jax.experimental.pallas import tpu as pltpu

F32 = jnp.float32
BF16 = jnp.bfloat16

RET_HEADS = 8
RET_QK_DIM = 128
RET_V_DIM = 256
CHUNK = 128
ROPE_THETA = 10000.0
SGU_GROUPS = 16
N_GROUPS = 4
EXPERTS_PER_GROUP = 8
N_EXPERTS = N_GROUPS * EXPERTS_PER_GROUP
ROUTER_ROWS = 40
LN_EPS = 1e-5
DEPTH = 1
DN_ALPHA = (2 * DEPTH) ** 0.25
SQRT_HALF = np.sqrt(0.5).astype(np.float32)

LANES = 128
VMEM_LIMIT = 52 * 1024 * 1024
EXPERT_TILE = 256


def _params(sem, vmem=VMEM_LIMIT):
    return pltpu.CompilerParams(dimension_semantics=sem, vmem_limit_bytes=vmem)


def _dot(a, b):
    return jnp.dot(a, b, preferred_element_type=F32)


def _dot_nt(a, b):
    return lax.dot_general(a, b, (((1,), (1,)), ((), ())), preferred_element_type=F32)


def _dot_tn(a, b):
    return lax.dot_general(a, b, (((0,), (0,)), ((), ())), preferred_element_type=F32)


def _layer_norm(r, g, b):
    mu = jnp.mean(r, axis=-1, keepdims=True)
    d = r - mu
    var = jnp.mean(d * d, axis=-1, keepdims=True)
    return d * lax.rsqrt(var + LN_EPS) * g + b


def _gelu(x):
    return 0.5 * x * (1.0 + lax.erf(x * SQRT_HALF))


def _rope_body(pos_ref, freq_ref, sign_ref, cos_ref, sin_ref):
    ang = pos_ref[...].astype(F32) * freq_ref[...]
    cos_ref[...] = jnp.cos(ang)
    sin_ref[...] = jnp.sin(ang) * sign_ref[...]


def _rope_table(pos_col, freq2, sign2):
    n = pos_col.shape[0]
    tm = min(1024, n)
    row = pl.BlockSpec((1, LANES), lambda i: (0, 0))
    out = pl.BlockSpec((tm, LANES), lambda i: (i, 0))
    return pl.pallas_call(
        _rope_body, grid=(n // tm,),
        in_specs=[pl.BlockSpec((tm, 1), lambda i: (i, 0)), row, row],
        out_specs=[out, out],
        out_shape=[jax.ShapeDtypeStruct((n, LANES), F32)] * 2,
        compiler_params=_params(("arbitrary",)), name="rope_table",
    )(pos_col, freq2, sign2)


def _inproj_body(x_ref, w_ref, o_ref, xb_ref):
    @pl.when(pl.program_id(1) == 0)
    def _():
        xb_ref[...] = x_ref[...].astype(BF16)

    o_ref[...] = _dot(xb_ref[...], w_ref[...]).astype(BF16)


def _in_proj(x2, w_bf):
    n, d = x2.shape
    width = w_bf.shape[1]
    tm, tn = min(1024, n), 1024
    return pl.pallas_call(
        _inproj_body, grid=(n // tm, width // tn),
        in_specs=[pl.BlockSpec((tm, d), lambda i, j: (i, 0)),
                  pl.BlockSpec((d, tn), lambda i, j: (0, j))],
        out_specs=pl.BlockSpec((tm, tn), lambda i, j: (i, j)),
        out_shape=jax.ShapeDtypeStruct((n, width), BF16),
        scratch_shapes=[pltpu.VMEM((tm, d), BF16)],
        compiler_params=_params(("arbitrary", "arbitrary")), name="in_proj",
    )(x2, w_bf)


HEAD_COLS = 2 * RET_QK_DIM + 2 * RET_V_DIM


def _retention_body(x_ref, cos_ref, sin_ref, wa_ref, wp_ref, mask_ref, xi_ref, zeta_ref, cd_ref, gain_ref,
                    o_ref, state_ref, yg_ref, *, n_chunks):
    @pl.when(pl.program_id(1) == 0)
    def _():
        state_ref[...] = jnp.zeros_like(state_ref)

    scale = RET_QK_DIM ** -0.5
    half = RET_QK_DIM // 2
    dk, dv = RET_QK_DIM, RET_V_DIM
    xb = x_ref[...].astype(BF16)
    for hd in range(RET_HEADS):
        proj = _dot(xb, wa_ref[hd])
        gain = gain_ref[:, hd * dv:(hd + 1) * dv]
        for c in range(n_chunks):
            lo, hi = c * CHUNK, (c + 1) * CHUNK
            cos, sin = cos_ref[lo:hi, :], sin_ref[lo:hi, :]
            q, k = proj[lo:hi, 0:dk], proj[lo:hi, dk:2 * dk]
            vb = proj[lo:hi, 2 * dk:2 * dk + dv].astype(BF16)
            g = proj[lo:hi, 2 * dk + dv:]
            qr = q * cos + pltpu.roll(q, half, 1) * sin
            kr = (k * cos + pltpu.roll(k, half, 1) * sin) * scale
            scores = _dot_nt(qr.astype(BF16), kr.astype(BF16)) * mask_ref[hd]
            state = state_ref[hd]
            lhs = jnp.concatenate([scores.astype(BF16), (qr * xi_ref[hd]).astype(BF16)], axis=1)
            rhs = jnp.concatenate([vb, state.astype(BF16)], axis=0)
            out = _dot(lhs, rhs)
            state_ref[hd] = state * cd_ref[hd] + _dot_tn((kr * zeta_ref[hd]).astype(BF16), vb)
            mu = jnp.mean(out, axis=-1, keepdims=True)
            d = out - mu
            var = jnp.mean(d * d, axis=-1, keepdims=True)
            yn = d * lax.rsqrt(var + LN_EPS) * gain
            yg_ref[lo:hi, hd * dv:(hd + 1) * dv] = (g * jax.nn.sigmoid(g) * yn).astype(BF16)
    o_ref[...] = _dot(yg_ref[...], wp_ref[...]).astype(BF16)


def _retention_branch(x2, cos2, sin2, wa_bf, wp_bf, consts, gain, batch, seq):
    n, d = x2.shape
    tr = min(256, seq)
    nr = seq // tr
    mask, xi_b, zeta_b, cd_b = consts
    vw = RET_HEADS * RET_V_DIM

    def once(shape):
        return pl.BlockSpec(shape, lambda b, r: (0,) * len(shape), pipeline_mode=pl.Buffered(1))

    rows = lambda w: pl.BlockSpec((tr, w), lambda b, r: (b * nr + r, 0))
    return pl.pallas_call(
        functools.partial(_retention_body, n_chunks=tr // CHUNK),
        grid=(batch, nr),
        in_specs=[rows(d), rows(LANES), rows(LANES),
                  once((RET_HEADS, d, HEAD_COLS)), once((vw, d)),
                  once((RET_HEADS, CHUNK, CHUNK)), once((RET_HEADS, CHUNK, RET_QK_DIM)),
                  once((RET_HEADS, CHUNK, RET_QK_DIM)), once((RET_HEADS, 1, RET_V_DIM)), once((1, vw))],
        out_specs=rows(d),
        out_shape=jax.ShapeDtypeStruct((n, d), BF16),
        scratch_shapes=[pltpu.VMEM((RET_HEADS, RET_QK_DIM, RET_V_DIM), F32), pltpu.VMEM((tr, vw), BF16)],
        compiler_params=_params(("arbitrary", "arbitrary")), name="retention_branch",
    )(x2, cos2, sin2, wa_bf, wp_bf, mask, xi_b, zeta_b, cd_b, gain)


def _sgu_body(u_ref, v_ref, lng_ref, lnb_ref, ws_ref, bs_ref, o_ref, wtril_ref, *, n_chunks):
    @pl.when(pl.program_id(0) == 0)
    def _():
        r = lax.broadcasted_iota(jnp.int32, (CHUNK, CHUNK), 0)
        c = lax.broadcasted_iota(jnp.int32, (CHUNK, CHUNK), 1)
        for g in range(SGU_GROUPS):
            wtril_ref[g] = jnp.where(r >= c, ws_ref[g], 0.0).astype(BF16)

    v = _gelu(v_ref[...].astype(F32))
    vn = _layer_norm(v, lng_ref[...], lnb_ref[...]).astype(BF16)
    for g in range(SGU_GROUPS):
        cols = pl.ds(g * CHUNK, CHUNK)
        lo = g * CHUNK
        rhs = jnp.concatenate([vn[c * CHUNK:(c + 1) * CHUNK, lo:lo + CHUNK] for c in range(n_chunks)], axis=1)
        mix = _dot(wtril_ref[g], rhs)
        bias = bs_ref[g]
        for c in range(n_chunks):
            rows = pl.ds(c * CHUNK, CHUNK)
            u = _gelu(u_ref[rows, cols].astype(F32))
            o_ref[rows, cols] = (u * (mix[:, c * CHUNK:(c + 1) * CHUNK] + bias)).astype(BF16)


def _spatial_gating(h, ln_g, ln_b, w_s, b_s_b, u_block, v_block):
    n = h.shape[0]
    width = SGU_GROUPS * CHUNK
    ts = min(256, n)
    row = pl.BlockSpec((1, width), lambda i: (0, 0))
    full3 = pl.BlockSpec((SGU_GROUPS, CHUNK, CHUNK), lambda i: (0, 0, 0))
    return pl.pallas_call(
        functools.partial(_sgu_body, n_chunks=ts // CHUNK),
        grid=(n // ts,),
        in_specs=[pl.BlockSpec((ts, width), lambda i: (i, u_block)),
                  pl.BlockSpec((ts, width), lambda i: (i, v_block)),
                  row, row, full3, full3],
        out_specs=pl.BlockSpec((ts, width), lambda i: (i, 0)),
        out_shape=jax.ShapeDtypeStruct((n, width), BF16),
        scratch_shapes=[pltpu.VMEM((SGU_GROUPS, CHUNK, CHUNK), BF16)],
        compiler_params=_params(("arbitrary",)), name="spatial_gating",
    )(h, h, ln_g, ln_b, w_s, b_s_b)


def _merge_body(ya_ref, ys_ref, ga_ref, gb_ref, bg_ref, ws_ref, o_ref):
    yb = _dot(ys_ref[...], ws_ref[...])
    sa = jax.nn.sigmoid(ga_ref[...].astype(F32) + bg_ref[0:1, :])
    sb = jax.nn.sigmoid(gb_ref[...].astype(F32) + bg_ref[1:2, :])
    o_ref[...] = (sa * ya_ref[...].astype(F32) + sb * yb).astype(BF16)


def _merge(y_a, y_sgu, h, b_gate, ws_bf, ga_block, gb_block):
    n, d = y_a.shape
    tm = min(512, n)
    tile = lambda blk: pl.BlockSpec((tm, d), lambda i: (i, blk))
    resident = pl.BlockSpec((d, d), lambda i: (0, 0), pipeline_mode=pl.Buffered(1))
    return pl.pallas_call(
        _merge_body, grid=(n // tm,),
        in_specs=[tile(0), tile(0), tile(ga_block), tile(gb_block),
                  pl.BlockSpec((2, d), lambda i: (0, 0)), resident],
        out_specs=tile(0),
        out_shape=jax.ShapeDtypeStruct((n, d), BF16),
        compiler_params=_params(("arbitrary",)), name="merge",
    )(y_a, y_sgu, h, h, b_gate, ws_bf)


def _outproj_body(m_ref, x_ref, w_ref, g_ref, b_ref, o_ref, orow_ref):
    r = DN_ALPHA * x_ref[...] + _dot(m_ref[...], w_ref[...])
    x1 = _layer_norm(r, g_ref[...], b_ref[...])
    o_ref[...] = x1
    orow_ref[:, 0, :] = x1


def _out_proj(merged, x2, wo_bf, ln_g, ln_b):
    n, d = x2.shape
    tm = min(512, n)
    tile = pl.BlockSpec((tm, d), lambda i: (i, 0))
    row = pl.BlockSpec((1, d), lambda i: (0, 0))
    return pl.pallas_call(
        _outproj_body, grid=(n // tm,),
        in_specs=[tile, tile, pl.BlockSpec((d, d), lambda i: (0, 0), pipeline_mode=pl.Buffered(1)), row, row],
        out_specs=[tile, pl.BlockSpec((tm, 1, d), lambda i: (i, 0, 0))],
        out_shape=[jax.ShapeDtypeStruct((n, d), F32), jax.ShapeDtypeStruct((n, 1, d), F32)],
        compiler_params=_params(("arbitrary",)), name="out_proj_ln1",
    )(merged, x2, wo_bf, ln_g, ln_b)


def _router_body(x_ref, whi_ref, wlo_ref, b_ref, ids_ref, cw_ref, cnt_ref, carry_ref, tri_ref, *, tm):
    @pl.when(pl.program_id(0) == 0)
    def _():
        carry_ref[...] = jnp.zeros_like(carry_ref)
        r = lax.broadcasted_iota(jnp.int32, (tm, tm), 0)
        c = lax.broadcasted_iota(jnp.int32, (tm, tm), 1)
        tri_ref[...] = jnp.where(r < c, 1.0, 0.0).astype(BF16)

    x = x_ref[...]
    hi = x.astype(BF16)
    lo = (x - hi.astype(F32)).astype(BF16)
    whi = whi_ref[...]
    logits = _dot_nt(whi, hi) + _dot_nt(whi, lo) + _dot_nt(wlo_ref[...], hi) + b_ref[...]
    el = logits[0:N_EXPERTS, :]
    gl = logits[N_EXPERTS:N_EXPERTS + N_GROUPS, :]

    gi = lax.broadcasted_iota(jnp.int32, gl.shape, 0)
    gmax = jnp.max(gl, axis=0, keepdims=True)
    gidx = jnp.min(jnp.where(gl == gmax, gi, N_GROUPS), axis=0, keepdims=True)
    p_group = 1.0 / jnp.sum(jnp.exp(gl - gmax), axis=0, keepdims=True)

    ei = lax.broadcasted_iota(jnp.int32, el.shape, 0)
    first = gidx * EXPERTS_PER_GROUP
    in_group = (ei >= first) & (ei < first + EXPERTS_PER_GROUP)
    m1 = jnp.where(in_group, el, -jnp.inf)
    v1 = jnp.max(m1, axis=0, keepdims=True)
    i1 = jnp.min(jnp.where(in_group & (m1 == v1), ei, N_EXPERTS), axis=0, keepdims=True)
    rest = in_group & (ei != i1)
    m2 = jnp.where(rest, el, -jnp.inf)
    v2 = jnp.max(m2, axis=0, keepdims=True)
    i2 = jnp.min(jnp.where(rest & (m2 == v2), ei, N_EXPERTS), axis=0, keepdims=True)
    t = jnp.exp(v2 - v1)
    p1 = 1.0 / (1.0 + t)
    p2 = t * p1

    sel1, sel2 = ei == i1, ei == i2
    onehot = jnp.where(sel1 | sel2, 1.0, 0.0)
    before = _dot(onehot.astype(BF16), tri_ref[...]) + carry_ref[:, 0:1]
    r1 = jnp.sum(jnp.where(sel1, before, 0.0), axis=0, keepdims=True)
    r2 = jnp.sum(jnp.where(sel2, before, 0.0), axis=0, keepdims=True)
    carry_ref[...] = carry_ref[...] + jnp.sum(onehot, axis=1, keepdims=True)

    ids_ref[0:1, :] = i1
    ids_ref[1:2, :] = i2
    ids_ref[2:3, :] = r1.astype(jnp.int32)
    ids_ref[3:4, :] = r2.astype(jnp.int32)
    cw_ref[0:1, :] = p_group * p1
    cw_ref[1:2, :] = p_group * p2
    cnt_ref[...] = carry_ref[...]


def _router(x1, w_hi, w_lo, bias_col):
    n, d = x1.shape
    tm = min(512, n)
    full = pl.BlockSpec((ROUTER_ROWS, d), lambda i: (0, 0))
    return pl.pallas_call(
        functools.partial(_router_body, tm=tm), grid=(n // tm,),
        in_specs=[pl.BlockSpec((tm, d), lambda i: (i, 0)), full, full,
                  pl.BlockSpec((ROUTER_ROWS, 1), lambda i: (0, 0))],
        out_specs=[pl.BlockSpec((4, tm), lambda i: (0, i)),
                   pl.BlockSpec((2, tm), lambda i: (0, i)),
                   pl.BlockSpec((N_EXPERTS, LANES), lambda i: (0, 0))],
        out_shape=[jax.ShapeDtypeStruct((4, n), jnp.int32),
                   jax.ShapeDtypeStruct((2, n), F32),
                   jax.ShapeDtypeStruct((N_EXPERTS, LANES), F32)],
        scratch_shapes=[pltpu.VMEM((N_EXPERTS, LANES), F32), pltpu.VMEM((tm, tm), BF16)],
        compiler_params=_params(("arbitrary",)), name="router",
    )(x1, w_hi, w_lo, bias_col)


def _routing_tables(ids, counts, n, max_pairs):
    te = EXPERT_TILE
    n_tiles = 2 * n // te
    i32 = jnp.int32
    row_end = jnp.cumsum(counts)
    row_start = row_end - counts
    dest1 = row_start[ids[0]] + ids[2]
    dest2 = row_start[ids[1]] + ids[3]
    ne_cum = jnp.cumsum((counts > 0).astype(i32))
    tile_lo = jnp.arange(n_tiles, dtype=i32) * te
    first_e = jnp.sum((row_end[None, :] <= tile_lo[:, None]).astype(i32), axis=1)
    last_e = jnp.sum((row_end[None, :] <= (tile_lo + te - 1)[:, None]).astype(i32), axis=1)
    pairs_t = ne_cum[last_e] - ne_cum[first_e] + 1
    pair_end = jnp.cumsum(pairs_t)
    pair_start = pair_end - pairs_t
    n_pairs = pair_end[-1]
    p = jnp.minimum(jnp.arange(max_pairs, dtype=i32), n_pairs - 1)
    p_tile = jnp.sum((pair_end[None, :] <= p[:, None]).astype(i32), axis=1)
    rank = ne_cum[first_e[p_tile]] - 1 + (p - pair_start[p_tile])
    p_exp = jnp.sum((ne_cum[None, :] <= rank[:, None]).astype(i32), axis=1)
    p_lo = jnp.maximum(row_start[p_exp], p_tile * te) - p_tile * te
    p_hi = jnp.minimum(row_end[p_exp], (p_tile + 1) * te) - p_tile * te
    return (dest1, dest2, p_tile.astype(i32), p_exp.astype(i32), p_lo.astype(i32), p_hi.astype(i32),
            n_pairs.reshape(1).astype(i32))


def _row_copy(src, dst, sem):
    return pltpu.make_async_copy(src, dst, sem)


def _expert_body(ptile_ref, pexp_ref, plo_ref, phi_ref, np_ref, d1_ref, d2_ref,
                 x1_hbm, w1_ref, w3_ref, w2_ref, y2_hbm,
                 w1b, w3b, w2b, xbuf0, xbuf1, ybuf0, ybuf1, tok_s, inv_s, gsem, ssem, *, n, n_tiles):
    te = EXPERT_TILE
    p = pl.program_id(0)
    valid = p < np_ref[0]
    t = ptile_ref[p]
    pm1 = jnp.maximum(p - 1, 0)
    first = (p == 0) | (ptile_ref[pm1] != t)
    new_expert = (p == 0) | (pexp_ref[pm1] != pexp_ref[p])
    lo, hi = plo_ref[p], phi_ref[p]

    def gather(tile, dst, j):
        _row_copy(x1_hbm.at[tok_s[tile * te + j]], dst.at[pl.ds(j, 1)], gsem).start()

    def scatter(tile, src, j):
        _row_copy(src.at[pl.ds(j, 1)], y2_hbm.at[inv_s[tile * te + j]], ssem).start()

    def wait_gather(dst):
        _row_copy(x1_hbm.at[pl.ds(0, te), 0], dst, gsem).wait()

    def wait_scatter(src):
        _row_copy(src, y2_hbm.at[pl.ds(0, te), 0], ssem).wait()

    def issue_loop(fn, tile, buf):
        def one(j, c):
            fn(tile, buf, j)
            return c
        lax.fori_loop(0, te, one, 0, unroll=8)

    @pl.when(p == 0)
    def _():
        def invert(tk, c):
            a, b = d1_ref[tk], d2_ref[tk]
            tok_s[a] = tk
            inv_s[a] = tk
            tok_s[b] = tk
            inv_s[b] = tk + n
            return c
        lax.fori_loop(0, n, invert, 0, unroll=8)
        ybuf0[...] = jnp.zeros_like(ybuf0)
        ybuf1[...] = jnp.zeros_like(ybuf1)
        issue_loop(gather, 0, xbuf0)

    @pl.when(valid & new_expert)
    def _():
        w1b[...] = w1_ref[0].astype(BF16)
        w3b[...] = w3_ref[0].astype(BF16)
        w2b[...] = w2_ref[0].astype(BF16)

    def ffn(xb_ref, yb_ref):
        xb = xb_ref[...].astype(BF16)
        a = _dot(xb, w1b[...])
        b = _dot(xb, w3b[...])
        y = _dot((a * jax.nn.sigmoid(a) * b).astype(BF16), w2b[...])
        rows = lax.broadcasted_iota(jnp.int32, (te, 1), 0)
        yb_ref[...] = jnp.where((rows >= lo) & (rows < hi), y, yb_ref[...])

    for par, (xb, yb, xo, yo) in enumerate(((xbuf0, ybuf0, xbuf1, ybuf1), (xbuf1, ybuf1, xbuf0, ybuf0))):
        mine = valid & ((t & 1) == par)
        interior = first & (t >= 1) & (t < n_tiles - 1)

        @pl.when(mine & interior)
        def _():
            wait_gather(xb)

            @pl.when(t >= 2)
            def _():
                wait_scatter(yb)

            for j in range(te):
                gather(t + 1, xo, j)
                scatter(t - 1, yo, j)
            ffn(xb, yb)

        @pl.when(mine & jnp.logical_not(interior))
        def _():
            @pl.when(first & (t == 0))
            def _():
                wait_gather(xb)
                if n_tiles > 1:
                    issue_loop(gather, 1, xo)

            @pl.when(first & (t == n_tiles - 1) & (t >= 1))
            def _():
                wait_gather(xb)

                @pl.when(t >= 2)
                def _():
                    wait_scatter(yb)

                issue_loop(scatter, t - 1, yo)

            ffn(xb, yb)

    @pl.when(p == pl.num_programs(0) - 1)
    def _():
        last = n_tiles - 1
        y_last, y_prev = (ybuf0, ybuf1) if last % 2 == 0 else (ybuf1, ybuf0)
        if n_tiles >= 2:
            wait_scatter(y_prev)
        issue_loop(scatter, last, y_last)
        wait_scatter(y_last)


def _experts(x1_rows, tables, w1, w3, w2):
    n, _, d = x1_rows.shape
    de = w1.shape[2]
    te = EXPERT_TILE
    n_tiles = 2 * n // te
    max_pairs = n_tiles + N_EXPERTS - 1
    wmap = lambda p, pt, pe, *_: (pe[p], 0, 0)
    return pl.pallas_call(
        functools.partial(_expert_body, n=n, n_tiles=n_tiles),
        grid_spec=pltpu.PrefetchScalarGridSpec(
            num_scalar_prefetch=7, grid=(max_pairs,),
            in_specs=[pl.BlockSpec(memory_space=pl.ANY),
                      pl.BlockSpec((1, d, de), wmap), pl.BlockSpec((1, d, de), wmap), pl.BlockSpec((1, de, d), wmap)],
            out_specs=pl.BlockSpec(memory_space=pl.ANY),
            scratch_shapes=[pltpu.VMEM((d, de), BF16), pltpu.VMEM((d, de), BF16), pltpu.VMEM((de, d), BF16),
                            pltpu.VMEM((te, d), F32), pltpu.VMEM((te, d), F32),
                            pltpu.VMEM((te, d), F32), pltpu.VMEM((te, d), F32),
                            pltpu.SMEM((2 * n,), jnp.int32), pltpu.SMEM((2 * n,), jnp.int32),
                            pltpu.SemaphoreType.DMA(()), pltpu.SemaphoreType.DMA(())]),
        out_shape=jax.ShapeDtypeStruct((2 * n, 1, d), F32),
        compiler_params=_params(("arbitrary",)), name="experts",
    )(*tables, x1_rows, w1, w3, w2)


def _combine_body(x_ref, ya_ref, yb_ref, cw_ref, g_ref, b_ref, o_ref):
    y = cw_ref[:, 0:1] * ya_ref[:, 0, :] + cw_ref[:, 1:2] * yb_ref[:, 0, :]
    o_ref[...] = _layer_norm(DN_ALPHA * x_ref[...] + y, g_ref[...], b_ref[...])


def _combine(y2, x1, cw_col, ln_g, ln_b):
    n, d = x1.shape
    tm = min(512, n)
    nt = n // tm
    tile = pl.BlockSpec((tm, d), lambda i: (i, 0))
    row = pl.BlockSpec((1, d), lambda i: (0, 0))
    return pl.pallas_call(
        _combine_body, grid=(nt,),
        in_specs=[tile, pl.BlockSpec((tm, 1, d), lambda i: (i, 0, 0)),
                  pl.BlockSpec((tm, 1, d), lambda i: (i + nt, 0, 0)),
                  pl.BlockSpec((tm, 2), lambda i: (i, 0)), row, row],
        out_specs=tile,
        out_shape=jax.ShapeDtypeStruct((n, d), F32),
        compiler_params=_params(("arbitrary",)), name="combine_ln2",
    )(x1, y2, y2, cw_col, ln_g, ln_b)


def _retention_consts():
    h, c = RET_HEADS, CHUNK
    log_gamma = jnp.log1p(-jnp.exp2(-5.0 - jnp.arange(h, dtype=F32)))
    idx = jnp.arange(c, dtype=F32)
    rel = idx[:, None] - idx[None, :]
    mask = jnp.where(rel >= 0, jnp.exp(log_gamma[:, None, None] * jnp.maximum(rel, 0.0)), 0.0)
    xi = jnp.exp(log_gamma[:, None] * (idx + 1.0))
    zeta = jnp.exp(log_gamma[:, None] * (c - 1.0 - idx))
    chunk_decay = jnp.exp(log_gamma * c)
    xi_b = jnp.broadcast_to(xi[:, :, None], (h, c, RET_QK_DIM))
    zeta_b = jnp.broadcast_to(zeta[:, :, None], (h, c, RET_QK_DIM))
    cd_b = jnp.broadcast_to(chunk_decay[:, None, None], (h, 1, RET_V_DIM))
    return mask, xi_b, zeta_b, cd_b


def _rope_consts():
    half = RET_QK_DIM // 2
    freq = ROPE_THETA ** (-jnp.arange(half, dtype=F32) / half)
    freq2 = jnp.concatenate([freq, freq])[None, :]
    sign2 = jnp.concatenate([-jnp.ones((half,), F32), jnp.ones((half,), F32)])[None, :]
    return freq2, sign2


def kernel(x, positions, w_in, b_gate, ret_gn_g, sgu_ln_g, sgu_ln_b, sgu_w, sgu_b, w_proj_ret, w_proj_sgu,
           w_out, ln1_g, ln1_b, w_group, b_group, w_er, b_er, w1, w3, w2, ln2_g, ln2_b):
    batch, seq, d = x.shape
    n = batch * seq
    qk_w, v_w = RET_HEADS * RET_QK_DIM, RET_HEADS * RET_V_DIM
    assert d == v_w == SGU_GROUPS * CHUNK and seq % CHUNK == 0
    ret_cols = 2 * qk_w + 2 * v_w
    max_pairs = (2 * n) // EXPERT_TILE + N_EXPERTS - 1

    freq2, sign2 = _rope_consts()
    cos2, sin2 = _rope_table(positions.reshape(n, 1), freq2, sign2)
    ret_consts = _retention_consts()

    xc = x.reshape(n, d)
    for l in range(w_in.shape[0]):
        wl = w_in[l]
        parts = [wl[:, 0:qk_w].reshape(d, RET_HEADS, RET_QK_DIM),
                 wl[:, qk_w:2 * qk_w].reshape(d, RET_HEADS, RET_QK_DIM),
                 wl[:, 2 * qk_w:2 * qk_w + v_w].reshape(d, RET_HEADS, RET_V_DIM),
                 wl[:, 2 * qk_w + v_w:ret_cols].reshape(d, RET_HEADS, RET_V_DIM)]
        wa = jnp.concatenate(parts, axis=2).transpose(1, 0, 2).astype(BF16)
        y_a = _retention_branch(xc, cos2, sin2, wa, w_proj_ret[l].astype(BF16), ret_consts,
                                ret_gn_g[l][None, :], batch, seq)
        h = _in_proj(xc, wl[:, ret_cols:].astype(BF16))
        bias_b = jnp.broadcast_to(sgu_b[l][:, :, None], (SGU_GROUPS, CHUNK, CHUNK))
        y_sgu = _spatial_gating(h, sgu_ln_g[l][None, :], sgu_ln_b[l][None, :], sgu_w[l], bias_b, 0, 1)
        merged = _merge(y_a, y_sgu, h, b_gate[l], w_proj_sgu[l].astype(BF16), 2, 3)
        x1, x1_rows = _out_proj(merged, xc, w_out[l].astype(BF16), ln1_g[l][None, :], ln1_b[l][None, :])

        w_r = jnp.concatenate([w_er[l], w_group[l]], axis=1).T
        w_r = jnp.pad(w_r, ((0, ROUTER_ROWS - w_r.shape[0]), (0, 0)))
        w_hi = w_r.astype(BF16)
        w_lo = (w_r - w_hi.astype(F32)).astype(BF16)
        bias = jnp.pad(jnp.concatenate([b_er[l], b_group[l]]), (0, ROUTER_ROWS - N_EXPERTS - N_GROUPS))[:, None]
        ids, cw, cnt = _router(x1, w_hi, w_lo, bias)

        counts = cnt[:, 0].astype(jnp.int32)
        dest1, dest2, p_tile, p_exp, p_lo, p_hi, n_pairs = _routing_tables(ids, counts, n, max_pairs)
        y2 = _experts(x1_rows, (p_tile, p_exp, p_lo, p_hi, n_pairs, dest1, dest2), w1[l], w3[l], w2[l])
        xc = _combine(y2, x1, cw.T, ln2_g[l][None, :], ln2_b[l][None, :])
    return xc.reshape(batch, seq, d)
```

```python
import functools

import numpy as np
import jax
import jax.numpy as jnp
from jax import lax
from jax.experimental import pallas as pl
from jax.experimental.pallas import tpu as pltpu

F32 = jnp.float32
BF16 = jnp.bfloat16

RET_HEADS = 8
RET_QK_DIM = 128
RET_V_DIM = 256
CHUNK = 128
ROPE_THETA = 10000.0
SGU_GROUPS = 16
N_GROUPS = 4
EXPERTS_PER_GROUP = 8
N_EXPERTS = N_GROUPS * EXPERTS_PER_GROUP
ROUTER_ROWS = 40
LN_EPS = 1e-5
DEPTH = 1
DN_ALPHA = (2 * DEPTH) ** 0.25
SQRT_HALF = np.sqrt(0.5).astype(np.float32)

LANES = 128
VMEM_LIMIT = 52 * 1024 * 1024
EXPERT_TILE = 256


def _params(sem, vmem=VMEM_LIMIT):
    return pltpu.CompilerParams(dimension_semantics=sem, vmem_limit_bytes=vmem)


def _dot(a, b):
    return jnp.dot(a, b, preferred_element_type=F32)


def _dot_nt(a, b):
    return lax.dot_general(a, b, (((1,), (1,)), ((), ())), preferred_element_type=F32)


def _dot_tn(a, b):
    return lax.dot_general(a, b, (((0,), (0,)), ((), ())), preferred_element_type=F32)


def _layer_norm(r, g, b):
    mu = jnp.mean(r, axis=-1, keepdims=True)
    d = r - mu
    var = jnp.mean(d * d, axis=-1, keepdims=True)
    return d * lax.rsqrt(var + LN_EPS) * g + b


def _gelu(x):
    return 0.5 * x * (1.0 + lax.erf(x * SQRT_HALF))


def _rope_body(pos_ref, freq_ref, sign_ref, cos_ref, sin_ref):
    ang = pos_ref[...].astype(F32) * freq_ref[...]
    cos_ref[...] = jnp.cos(ang)
    sin_ref[...] = jnp.sin(ang) * sign_ref[...]


def _rope_table(pos_col, freq2, sign2):
    n = pos_col.shape[0]
    tm = min(1024, n)
    row = pl.BlockSpec((1, LANES), lambda i: (0, 0))
    out = pl.BlockSpec((tm, LANES), lambda i: (i, 0))
    return pl.pallas_call(
        _rope_body, grid=(n // tm,),
        in_specs=[pl.BlockSpec((tm, 1), lambda i: (i, 0)), row, row],
        out_specs=[out, out],
        out_shape=[jax.ShapeDtypeStruct((n, LANES), F32)] * 2,
        compiler_params=_params(("arbitrary",)), name="rope_table",
    )(pos_col, freq2, sign2)


def _inproj_body(x_ref, w_ref, o_ref, xb_ref):
    @pl.when(pl.program_id(1) == 0)
    def _():
        xb_ref[...] = x_ref[...].astype(BF16)

    o_ref[...] = _dot(xb_ref[...], w_ref[...]).astype(BF16)


def _in_proj(x2, w_bf):
    n, d = x2.shape
    width = w_bf.shape[1]
    tm, tn = min(1024, n), 1024
    return pl.pallas_call(
        _inproj_body, grid=(n // tm, width // tn),
        in_specs=[pl.BlockSpec((tm, d), lambda i, j: (i, 0)),
                  pl.BlockSpec((d, tn), lambda i, j: (0, j))],
        out_specs=pl.BlockSpec((tm, tn), lambda i, j: (i, j)),
        out_shape=jax.ShapeDtypeStruct((n, width), BF16),
        scratch_shapes=[pltpu.VMEM((tm, d), BF16)],
        compiler_params=_params(("arbitrary", "arbitrary")), name="in_proj",
    )(x2, w_bf)


HEAD_COLS = 2 * RET_QK_DIM + 2 * RET_V_DIM


def _retention_body(x_ref, cos_ref, sin_ref, wa_ref, wp_ref, mask_ref, xi_ref, zeta_ref, cd_ref, gain_ref,
                    o_ref, state_ref, yg_ref, *, n_chunks):
    @pl.when(pl.program_id(1) == 0)
    def _():
        state_ref[...] = jnp.zeros_like(state_ref)

    scale = RET_QK_DIM ** -0.5
    half = RET_QK_DIM // 2
    dk, dv = RET_QK_DIM, RET_V_DIM
    xb = x_ref[...].astype(BF16)
    for hd in range(RET_HEADS):
        proj = _dot(xb, wa_ref[hd])
        gain = gain_ref[:, hd * dv:(hd + 1) * dv]
        for c in range(n_chunks):
            lo, hi = c * CHUNK, (c + 1) * CHUNK
            cos, sin = cos_ref[lo:hi, :], sin_ref[lo:hi, :]
            q, k = proj[lo:hi, 0:dk], proj[lo:hi, dk:2 * dk]
            vb = proj[lo:hi, 2 * dk:2 * dk + dv].astype(BF16)
            g = proj[lo:hi, 2 * dk + dv:]
            qr = q * cos + pltpu.roll(q, half, 1) * sin
            kr = (k * cos + pltpu.roll(k, half, 1) * sin) * scale
            scores = _dot_nt(qr.astype(BF16), kr.astype(BF16)) * mask_ref[hd]
            state = state_ref[hd]
            lhs = jnp.concatenate([scores.astype(BF16), (qr * xi_ref[hd]).astype(BF16)], axis=1)
            rhs = jnp.concatenate([vb, state.astype(BF16)], axis=0)
            out = _dot(lhs, rhs)
            state_ref[hd] = state * cd_ref[hd] + _dot_tn((kr * zeta_ref[hd]).astype(BF16), vb)
            mu = jnp.mean(out, axis=-1, keepdims=True)
            d = out - mu
            var = jnp.mean(d * d, axis=-1, keepdims=True)
            yn = d * lax.rsqrt(var + LN_EPS) * gain
            yg_ref[lo:hi, hd * dv:(hd + 1) * dv] = (g * jax.nn.sigmoid(g) * yn).astype(BF16)
    o_ref[...] = _dot(yg_ref[...], wp_ref[...]).astype(BF16)


def _retention_branch(x2, cos2, sin2, wa_bf, wp_bf, consts, gain, batch, seq):
    n, d = x2.shape
    tr = min(256, seq)
    nr = seq // tr
    mask, xi_b, zeta_b, cd_b = consts
    vw = RET_HEADS * RET_V_DIM

    def once(shape):
        return pl.BlockSpec(shape, lambda b, r: (0,) * len(shape), pipeline_mode=pl.Buffered(1))

    rows = lambda w: pl.BlockSpec((tr, w), lambda b, r: (b * nr + r, 0))
    return pl.pallas_call(
        functools.partial(_retention_body, n_chunks=tr // CHUNK),
        grid=(batch, nr),
        in_specs=[rows(d), rows(LANES), rows(LANES),
                  once((RET_HEADS, d, HEAD_COLS)), once((vw, d)),
                  once((RET_HEADS, CHUNK, CHUNK)), once((RET_HEADS, CHUNK, RET_QK_DIM)),
                  once((RET_HEADS, CHUNK, RET_QK_DIM)), once((RET_HEADS, 1, RET_V_DIM)), once((1, vw))],
        out_specs=rows(d),
        out_shape=jax.ShapeDtypeStruct((n, d), BF16),
        scratch_shapes=[pltpu.VMEM((RET_HEADS, RET_QK_DIM, RET_V_DIM), F32), pltpu.VMEM((tr, vw), BF16)],
        compiler_params=_params(("arbitrary", "arbitrary")), name="retention_branch",
    )(x2, cos2, sin2, wa_bf, wp_bf, mask, xi_b, zeta_b, cd_b, gain)


def _sgu_body(u_ref, v_ref, lng_ref, lnb_ref, ws_ref, bs_ref, o_ref, wtril_ref, *, n_chunks):
    @pl.when(pl.program_id(0) == 0)
    def _():
        r = lax.broadcasted_iota(jnp.int32, (CHUNK, CHUNK), 0)
        c = lax.broadcasted_iota(jnp.int32, (CHUNK, CHUNK), 1)
        for g in range(SGU_GROUPS):
            wtril_ref[g] = jnp.where(r >= c, ws_ref[g], 0.0).astype(BF16)

    v = _gelu(v_ref[...].astype(F32))
    vn = _layer_norm(v, lng_ref[...], lnb_ref[...]).astype(BF16)
    for g in range(SGU_GROUPS):
        cols = pl.ds(g * CHUNK, CHUNK)
        lo = g * CHUNK
        rhs = jnp.concatenate([vn[c * CHUNK:(c + 1) * CHUNK, lo:lo + CHUNK] for c in range(n_chunks)], axis=1)
        mix = _dot(wtril_ref[g], rhs)
        bias = bs_ref[g]
        for c in range(n_chunks):
            rows = pl.ds(c * CHUNK, CHUNK)
            u = _gelu(u_ref[rows, cols].astype(F32))
            o_ref[rows, cols] = (u * (mix[:, c * CHUNK:(c + 1) * CHUNK] + bias)).astype(BF16)


def _spatial_gating(h, ln_g, ln_b, w_s, b_s_b, u_block, v_block):
    n = h.shape[0]
    width = SGU_GROUPS * CHUNK
    ts = min(256, n)
    row = pl.BlockSpec((1, width), lambda i: (0, 0))
    full3 = pl.BlockSpec((SGU_GROUPS, CHUNK, CHUNK), lambda i: (0, 0, 0))
    return pl.pallas_call(
        functools.partial(_sgu_body, n_chunks=ts // CHUNK),
        grid=(n // ts,),
        in_specs=[pl.BlockSpec((ts, width), lambda i: (i, u_block)),
                  pl.BlockSpec((ts, width), lambda i: (i, v_block)),
                  row, row, full3, full3],
        out_specs=pl.BlockSpec((ts, width), lambda i: (i, 0)),
        out_shape=jax.ShapeDtypeStruct((n, width), BF16),
        scratch_shapes=[pltpu.VMEM((SGU_GROUPS, CHUNK, CHUNK), BF16)],
        compiler_params=_params(("arbitrary",)), name="spatial_gating",
    )(h, h, ln_g, ln_b, w_s, b_s_b)


def _merge_body(ya_ref, ys_ref, ga_ref, gb_ref, bg_ref, ws_ref, o_ref):
    yb = _dot(ys_ref[...], ws_ref[...])
    sa = jax.nn.sigmoid(ga_ref[...].astype(F32) + bg_ref[0:1, :])
    sb = jax.nn.sigmoid(gb_ref[...].astype(F32) + bg_ref[1:2, :])
    o_ref[...] = (sa * ya_ref[...].astype(F32) + sb * yb).astype(BF16)


def _merge(y_a, y_sgu, h, b_gate, ws_bf, ga_block, gb_block):
    n, d = y_a.shape
    tm = min(512, n)
    tile = lambda blk: pl.BlockSpec((tm, d), lambda i: (i, blk))
    resident = pl.BlockSpec((d, d), lambda i: (0, 0), pipeline_mode=pl.Buffered(1))
    return pl.pallas_call(
        _merge_body, grid=(n // tm,),
        in_specs=[tile(0), tile(0), tile(ga_block), tile(gb_block),
                  pl.BlockSpec((2, d), lambda i: (0, 0)), resident],
        out_specs=tile(0),
        out_shape=jax.ShapeDtypeStruct((n, d), BF16),
        compiler_params=_params(("arbitrary",)), name="merge",
    )(y_a, y_sgu, h, h, b_gate, ws_bf)


def _outproj_body(m_ref, x_ref, w_ref, g_ref, b_ref, o_ref, orow_ref):
    r = DN_ALPHA * x_ref[...] + _dot(m_ref[...], w_ref[...])
    x1 = _layer_norm(r, g_ref[...], b_ref[...])
    o_ref[...] = x1
    orow_ref[:, 0, :] = x1


def _out_proj(merged, x2, wo_bf, ln_g, ln_b):
    n, d = x2.shape
    tm = min(512, n)
    tile = pl.BlockSpec((tm, d), lambda i: (i, 0))
    row = pl.BlockSpec((1, d), lambda i: (0, 0))
    return pl.pallas_call(
        _outproj_body, grid=(n // tm,),
        in_specs=[tile, tile, pl.BlockSpec((d, d), lambda i: (0, 0), pipeline_mode=pl.Buffered(1)), row, row],
        out_specs=[tile, pl.BlockSpec((tm, 1, d), lambda i: (i, 0, 0))],
        out_shape=[jax.ShapeDtypeStruct((n, d), F32), jax.ShapeDtypeStruct((n, 1, d), F32)],
        compiler_params=_params(("arbitrary",)), name="out_proj_ln1",
    )(merged, x2, wo_bf, ln_g, ln_b)


def _router_body(x_ref, whi_ref, wlo_ref, b_ref, ids_ref, cw_ref, cnt_ref, carry_ref, tri_ref, *, tm):
    @pl.when(pl.program_id(0) == 0)
    def _():
        carry_ref[...] = jnp.zeros_like(carry_ref)
        r = lax.broadcasted_iota(jnp.int32, (tm, tm), 0)
        c = lax.broadcasted_iota(jnp.int32, (tm, tm), 1)
        tri_ref[...] = jnp.where(r < c, 1.0, 0.0).astype(BF16)

    x = x_ref[...]
    hi = x.astype(BF16)
    lo = (x - hi.astype(F32)).astype(BF16)
    whi = whi_ref[...]
    logits = _dot_nt(whi, hi) + _dot_nt(whi, lo) + _dot_nt(wlo_ref[...], hi) + b_ref[...]
    el = logits[0:N_EXPERTS, :]
    gl = logits[N_EXPERTS:N_EXPERTS + N_GROUPS, :]

    gi = lax.broadcasted_iota(jnp.int32, gl.shape, 0)
    gmax = jnp.max(gl, axis=0, keepdims=True)
    gidx = jnp.min(jnp.where(gl == gmax, gi, N_GROUPS), axis=0, keepdims=True)
    p_group = 1.0 / jnp.sum(jnp.exp(gl - gmax), axis=0, keepdims=True)

    ei = lax.broadcasted_iota(jnp.int32, el.shape, 0)
    first = gidx * EXPERTS_PER_GROUP
    in_group = (ei >= first) & (ei < first + EXPERTS_PER_GROUP)
    m1 = jnp.where(in_group, el, -jnp.inf)
    v1 = jnp.max(m1, axis=0, keepdims=True)
    i1 = jnp.min(jnp.where(in_group & (m1 == v1), ei, N_EXPERTS), axis=0, keepdims=True)
    rest = in_group & (ei != i1)
    m2 = jnp.where(rest, el, -jnp.inf)
    v2 = jnp.max(m2, axis=0, keepdims=True)
    i2 = jnp.min(jnp.where(rest & (m2 == v2), ei, N_EXPERTS), axis=0, keepdims=True)
    t = jnp.exp(v2 - v1)
    p1 = 1.0 / (1.0 + t)
    p2 = t * p1

    sel1, sel2 = ei == i1, ei == i2
    onehot = jnp.where(sel1 | sel2, 1.0, 0.0)
    before = _dot(onehot.astype(BF16), tri_ref[...]) + carry_ref[:, 0:1]
    r1 = jnp.sum(jnp.where(sel1, before, 0.0), axis=0, keepdims=True)
    r2 = jnp.sum(jnp.where(sel2, before, 0.0), axis=0, keepdims=True)
    carry_ref[...] = carry_ref[...] + jnp.sum(onehot, axis=1, keepdims=True)

    ids_ref[0:1, :] = i1
    ids_ref[1:2, :] = i2
    ids_ref[2:3, :] = r1.astype(jnp.int32)
    ids_ref[3:4, :] = r2.astype(jnp.int32)
    cw_ref[0:1, :] = p_group * p1
    cw_ref[1:2, :] = p_group * p2
    cnt_ref[...] = carry_ref[...]


def _router(x1, w_hi, w_lo, bias_col):
    n, d = x1.shape
    tm = min(512, n)
    full = pl.BlockSpec((ROUTER_ROWS, d), lambda i: (0, 0))
    return pl.pallas_call(
        functools.partial(_router_body, tm=tm), grid=(n // tm,),
        in_specs=[pl.BlockSpec((tm, d), lambda i: (i, 0)), full, full,
                  pl.BlockSpec((ROUTER_ROWS, 1), lambda i: (0, 0))],
        out_specs=[pl.BlockSpec((4, tm), lambda i: (0, i)),
                   pl.BlockSpec((2, tm), lambda i: (0, i)),
                   pl.BlockSpec((N_EXPERTS, LANES), lambda i: (0, 0))],
        out_shape=[jax.ShapeDtypeStruct((4, n), jnp.int32),
                   jax.ShapeDtypeStruct((2, n), F32),
                   jax.ShapeDtypeStruct((N_EXPERTS, LANES), F32)],
        scratch_shapes=[pltpu.VMEM((N_EXPERTS, LANES), F32), pltpu.VMEM((tm, tm), BF16)],
        compiler_params=_params(("arbitrary",)), name="router",
    )(x1, w_hi, w_lo, bias_col)


def _routing_tables(ids, counts, n, max_pairs):
    te = EXPERT_TILE
    n_tiles = 2 * n // te
    i32 = jnp.int32
    row_end = jnp.cumsum(counts)
    row_start = row_end - counts
    dest1 = row_start[ids[0]] + ids[2]
    dest2 = row_start[ids[1]] + ids[3]
    ne_cum = jnp.cumsum((counts > 0).astype(i32))
    tile_lo = jnp.arange(n_tiles, dtype=i32) * te
    first_e = jnp.sum((row_end[None, :] <= tile_lo[:, None]).astype(i32), axis=1)
    last_e = jnp.sum((row_end[None, :] <= (tile_lo + te - 1)[:, None]).astype(i32), axis=1)
    pairs_t = ne_cum[last_e] - ne_cum[first_e] + 1
    pair_end = jnp.cumsum(pairs_t)
    pair_start = pair_end - pairs_t
    n_pairs = pair_end[-1]
    p = jnp.minimum(jnp.arange(max_pairs, dtype=i32), n_pairs - 1)
    p_tile = jnp.sum((pair_end[None, :] <= p[:, None]).astype(i32), axis=1)
    rank = ne_cum[first_e[p_tile]] - 1 + (p - pair_start[p_tile])
    p_exp = jnp.sum((ne_cum[None, :] <= rank[:, None]).astype(i32), axis=1)
    p_lo = jnp.maximum(row_start[p_exp], p_tile * te) - p_tile * te
    p_hi = jnp.minimum(row_end[p_exp], (p_tile + 1) * te) - p_tile * te
    return (dest1, dest2, p_tile.astype(i32), p_exp.astype(i32), p_lo.astype(i32), p_hi.astype(i32),
            n_pairs.reshape(1).astype(i32))


def _row_copy(src, dst, sem):
    return pltpu.make_async_copy(src, dst, sem)


def _expert_body(ptile_ref, pexp_ref, plo_ref, phi_ref, np_ref, d1_ref, d2_ref,
                 x1_hbm, w1_ref, w3_ref, w2_ref, y2_hbm,
                 w1b, w3b, w2b, xbuf0, xbuf1, ybuf0, ybuf1, tok_s, inv_s, gsem, ssem, *, n, n_tiles):
    te = EXPERT_TILE
    p = pl.program_id(0)
    valid = p < np_ref[0]
    t = ptile_ref[p]
    pm1 = jnp.maximum(p - 1, 0)
    first = (p == 0) | (ptile_ref[pm1] != t)
    new_expert = (p == 0) | (pexp_ref[pm1] != pexp_ref[p])
    lo, hi = plo_ref[p], phi_ref[p]

    def gather(tile, dst, j):
        _row_copy(x1_hbm.at[tok_s[tile * te + j]], dst.at[pl.ds(j, 1)], gsem).start()

    def scatter(tile, src, j):
        _row_copy(src.at[pl.ds(j, 1)], y2_hbm.at[inv_s[tile * te + j]], ssem).start()

    def wait_gather(dst):
        _row_copy(x1_hbm.at[pl.ds(0, te), 0], dst, gsem).wait()

    def wait_scatter(src):
        _row_copy(src, y2_hbm.at[pl.ds(0, te), 0], ssem).wait()

    def issue_loop(fn, tile, buf):
        def one(j, c):
            fn(tile, buf, j)
            return c
        lax.fori_loop(0, te, one, 0, unroll=8)

    @pl.when(p == 0)
    def _():
        def invert(tk, c):
            a, b = d1_ref[tk], d2_ref[tk]
            tok_s[a] = tk
            inv_s[a] = tk
            tok_s[b] = tk
            inv_s[b] = tk + n
            return c
        lax.fori_loop(0, n, invert, 0, unroll=8)
        ybuf0[...] = jnp.zeros_like(ybuf0)
        ybuf1[...] = jnp.zeros_like(ybuf1)
        issue_loop(gather, 0, xbuf0)

    @pl.when(valid & new_expert)
    def _():
        w1b[...] = w1_ref[0].astype(BF16)
        w3b[...] = w3_ref[0].astype(BF16)
        w2b[...] = w2_ref[0].astype(BF16)

    def ffn(xb_ref, yb_ref):
        xb = xb_ref[...].astype(BF16)
        a = _dot(xb, w1b[...])
        b = _dot(xb, w3b[...])
        y = _dot((a * jax.nn.sigmoid(a) * b).astype(BF16), w2b[...])
        rows = lax.broadcasted_iota(jnp.int32, (te, 1), 0)
        yb_ref[...] = jnp.where((rows >= lo) & (rows < hi), y, yb_ref[...])

    for par, (xb, yb, xo, yo) in enumerate(((xbuf0, ybuf0, xbuf1, ybuf1), (xbuf1, ybuf1, xbuf0, ybuf0))):
        mine = valid & ((t & 1) == par)
        interior = first & (t >= 1) & (t < n_tiles - 1)

        @pl.when(mine & interior)
        def _():
            wait_gather(xb)

            @pl.when(t >= 2)
            def _():
                wait_scatter(yb)

            for j in range(te):
                gather(t + 1, xo, j)

            @pl.when(hi > 0)
            def _():
                for j in range(te):
                    scatter(t - 1, yo, j)
                ffn(xb, yb)

        @pl.when(mine & jnp.logical_not(interior))
        def _():
            @pl.when(first & (t == 0))
            def _():
                wait_gather(xb)
                if n_tiles > 1:
                    issue_loop(gather, 1, xo)

            @pl.when(first & (t == n_tiles - 1) & (t >= 1))
            def _():
                wait_gather(xb)

                @pl.when(t >= 2)
                def _():
                    wait_scatter(yb)

                issue_loop(scatter, t - 1, yo)

            ffn(xb, yb)

    @pl.when(p == pl.num_programs(0) - 1)
    def _():
        last = n_tiles - 1
        y_last, y_prev = (ybuf0, ybuf1) if last % 2 == 0 else (ybuf1, ybuf0)
        if n_tiles >= 2:
            wait_scatter(y_prev)
        issue_loop(scatter, last, y_last)
        wait_scatter(y_last)


def _experts(x1_rows, tables, w1, w3, w2):
    n, _, d = x1_rows.shape
    de = w1.shape[2]
    te = EXPERT_TILE
    n_tiles = 2 * n // te
    max_pairs = n_tiles + N_EXPERTS - 1
    wmap = lambda p, pt, pe, *_: (pe[p], 0, 0)
    return pl.pallas_call(
        functools.partial(_expert_body, n=n, n_tiles=n_tiles),
        grid_spec=pltpu.PrefetchScalarGridSpec(
            num_scalar_prefetch=7, grid=(max_pairs,),
            in_specs=[pl.BlockSpec(memory_space=pl.ANY),
                      pl.BlockSpec((1, d, de), wmap), pl.BlockSpec((1, d, de), wmap), pl.BlockSpec((1, de, d), wmap)],
            out_specs=pl.BlockSpec(memory_space=pl.ANY),
            scratch_shapes=[pltpu.VMEM((d, de), BF16), pltpu.VMEM((d, de), BF16), pltpu.VMEM((de, d), BF16),
                            pltpu.VMEM((te, d), F32), pltpu.VMEM((te, d), F32),
                            pltpu.VMEM((te, d), F32), pltpu.VMEM((te, d), F32),
                            pltpu.SMEM((2 * n,), jnp.int32), pltpu.SMEM((2 * n,), jnp.int32),
                            pltpu.SemaphoreType.DMA(()), pltpu.SemaphoreType.DMA(())]),
        out_shape=jax.ShapeDtypeStruct((2 * n, 1, d), F32),
        compiler_params=_params(("arbitrary",)), name="experts",
    )(*tables, x1_rows, w1, w3, w2)


def _combine_body(x_ref, ya_ref, yb_ref, cw_ref, g_ref, b_ref, o_ref):
    y = cw_ref[:, 0:1] * ya_ref[:, 0, :] + cw_ref[:, 1:2] * yb_ref[:, 0, :]
    o_ref[...] = _layer_norm(DN_ALPHA * x_ref[...] + y, g_ref[...], b_ref[...])


def _combine(y2, x1, cw_col, ln_g, ln_b):
    n, d = x1.shape
    tm = min(512, n)
    nt = n // tm
    tile = pl.BlockSpec((tm, d), lambda i: (i, 0))
    row = pl.BlockSpec((1, d), lambda i: (0, 0))
    return pl.pallas_call(
        _combine_body, grid=(nt,),
        in_specs=[tile, pl.BlockSpec((tm, 1, d), lambda i: (i, 0, 0)),
                  pl.BlockSpec((tm, 1, d), lambda i: (i + nt, 0, 0)),
                  pl.BlockSpec((tm, 2), lambda i: (i, 0)), row, row],
        out_specs=tile,
        out_shape=jax.ShapeDtypeStruct((n, d), F32),
        compiler_params=_params(("arbitrary",)), name="combine_ln2",
    )(x1, y2, y2, cw_col, ln_g, ln_b)


def _retention_consts():
    h, c = RET_HEADS, CHUNK
    log_gamma = jnp.log1p(-jnp.exp2(-5.0 - jnp.arange(h, dtype=F32)))
    idx = jnp.arange(c, dtype=F32)
    rel = idx[:, None] - idx[None, :]
    mask = jnp.where(rel >= 0, jnp.exp(log_gamma[:, None, None] * jnp.maximum(rel, 0.0)), 0.0)
    xi = jnp.exp(log_gamma[:, None] * (idx + 1.0))
    zeta = jnp.exp(log_gamma[:, None] * (c - 1.0 - idx))
    chunk_decay = jnp.exp(log_gamma * c)
    xi_b = jnp.broadcast_to(xi[:, :, None], (h, c, RET_QK_DIM))
    zeta_b = jnp.broadcast_to(zeta[:, :, None], (h, c, RET_QK_DIM))
    cd_b = jnp.broadcast_to(chunk_decay[:, None, None], (h, 1, RET_V_DIM))
    return mask, xi_b, zeta_b, cd_b


def _rope_consts():
    half = RET_QK_DIM // 2
    freq = ROPE_THETA ** (-jnp.arange(half, dtype=F32) / half)
    freq2 = jnp.concatenate([freq, freq])[None, :]
    sign2 = jnp.concatenate([-jnp.ones((half,), F32), jnp.ones((half,), F32)])[None, :]
    return freq2, sign2


def kernel(x, positions, w_in, b_gate, ret_gn_g, sgu_ln_g, sgu_ln_b, sgu_w, sgu_b, w_proj_ret, w_proj_sgu,
           w_out, ln1_g, ln1_b, w_group, b_group, w_er, b_er, w1, w3, w2, ln2_g, ln2_b):
    batch, seq, d = x.shape
    n = batch * seq
    qk_w, v_w = RET_HEADS * RET_QK_DIM, RET_HEADS * RET_V_DIM
    assert d == v_w == SGU_GROUPS * CHUNK and seq % CHUNK == 0
    ret_cols = 2 * qk_w + 2 * v_w
    max_pairs = (2 * n) // EXPERT_TILE + N_EXPERTS - 1

    freq2, sign2 = _rope_consts()
    cos2, sin2 = _rope_table(positions.reshape(n, 1), freq2, sign2)
    ret_consts = _retention_consts()

    xc = x.reshape(n, d)
    for l in range(w_in.shape[0]):
        wl = w_in[l]
        parts = [wl[:, 0:qk_w].reshape(d, RET_HEADS, RET_QK_DIM),
                 wl[:, qk_w:2 * qk_w].reshape(d, RET_HEADS, RET_QK_DIM),
                 wl[:, 2 * qk_w:2 * qk_w + v_w].reshape(d, RET_HEADS, RET_V_DIM),
                 wl[:, 2 * qk_w + v_w:ret_cols].reshape(d, RET_HEADS, RET_V_DIM)]
        wa = jnp.concatenate(parts, axis=2).transpose(1, 0, 2).astype(BF16)
        y_a = _retention_branch(xc, cos2, sin2, wa, w_proj_ret[l].astype(BF16), ret_consts,
                                ret_gn_g[l][None, :], batch, seq)
        h = _in_proj(xc, wl[:, ret_cols:].astype(BF16))
        bias_b = jnp.broadcast_to(sgu_b[l][:, :, None], (SGU_GROUPS, CHUNK, CHUNK))
        y_sgu = _spatial_gating(h, sgu_ln_g[l][None, :], sgu_ln_b[l][None, :], sgu_w[l], bias_b, 0, 1)
        merged = _merge(y_a, y_sgu, h, b_gate[l], w_proj_sgu[l].astype(BF16), 2, 3)
        x1, x1_rows = _out_proj(merged, xc, w_out[l].astype(BF16), ln1_g[l][None, :], ln1_b[l][None, :])

        w_r = jnp.concatenate([w_er[l], w_group[l]], axis=1).T
        w_r = jnp.pad(w_r, ((0, ROUTER_ROWS - w_r.shape[0]), (0, 0)))
        w_hi = w_r.astype(BF16)
        w_lo = (w_r - w_hi.astype(F32)).astype(BF16)
        bias = jnp.pad(jnp.concatenate([b_er[l], b_group[l]]), (0, ROUTER_ROWS - N_EXPERTS - N_GROUPS))[:, None]
        ids, cw, cnt = _router(x1, w_hi, w_lo, bias)

        counts = cnt[:, 0].astype(jnp.int32)
        dest1, dest2, p_tile, p_exp, p_lo, p_hi, n_pairs = _routing_tables(ids, counts, n, max_pairs)
        y2 = _experts(x1_rows, (p_tile, p_exp, p_lo, p_hi, n_pairs, dest1, dest2), w1[l], w3[l], w2[l])
        xc = _combine(y2, x1, cw.T, ln2_g[l][None, :], ln2_b[l][None, :])
    return xc.reshape(batch, seq, d)
```

```python
import functools

import numpy as np
import jax
import jax.numpy as jnp
from jax import lax
from jax.experimental import pallas as pl
from jax.experimental.pallas import tpu as pltpu

F32 = jnp.float32
BF16 = jnp.bfloat16

RET_HEADS = 8
RET_QK_DIM = 128
RET_V_DIM = 256
CHUNK = 128
ROPE_THETA = 10000.0
SGU_GROUPS = 16
N_GROUPS = 4
EXPERTS_PER_GROUP = 8
N_EXPERTS = N_GROUPS * EXPERTS_PER_GROUP
ROUTER_ROWS = 40
LN_EPS = 1e-5
DEPTH = 1
DN_ALPHA = (2 * DEPTH) ** 0.25
SQRT_HALF = np.sqrt(0.5).astype(np.float32)

LANES = 128
VMEM_LIMIT = 52 * 1024 * 1024
EXPERT_TILE = 256


def _params(sem, vmem=VMEM_LIMIT):
    return pltpu.CompilerParams(dimension_semantics=sem, vmem_limit_bytes=vmem)


def _dot(a, b):
    return jnp.dot(a, b, preferred_element_type=F32)


def _dot_nt(a, b):
    return lax.dot_general(a, b, (((1,), (1,)), ((), ())), preferred_element_type=F32)


def _dot_tn(a, b):
    return lax.dot_general(a, b, (((0,), (0,)), ((), ())), preferred_element_type=F32)


def _layer_norm(r, g, b):
    mu = jnp.mean(r, axis=-1, keepdims=True)
    d = r - mu
    var = jnp.mean(d * d, axis=-1, keepdims=True)
    return d * lax.rsqrt(var + LN_EPS) * g + b


def _gelu(x):
    return 0.5 * x * (1.0 + lax.erf(x * SQRT_HALF))


def _inproj_body(x_ref, w_ref, o_ref, xb_ref):
    @pl.when(pl.program_id(1) == 0)
    def _():
        xb_ref[...] = x_ref[...].astype(BF16)

    o_ref[...] = _dot(xb_ref[...], w_ref[...]).astype(BF16)


def _in_proj(x2, w_bf):
    n, d = x2.shape
    width = w_bf.shape[1]
    tm, tn = min(1024, n), 1024
    return pl.pallas_call(
        _inproj_body, grid=(n // tm, width // tn),
        in_specs=[pl.BlockSpec((tm, d), lambda i, j: (i, 0)),
                  pl.BlockSpec((d, tn), lambda i, j: (0, j))],
        out_specs=pl.BlockSpec((tm, tn), lambda i, j: (i, j)),
        out_shape=jax.ShapeDtypeStruct((n, width), BF16),
        scratch_shapes=[pltpu.VMEM((tm, d), BF16)],
        compiler_params=_params(("arbitrary", "arbitrary")), name="in_proj",
    )(x2, w_bf)


HEAD_COLS = 2 * RET_QK_DIM + 2 * RET_V_DIM


def _retention_body(x_ref, pos_ref, freq_ref, sign_ref, wa_ref, wp_ref, mask_ref, xi_ref, zeta_ref, cd_ref,
                    gain_ref, o_ref, state_ref, yg_ref, cos_ref, sin_ref, *, n_chunks):
    @pl.when(pl.program_id(1) == 0)
    def _():
        state_ref[...] = jnp.zeros_like(state_ref)

    ang = pos_ref[...].astype(F32) * freq_ref[...]
    cos_ref[...] = jnp.cos(ang)
    sin_ref[...] = jnp.sin(ang) * sign_ref[...]

    scale = RET_QK_DIM ** -0.5
    half = RET_QK_DIM // 2
    dk, dv = RET_QK_DIM, RET_V_DIM
    xb = x_ref[...].astype(BF16)
    for hd in range(RET_HEADS):
        proj = _dot(xb, wa_ref[hd])
        gain = gain_ref[:, hd * dv:(hd + 1) * dv]
        for c in range(n_chunks):
            lo, hi = c * CHUNK, (c + 1) * CHUNK
            cos, sin = cos_ref[lo:hi, :], sin_ref[lo:hi, :]
            q, k = proj[lo:hi, 0:dk], proj[lo:hi, dk:2 * dk]
            vb = proj[lo:hi, 2 * dk:2 * dk + dv].astype(BF16)
            g = proj[lo:hi, 2 * dk + dv:]
            qr = q * cos + pltpu.roll(q, half, 1) * sin
            kr = (k * cos + pltpu.roll(k, half, 1) * sin) * scale
            scores = _dot_nt(qr.astype(BF16), kr.astype(BF16)) * mask_ref[hd]
            state = state_ref[hd]
            lhs = jnp.concatenate([scores.astype(BF16), (qr * xi_ref[hd]).astype(BF16)], axis=1)
            rhs = jnp.concatenate([vb, state.astype(BF16)], axis=0)
            out = _dot(lhs, rhs)
            state_ref[hd] = state * cd_ref[hd] + _dot_tn((kr * zeta_ref[hd]).astype(BF16), vb)
            mu = jnp.mean(out, axis=-1, keepdims=True)
            d = out - mu
            var = jnp.mean(d * d, axis=-1, keepdims=True)
            yn = d * lax.rsqrt(var + LN_EPS) * gain
            yg_ref[lo:hi, hd * dv:(hd + 1) * dv] = (g * jax.nn.sigmoid(g) * yn).astype(BF16)
    o_ref[...] = _dot(yg_ref[...], wp_ref[...]).astype(BF16)


def _retention_branch(x2, pos_col, freq2, sign2, wa_bf, wp_bf, consts, gain, batch, seq):
    n, d = x2.shape
    tr = min(256, seq)
    nr = seq // tr
    mask, xi_b, zeta_b, cd_b = consts
    vw = RET_HEADS * RET_V_DIM

    def once(shape):
        return pl.BlockSpec(shape, lambda b, r: (0,) * len(shape), pipeline_mode=pl.Buffered(1))

    rows = lambda w: pl.BlockSpec((tr, w), lambda b, r: (b * nr + r, 0))
    return pl.pallas_call(
        functools.partial(_retention_body, n_chunks=tr // CHUNK),
        grid=(batch, nr),
        in_specs=[rows(d), rows(1), once((1, LANES)), once((1, LANES)),
                  once((RET_HEADS, d, HEAD_COLS)), once((vw, d)),
                  once((RET_HEADS, CHUNK, CHUNK)), once((RET_HEADS, CHUNK, RET_QK_DIM)),
                  once((RET_HEADS, CHUNK, RET_QK_DIM)), once((RET_HEADS, 1, RET_V_DIM)), once((1, vw))],
        out_specs=rows(d),
        out_shape=jax.ShapeDtypeStruct((n, d), BF16),
        scratch_shapes=[pltpu.VMEM((RET_HEADS, RET_QK_DIM, RET_V_DIM), F32), pltpu.VMEM((tr, vw), BF16),
                        pltpu.VMEM((tr, LANES), F32), pltpu.VMEM((tr, LANES), F32)],
        compiler_params=_params(("arbitrary", "arbitrary")), name="retention_branch",
    )(x2, pos_col, freq2, sign2, wa_bf, wp_bf, mask, xi_b, zeta_b, cd_b, gain)


def _sgu_merge_body(u_ref, v_ref, ga_ref, gb_ref, ya_ref, lng_ref, lnb_ref, ws_ref, bs_ref, bg_ref, wp_ref,
                    o_ref, wtril_ref, ys_ref, *, n_chunks):
    @pl.when(pl.program_id(0) == 0)
    def _():
        r = lax.broadcasted_iota(jnp.int32, (CHUNK, CHUNK), 0)
        c = lax.broadcasted_iota(jnp.int32, (CHUNK, CHUNK), 1)
        for g in range(SGU_GROUPS):
            wtril_ref[g] = jnp.where(r >= c, ws_ref[g], 0.0).astype(BF16)

    v = _gelu(v_ref[...].astype(F32))
    vn = _layer_norm(v, lng_ref[...], lnb_ref[...]).astype(BF16)
    for g in range(SGU_GROUPS):
        cols = pl.ds(g * CHUNK, CHUNK)
        lo = g * CHUNK
        rhs = jnp.concatenate([vn[c * CHUNK:(c + 1) * CHUNK, lo:lo + CHUNK] for c in range(n_chunks)], axis=1)
        mix = _dot(wtril_ref[g], rhs)
        bias = bs_ref[g]
        for c in range(n_chunks):
            rows = pl.ds(c * CHUNK, CHUNK)
            u = _gelu(u_ref[rows, cols].astype(F32))
            ys_ref[rows, cols] = (u * (mix[:, c * CHUNK:(c + 1) * CHUNK] + bias)).astype(BF16)

    yb = _dot(ys_ref[...], wp_ref[...])
    sa = jax.nn.sigmoid(ga_ref[...].astype(F32) + bg_ref[0:1, :])
    sb = jax.nn.sigmoid(gb_ref[...].astype(F32) + bg_ref[1:2, :])
    o_ref[...] = (sa * ya_ref[...].astype(F32) + sb * yb).astype(BF16)


def _sgu_merge(h, y_a, ln_g, ln_b, w_s, b_s_b, b_gate, wp_bf):
    n, d = y_a.shape
    ts = min(256, n)
    tile = lambda blk: pl.BlockSpec((ts, d), lambda i: (i, blk))
    row = pl.BlockSpec((1, d), lambda i: (0, 0))
    full3 = pl.BlockSpec((SGU_GROUPS, CHUNK, CHUNK), lambda i: (0, 0, 0))
    return pl.pallas_call(
        functools.partial(_sgu_merge_body, n_chunks=ts // CHUNK),
        grid=(n // ts,),
        in_specs=[tile(0), tile(1), tile(2), tile(3), tile(0), row, row, full3, full3,
                  pl.BlockSpec((2, d), lambda i: (0, 0)),
                  pl.BlockSpec((d, d), lambda i: (0, 0), pipeline_mode=pl.Buffered(1))],
        out_specs=tile(0),
        out_shape=jax.ShapeDtypeStruct((n, d), BF16),
        scratch_shapes=[pltpu.VMEM((SGU_GROUPS, CHUNK, CHUNK), BF16), pltpu.VMEM((ts, d), BF16)],
        compiler_params=_params(("arbitrary",)), name="sgu_merge",
    )(h, h, h, h, y_a, ln_g, ln_b, w_s, b_s_b, b_gate, wp_bf)


def _outproj_router_body(m_ref, x_ref, w_ref, g_ref, b_ref, whi_ref, wlo_ref, rb_ref,
                         o_ref, orow_ref, ids_ref, cw_ref, cnt_ref, carry_ref, tri_ref, *, tm):
    @pl.when(pl.program_id(0) == 0)
    def _():
        carry_ref[...] = jnp.zeros_like(carry_ref)
        r = lax.broadcasted_iota(jnp.int32, (tm, tm), 0)
        c = lax.broadcasted_iota(jnp.int32, (tm, tm), 1)
        tri_ref[...] = jnp.where(r < c, 1.0, 0.0).astype(BF16)

    r = DN_ALPHA * x_ref[...] + _dot(m_ref[...], w_ref[...])
    x = _layer_norm(r, g_ref[...], b_ref[...])
    o_ref[...] = x
    orow_ref[:, 0, :] = x

    hi = x.astype(BF16)
    lo = (x - hi.astype(F32)).astype(BF16)
    whi = whi_ref[...]
    logits = _dot_nt(whi, hi) + _dot_nt(whi, lo) + _dot_nt(wlo_ref[...], hi) + rb_ref[...]
    el = logits[0:N_EXPERTS, :]
    gl = logits[N_EXPERTS:N_EXPERTS + N_GROUPS, :]

    gi = lax.broadcasted_iota(jnp.int32, gl.shape, 0)
    gmax = jnp.max(gl, axis=0, keepdims=True)
    gidx = jnp.min(jnp.where(gl == gmax, gi, N_GROUPS), axis=0, keepdims=True)
    p_group = 1.0 / jnp.sum(jnp.exp(gl - gmax), axis=0, keepdims=True)

    ei = lax.broadcasted_iota(jnp.int32, el.shape, 0)
    first = gidx * EXPERTS_PER_GROUP
    in_group = (ei >= first) & (ei < first + EXPERTS_PER_GROUP)
    m1 = jnp.where(in_group, el, -jnp.inf)
    v1 = jnp.max(m1, axis=0, keepdims=True)
    i1 = jnp.min(jnp.where(in_group & (m1 == v1), ei, N_EXPERTS), axis=0, keepdims=True)
    rest = in_group & (ei != i1)
    m2 = jnp.where(rest, el, -jnp.inf)
    v2 = jnp.max(m2, axis=0, keepdims=True)
    i2 = jnp.min(jnp.where(rest & (m2 == v2), ei, N_EXPERTS), axis=0, keepdims=True)
    t = jnp.exp(v2 - v1)
    p1 = 1.0 / (1.0 + t)
    p2 = t * p1

    sel1, sel2 = ei == i1, ei == i2
    onehot = jnp.where(sel1 | sel2, 1.0, 0.0)
    before = _dot(onehot.astype(BF16), tri_ref[...]) + carry_ref[:, 0:1]
    r1 = jnp.sum(jnp.where(sel1, before, 0.0), axis=0, keepdims=True)
    r2 = jnp.sum(jnp.where(sel2, before, 0.0), axis=0, keepdims=True)
    carry_ref[...] = carry_ref[...] + jnp.sum(onehot, axis=1, keepdims=True)

    ids_ref[0:1, :] = i1
    ids_ref[1:2, :] = i2
    ids_ref[2:3, :] = r1.astype(jnp.int32)
    ids_ref[3:4, :] = r2.astype(jnp.int32)
    cw_ref[0:1, :] = p_group * p1
    cw_ref[1:2, :] = p_group * p2
    cnt_ref[...] = carry_ref[...]


def _out_proj_router(merged, x2, wo_bf, ln_g, ln_b, w_hi, w_lo, bias_col):
    n, d = x2.shape
    tm = min(512, n)
    tile = pl.BlockSpec((tm, d), lambda i: (i, 0))
    row = pl.BlockSpec((1, d), lambda i: (0, 0))
    full = pl.BlockSpec((ROUTER_ROWS, d), lambda i: (0, 0))
    return pl.pallas_call(
        functools.partial(_outproj_router_body, tm=tm), grid=(n // tm,),
        in_specs=[tile, tile, pl.BlockSpec((d, d), lambda i: (0, 0), pipeline_mode=pl.Buffered(1)), row, row,
                  full, full, pl.BlockSpec((ROUTER_ROWS, 1), lambda i: (0, 0))],
        out_specs=[tile, pl.BlockSpec((tm, 1, d), lambda i: (i, 0, 0)),
                   pl.BlockSpec((4, tm), lambda i: (0, i)),
                   pl.BlockSpec((2, tm), lambda i: (0, i)),
                   pl.BlockSpec((N_EXPERTS, LANES), lambda i: (0, 0))],
        out_shape=[jax.ShapeDtypeStruct((n, d), F32), jax.ShapeDtypeStruct((n, 1, d), F32),
                   jax.ShapeDtypeStruct((4, n), jnp.int32),
                   jax.ShapeDtypeStruct((2, n), F32),
                   jax.ShapeDtypeStruct((N_EXPERTS, LANES), F32)],
        scratch_shapes=[pltpu.VMEM((N_EXPERTS, LANES), F32), pltpu.VMEM((tm, tm), BF16)],
        compiler_params=_params(("arbitrary",)), name="out_proj_ln1_router",
    )(merged, x2, wo_bf, ln_g, ln_b, w_hi, w_lo, bias_col)


def _routing_tables(ids, counts, n, max_pairs):
    te = EXPERT_TILE
    n_tiles = 2 * n // te
    i32 = jnp.int32
    row_end = jnp.cumsum(counts)
    row_start = row_end - counts
    dest1 = row_start[ids[0]] + ids[2]
    dest2 = row_start[ids[1]] + ids[3]
    ne_cum = jnp.cumsum((counts > 0).astype(i32))
    tile_lo = jnp.arange(n_tiles, dtype=i32) * te
    first_e = jnp.sum((row_end[None, :] <= tile_lo[:, None]).astype(i32), axis=1)
    last_e = jnp.sum((row_end[None, :] <= (tile_lo + te - 1)[:, None]).astype(i32), axis=1)
    pairs_t = ne_cum[last_e] - ne_cum[first_e] + 1
    pair_end = jnp.cumsum(pairs_t)
    pair_start = pair_end - pairs_t
    n_pairs = pair_end[-1]
    p = jnp.minimum(jnp.arange(max_pairs, dtype=i32), n_pairs - 1)
    p_tile = jnp.sum((pair_end[None, :] <= p[:, None]).astype(i32), axis=1)
    rank = ne_cum[first_e[p_tile]] - 1 + (p - pair_start[p_tile])
    p_exp = jnp.sum((ne_cum[None, :] <= rank[:, None]).astype(i32), axis=1)
    p_lo = jnp.maximum(row_start[p_exp], p_tile * te) - p_tile * te
    p_hi = jnp.minimum(row_end[p_exp], (p_tile + 1) * te) - p_tile * te
    return (dest1, dest2, p_tile.astype(i32), p_exp.astype(i32), p_lo.astype(i32), p_hi.astype(i32),
            n_pairs.reshape(1).astype(i32))


def _row_copy(src, dst, sem):
    return pltpu.make_async_copy(src, dst, sem)


def _expert_body(ptile_ref, pexp_ref, plo_ref, phi_ref, np_ref, d1_ref, d2_ref,
                 x1_hbm, w1_ref, w3_ref, w2_ref, y2_hbm,
                 w1b, w3b, w2b, xbuf0, xbuf1, ybuf0, ybuf1, tok_s, inv_s, gsem, ssem, *, n, n_tiles):
    te = EXPERT_TILE
    p = pl.program_id(0)
    valid = p < np_ref[0]
    t = ptile_ref[p]
    pm1 = jnp.maximum(p - 1, 0)
    first = (p == 0) | (ptile_ref[pm1] != t)
    new_expert = (p == 0) | (pexp_ref[pm1] != pexp_ref[p])
    lo, hi = plo_ref[p], phi_ref[p]

    def gather(tile, dst, j):
        _row_copy(x1_hbm.at[tok_s[tile * te + j]], dst.at[pl.ds(j, 1)], gsem).start()

    def scatter(tile, src, j):
        _row_copy(src.at[pl.ds(j, 1)], y2_hbm.at[inv_s[tile * te + j]], ssem).start()

    def wait_gather(dst):
        _row_copy(x1_hbm.at[pl.ds(0, te), 0], dst, gsem).wait()

    def wait_scatter(src):
        _row_copy(src, y2_hbm.at[pl.ds(0, te), 0], ssem).wait()

    def issue_loop(fn, tile, buf):
        def one(j, c):
            fn(tile, buf, j)
            return c
        lax.fori_loop(0, te, one, 0, unroll=8)

    @pl.when(p == 0)
    def _():
        def invert(tk, c):
            a, b = d1_ref[tk], d2_ref[tk]
            tok_s[a] = tk
            inv_s[a] = tk
            tok_s[b] = tk
            inv_s[b] = tk + n
            return c
        lax.fori_loop(0, n, invert, 0, unroll=8)
        ybuf0[...] = jnp.zeros_like(ybuf0)
        ybuf1[...] = jnp.zeros_like(ybuf1)
        issue_loop(gather, 0, xbuf0)

    @pl.when(valid & new_expert)
    def _():
        w1b[...] = w1_ref[0].astype(BF16)
        w3b[...] = w3_ref[0].astype(BF16)
        w2b[...] = w2_ref[0].astype(BF16)

    def ffn(xb_ref, yb_ref):
        xb = xb_ref[...].astype(BF16)
        a = _dot(xb, w1b[...])
        b = _dot(xb, w3b[...])
        y = _dot((a * jax.nn.sigmoid(a) * b).astype(BF16), w2b[...])
        rows = lax.broadcasted_iota(jnp.int32, (te, 1), 0)
        yb_ref[...] = jnp.where((rows >= lo) & (rows < hi), y, yb_ref[...])

    for par, (xb, yb, xo, yo) in enumerate(((xbuf0, ybuf0, xbuf1, ybuf1), (xbuf1, ybuf1, xbuf0, ybuf0))):
        mine = valid & ((t & 1) == par)
        interior = first & (t >= 1) & (t < n_tiles - 1)

        @pl.when(mine & interior)
        def _():
            wait_gather(xb)

            @pl.when(t >= 2)
            def _():
                wait_scatter(yb)

            for j in range(te):
                gather(t + 1, xo, j)

            @pl.when(hi > 0)
            def _():
                for j in range(te):
                    scatter(t - 1, yo, j)
                ffn(xb, yb)

        @pl.when(mine & jnp.logical_not(interior))
        def _():
            @pl.when(first & (t == 0))
            def _():
                wait_gather(xb)
                if n_tiles > 1:
                    issue_loop(gather, 1, xo)

            @pl.when(first & (t == n_tiles - 1) & (t >= 1))
            def _():
                wait_gather(xb)

                @pl.when(t >= 2)
                def _():
                    wait_scatter(yb)

                issue_loop(scatter, t - 1, yo)

            ffn(xb, yb)

    @pl.when(p == pl.num_programs(0) - 1)
    def _():
        last = n_tiles - 1
        y_last, y_prev = (ybuf0, ybuf1) if last % 2 == 0 else (ybuf1, ybuf0)
        if n_tiles >= 2:
            wait_scatter(y_prev)
        issue_loop(scatter, last, y_last)
        wait_scatter(y_last)


def _experts(x1_rows, tables, w1, w3, w2):
    n, _, d = x1_rows.shape
    de = w1.shape[2]
    te = EXPERT_TILE
    n_tiles = 2 * n // te
    max_pairs = n_tiles + N_EXPERTS - 1
    wmap = lambda p, pt, pe, *_: (pe[p], 0, 0)
    return pl.pallas_call(
        functools.partial(_expert_body, n=n, n_tiles=n_tiles),
        grid_spec=pltpu.PrefetchScalarGridSpec(
            num_scalar_prefetch=7, grid=(max_pairs,),
            in_specs=[pl.BlockSpec(memory_space=pl.ANY),
                      pl.BlockSpec((1, d, de), wmap), pl.BlockSpec((1, d, de), wmap), pl.BlockSpec((1, de, d), wmap)],
            out_specs=pl.BlockSpec(memory_space=pl.ANY),
            scratch_shapes=[pltpu.VMEM((d, de), BF16), pltpu.VMEM((d, de), BF16), pltpu.VMEM((de, d), BF16),
                            pltpu.VMEM((te, d), F32), pltpu.VMEM((te, d), F32),
                            pltpu.VMEM((te, d), F32), pltpu.VMEM((te, d), F32),
                            pltpu.SMEM((2 * n,), jnp.int32), pltpu.SMEM((2 * n,), jnp.int32),
                            pltpu.SemaphoreType.DMA(()), pltpu.SemaphoreType.DMA(())]),
        out_shape=jax.ShapeDtypeStruct((2 * n, 1, d), F32),
        compiler_params=_params(("arbitrary",)), name="experts",
    )(*tables, x1_rows, w1, w3, w2)


def _combine_body(x_ref, ya_ref, yb_ref, cw_ref, g_ref, b_ref, o_ref):
    y = cw_ref[:, 0:1] * ya_ref[:, 0, :] + cw_ref[:, 1:2] * yb_ref[:, 0, :]
    o_ref[...] = _layer_norm(DN_ALPHA * x_ref[...] + y, g_ref[...], b_ref[...])


def _combine(y2, x1, cw_col, ln_g, ln_b):
    n, d = x1.shape
    tm = min(512, n)
    nt = n // tm
    tile = pl.BlockSpec((tm, d), lambda i: (i, 0))
    row = pl.BlockSpec((1, d), lambda i: (0, 0))
    return pl.pallas_call(
        _combine_body, grid=(nt,),
        in_specs=[tile, pl.BlockSpec((tm, 1, d), lambda i: (i, 0, 0)),
                  pl.BlockSpec((tm, 1, d), lambda i: (i + nt, 0, 0)),
                  pl.BlockSpec((tm, 2), lambda i: (i, 0)), row, row],
        out_specs=tile,
        out_shape=jax.ShapeDtypeStruct((n, d), F32),
        compiler_params=_params(("arbitrary",)), name="combine_ln2",
    )(x1, y2, y2, cw_col, ln_g, ln_b)


def _retention_consts():
    h, c = RET_HEADS, CHUNK
    log_gamma = jnp.log1p(-jnp.exp2(-5.0 - jnp.arange(h, dtype=F32)))
    idx = jnp.arange(c, dtype=F32)
    rel = idx[:, None] - idx[None, :]
    mask = jnp.where(rel >= 0, jnp.exp(log_gamma[:, None, None] * jnp.maximum(rel, 0.0)), 0.0)
    xi = jnp.exp(log_gamma[:, None] * (idx + 1.0))
    zeta = jnp.exp(log_gamma[:, None] * (c - 1.0 - idx))
    chunk_decay = jnp.exp(log_gamma * c)
    xi_b = jnp.broadcast_to(xi[:, :, None], (h, c, RET_QK_DIM))
    zeta_b = jnp.broadcast_to(zeta[:, :, None], (h, c, RET_QK_DIM))
    cd_b = jnp.broadcast_to(chunk_decay[:, None, None], (h, 1, RET_V_DIM))
    return mask, xi_b, zeta_b, cd_b


def _rope_consts():
    half = RET_QK_DIM // 2
    freq = ROPE_THETA ** (-jnp.arange(half, dtype=F32) / half)
    freq2 = jnp.concatenate([freq, freq])[None, :]
    sign2 = jnp.concatenate([-jnp.ones((half,), F32), jnp.ones((half,), F32)])[None, :]
    return freq2, sign2


def kernel(x, positions, w_in, b_gate, ret_gn_g, sgu_ln_g, sgu_ln_b, sgu_w, sgu_b, w_proj_ret, w_proj_sgu,
           w_out, ln1_g, ln1_b, w_group, b_group, w_er, b_er, w1, w3, w2, ln2_g, ln2_b):
    batch, seq, d = x.shape
    n = batch * seq
    qk_w, v_w = RET_HEADS * RET_QK_DIM, RET_HEADS * RET_V_DIM
    assert d == v_w == SGU_GROUPS * CHUNK and seq % CHUNK == 0
    ret_cols = 2 * qk_w + 2 * v_w
    max_pairs = (2 * n) // EXPERT_TILE + N_EXPERTS - 1

    freq2, sign2 = _rope_consts()
    pos_col = positions.reshape(n, 1)
    ret_consts = _retention_consts()

    xc = x.reshape(n, d)
    for l in range(w_in.shape[0]):
        wl = w_in[l]
        parts = [wl[:, 0:qk_w].reshape(d, RET_HEADS, RET_QK_DIM),
                 wl[:, qk_w:2 * qk_w].reshape(d, RET_HEADS, RET_QK_DIM),
                 wl[:, 2 * qk_w:2 * qk_w + v_w].reshape(d, RET_HEADS, RET_V_DIM),
                 wl[:, 2 * qk_w + v_w:ret_cols].reshape(d, RET_HEADS, RET_V_DIM)]
        wa = jnp.concatenate(parts, axis=2).transpose(1, 0, 2).astype(BF16)
        y_a = _retention_branch(xc, pos_col, freq2, sign2, wa, w_proj_ret[l].astype(BF16), ret_consts,
                                ret_gn_g[l][None, :], batch, seq)
        h = _in_proj(xc, wl[:, ret_cols:].astype(BF16))
        bias_b = jnp.broadcast_to(sgu_b[l][:, :, None], (SGU_GROUPS, CHUNK, CHUNK))
        merged = _sgu_merge(h, y_a, sgu_ln_g[l][None, :], sgu_ln_b[l][None, :], sgu_w[l], bias_b, b_gate[l],
                            w_proj_sgu[l].astype(BF16))

        w_r = jnp.concatenate([w_er[l], w_group[l]], axis=1).T
        w_r = jnp.pad(w_r, ((0, ROUTER_ROWS - w_r.shape[0]), (0, 0)))
        w_hi = w_r.astype(BF16)
        w_lo = (w_r - w_hi.astype(F32)).astype(BF16)
        bias = jnp.pad(jnp.concatenate([b_er[l], b_group[l]]), (0, ROUTER_ROWS - N_EXPERTS - N_GROUPS))[:, None]
        x1, x1_rows, ids, cw, cnt = _out_proj_router(merged, xc, w_out[l].astype(BF16), ln1_g[l][None, :],
                                                     ln1_b[l][None, :], w_hi, w_lo, bias)

        counts = cnt[:, 0].astype(jnp.int32)
        dest1, dest2, p_tile, p_exp, p_lo, p_hi, n_pairs = _routing_tables(ids, counts, n, max_pairs)
        y2 = _experts(x1_rows, (p_tile, p_exp, p_lo, p_hi, n_pairs, dest1, dest2), w1[l], w3[l], w2[l])
        xc = _combine(y2, x1, cw.T, ln2_g[l][None, :], ln2_b[l][None, :])
    return xc.reshape(batch, seq, d)
```

```python
import functools

import numpy as np
import jax
import jax.numpy as jnp
from jax import lax
from jax.experimental import pallas as pl
from jax.experimental.pallas import tpu as pltpu

F32 = jnp.float32
BF16 = jnp.bfloat16

RET_HEADS = 8
RET_QK_DIM = 128
RET_V_DIM = 256
CHUNK = 128
ROPE_THETA = 10000.0
SGU_GROUPS = 16
N_GROUPS = 4
EXPERTS_PER_GROUP = 8
N_EXPERTS = N_GROUPS * EXPERTS_PER_GROUP
ROUTER_ROWS = 40
LN_EPS = 1e-5
DEPTH = 1
DN_ALPHA = (2 * DEPTH) ** 0.25
SQRT_HALF = np.sqrt(0.5).astype(np.float32)

LANES = 128
VMEM_LIMIT = 52 * 1024 * 1024
EXPERT_TILE = 256


def _params(sem, vmem=VMEM_LIMIT):
    return pltpu.CompilerParams(dimension_semantics=sem, vmem_limit_bytes=vmem)


def _dot(a, b):
    return jnp.dot(a, b, preferred_element_type=F32)


def _dot_nt(a, b):
    return lax.dot_general(a, b, (((1,), (1,)), ((), ())), preferred_element_type=F32)


def _dot_tn(a, b):
    return lax.dot_general(a, b, (((0,), (0,)), ((), ())), preferred_element_type=F32)


def _layer_norm(r, g, b):
    mu = jnp.mean(r, axis=-1, keepdims=True)
    d = r - mu
    var = jnp.mean(d * d, axis=-1, keepdims=True)
    return d * lax.rsqrt(var + LN_EPS) * g + b


def _gelu(x):
    return 0.5 * x * (1.0 + lax.erf(x * SQRT_HALF))


def _inproj_body(x_ref, w_ref, o_ref, xb_ref):
    @pl.when(pl.program_id(1) == 0)
    def _():
        xb_ref[...] = x_ref[...].astype(BF16)

    o_ref[...] = _dot(xb_ref[...], w_ref[...]).astype(BF16)


def _in_proj(x2, w_bf, col0):
    n, d = x2.shape
    width = w_bf.shape[1] - col0
    tm, tn = min(1024, n), 1024
    assert col0 % tn == 0 and width % tn == 0
    return pl.pallas_call(
        _inproj_body, grid=(n // tm, width // tn),
        in_specs=[pl.BlockSpec((tm, d), lambda i, j: (i, 0)),
                  pl.BlockSpec((d, tn), lambda i, j: (0, j + col0 // tn))],
        out_specs=pl.BlockSpec((tm, tn), lambda i, j: (i, j)),
        out_shape=jax.ShapeDtypeStruct((n, width), BF16),
        scratch_shapes=[pltpu.VMEM((tm, d), BF16)],
        compiler_params=_params(("arbitrary", "arbitrary")), name="in_proj",
    )(x2, w_bf)


def _retention_body(x_ref, pos_ref, freq_ref, sign_ref, wa_ref, wp_ref, mask_ref, xi_ref, zeta_ref, cd_ref,
                    gain_ref, o_ref, state_ref, yg_ref, cos_ref, sin_ref, *, n_chunks):
    @pl.when(pl.program_id(1) == 0)
    def _():
        state_ref[...] = jnp.zeros_like(state_ref)

    ang = pos_ref[...].astype(F32) * freq_ref[...]
    cos_ref[...] = jnp.cos(ang)
    sin_ref[...] = jnp.sin(ang) * sign_ref[...]

    scale = RET_QK_DIM ** -0.5
    half = RET_QK_DIM // 2
    dk, dv = RET_QK_DIM, RET_V_DIM
    k_col, v_col, g_col = RET_HEADS * dk, 2 * RET_HEADS * dk, 2 * RET_HEADS * dk + RET_HEADS * dv
    xb = x_ref[...].astype(BF16)
    for hd in range(RET_HEADS):
        if hd % 2 == 0:
            qq = _dot(xb, wa_ref[:, hd * dk:(hd + 2) * dk])
            kk = _dot(xb, wa_ref[:, k_col + hd * dk:k_col + (hd + 2) * dk])
        v_all = _dot(xb, wa_ref[:, v_col + hd * dv:v_col + (hd + 1) * dv])
        g_all = _dot(xb, wa_ref[:, g_col + hd * dv:g_col + (hd + 1) * dv])
        pair = (hd % 2) * dk
        gain = gain_ref[:, hd * dv:(hd + 1) * dv]
        for c in range(n_chunks):
            lo, hi = c * CHUNK, (c + 1) * CHUNK
            cos, sin = cos_ref[lo:hi, :], sin_ref[lo:hi, :]
            q, k = qq[lo:hi, pair:pair + dk], kk[lo:hi, pair:pair + dk]
            vb = v_all[lo:hi, :].astype(BF16)
            g = g_all[lo:hi, :]
            qr = q * cos + pltpu.roll(q, half, 1) * sin
            kr = (k * cos + pltpu.roll(k, half, 1) * sin) * scale
            scores = _dot_nt(qr.astype(BF16), kr.astype(BF16)) * mask_ref[hd]
            state = state_ref[hd]
            lhs = jnp.concatenate([scores.astype(BF16), (qr * xi_ref[hd]).astype(BF16)], axis=1)
            rhs = jnp.concatenate([vb, state.astype(BF16)], axis=0)
            out = _dot(lhs, rhs)
            state_ref[hd] = state * cd_ref[hd] + _dot_tn((kr * zeta_ref[hd]).astype(BF16), vb)
            mu = jnp.mean(out, axis=-1, keepdims=True)
            d = out - mu
            var = jnp.mean(d * d, axis=-1, keepdims=True)
            yn = d * lax.rsqrt(var + LN_EPS) * gain
            yg_ref[lo:hi, hd * dv:(hd + 1) * dv] = (g * jax.nn.sigmoid(g) * yn).astype(BF16)
    o_ref[...] = _dot(yg_ref[...], wp_ref[...]).astype(BF16)


def _retention_branch(x2, pos_col, freq2, sign2, wa_bf, wp_bf, consts, gain, batch, seq):
    n, d = x2.shape
    tr = min(256, seq)
    nr = seq // tr
    mask, xi_b, zeta_b, cd_b = consts
    vw = RET_HEADS * RET_V_DIM

    def once(shape):
        return pl.BlockSpec(shape, lambda b, r: (0,) * len(shape), pipeline_mode=pl.Buffered(1))

    rows = lambda w: pl.BlockSpec((tr, w), lambda b, r: (b * nr + r, 0))
    return pl.pallas_call(
        functools.partial(_retention_body, n_chunks=tr // CHUNK),
        grid=(batch, nr),
        in_specs=[rows(d), rows(1), once((1, LANES)), once((1, LANES)),
                  once((d, 2 * RET_HEADS * RET_QK_DIM + 2 * vw)), once((vw, d)),
                  once((RET_HEADS, CHUNK, CHUNK)), once((RET_HEADS, CHUNK, RET_QK_DIM)),
                  once((RET_HEADS, CHUNK, RET_QK_DIM)), once((RET_HEADS, 1, RET_V_DIM)), once((1, vw))],
        out_specs=rows(d),
        out_shape=jax.ShapeDtypeStruct((n, d), BF16),
        scratch_shapes=[pltpu.VMEM((RET_HEADS, RET_QK_DIM, RET_V_DIM), F32), pltpu.VMEM((tr, vw), BF16),
                        pltpu.VMEM((tr, LANES), F32), pltpu.VMEM((tr, LANES), F32)],
        compiler_params=_params(("arbitrary", "arbitrary")), name="retention_branch",
    )(x2, pos_col, freq2, sign2, wa_bf, wp_bf, mask, xi_b, zeta_b, cd_b, gain)


def _sgu_merge_body(u_ref, v_ref, ga_ref, gb_ref, ya_ref, lng_ref, lnb_ref, ws_ref, bs_ref, bg_ref, wp_ref,
                    o_ref, wtril_ref, ys_ref, *, n_chunks):
    @pl.when(pl.program_id(0) == 0)
    def _():
        r = lax.broadcasted_iota(jnp.int32, (CHUNK, CHUNK), 0)
        c = lax.broadcasted_iota(jnp.int32, (CHUNK, CHUNK), 1)
        for g in range(SGU_GROUPS):
            wtril_ref[g] = jnp.where(r >= c, ws_ref[g], 0.0).astype(BF16)

    v = _gelu(v_ref[...].astype(F32))
    vn = _layer_norm(v, lng_ref[...], lnb_ref[...]).astype(BF16)
    for g in range(SGU_GROUPS):
        cols = pl.ds(g * CHUNK, CHUNK)
        lo = g * CHUNK
        rhs = jnp.concatenate([vn[c * CHUNK:(c + 1) * CHUNK, lo:lo + CHUNK] for c in range(n_chunks)], axis=1)
        mix = _dot(wtril_ref[g], rhs)
        bias = bs_ref[g]
        for c in range(n_chunks):
            rows = pl.ds(c * CHUNK, CHUNK)
            u = _gelu(u_ref[rows, cols].astype(F32))
            ys_ref[rows, cols] = (u * (mix[:, c * CHUNK:(c + 1) * CHUNK] + bias)).astype(BF16)

    yb = _dot(ys_ref[...], wp_ref[...])
    sa = jax.nn.sigmoid(ga_ref[...].astype(F32) + bg_ref[0:1, :])
    sb = jax.nn.sigmoid(gb_ref[...].astype(F32) + bg_ref[1:2, :])
    o_ref[...] = (sa * ya_ref[...].astype(F32) + sb * yb).astype(BF16)


def _sgu_merge(h, y_a, ln_g, ln_b, w_s, b_s_b, b_gate, wp_bf):
    n, d = y_a.shape
    ts = min(256, n)
    tile = lambda blk: pl.BlockSpec((ts, d), lambda i: (i, blk))
    row = pl.BlockSpec((1, d), lambda i: (0, 0))
    full3 = pl.BlockSpec((SGU_GROUPS, CHUNK, CHUNK), lambda i: (0, 0, 0))
    return pl.pallas_call(
        functools.partial(_sgu_merge_body, n_chunks=ts // CHUNK),
        grid=(n // ts,),
        in_specs=[tile(0), tile(1), tile(2), tile(3), tile(0), row, row, full3, full3,
                  pl.BlockSpec((2, d), lambda i: (0, 0)),
                  pl.BlockSpec((d, d), lambda i: (0, 0), pipeline_mode=pl.Buffered(1))],
        out_specs=tile(0),
        out_shape=jax.ShapeDtypeStruct((n, d), BF16),
        scratch_shapes=[pltpu.VMEM((SGU_GROUPS, CHUNK, CHUNK), BF16), pltpu.VMEM((ts, d), BF16)],
        compiler_params=_params(("arbitrary",)), name="sgu_merge",
    )(h, h, h, h, y_a, ln_g, ln_b, w_s, b_s_b, b_gate, wp_bf)


def _outproj_router_body(m_ref, x_ref, w_ref, g_ref, b_ref, whi_ref, wlo_ref, rb_ref,
                         o_ref, orow_ref, ids_ref, cw_ref, cnt_ref, carry_ref, tri_ref, *, tm):
    @pl.when(pl.program_id(0) == 0)
    def _():
        carry_ref[...] = jnp.zeros_like(carry_ref)
        r = lax.broadcasted_iota(jnp.int32, (tm, tm), 0)
        c = lax.broadcasted_iota(jnp.int32, (tm, tm), 1)
        tri_ref[...] = jnp.where(r < c, 1.0, 0.0).astype(BF16)

    r = DN_ALPHA * x_ref[...] + _dot(m_ref[...], w_ref[...])
    x = _layer_norm(r, g_ref[...], b_ref[...])
    o_ref[...] = x
    orow_ref[:, 0, :] = x

    hi = x.astype(BF16)
    lo = (x - hi.astype(F32)).astype(BF16)
    whi = whi_ref[...]
    logits = _dot_nt(whi, hi) + _dot_nt(whi, lo) + _dot_nt(wlo_ref[...], hi) + rb_ref[...]
    el = logits[0:N_EXPERTS, :]
    gl = logits[N_EXPERTS:N_EXPERTS + N_GROUPS, :]

    gi = lax.broadcasted_iota(jnp.int32, gl.shape, 0)
    gmax = jnp.max(gl, axis=0, keepdims=True)
    gidx = jnp.min(jnp.where(gl == gmax, gi, N_GROUPS), axis=0, keepdims=True)
    p_group = 1.0 / jnp.sum(jnp.exp(gl - gmax), axis=0, keepdims=True)

    ei = lax.broadcasted_iota(jnp.int32, el.shape, 0)
    first = gidx * EXPERTS_PER_GROUP
    in_group = (ei >= first) & (ei < first + EXPERTS_PER_GROUP)
    m1 = jnp.where(in_group, el, -jnp.inf)
    v1 = jnp.max(m1, axis=0, keepdims=True)
    i1 = jnp.min(jnp.where(in_group & (m1 == v1), ei, N_EXPERTS), axis=0, keepdims=True)
    rest = in_group & (ei != i1)
    m2 = jnp.where(rest, el, -jnp.inf)
    v2 = jnp.max(m2, axis=0, keepdims=True)
    i2 = jnp.min(jnp.where(rest & (m2 == v2), ei, N_EXPERTS), axis=0, keepdims=True)
    t = jnp.exp(v2 - v1)
    p1 = 1.0 / (1.0 + t)
    p2 = t * p1

    sel1, sel2 = ei == i1, ei == i2
    onehot = jnp.where(sel1 | sel2, 1.0, 0.0)
    before = _dot(onehot.astype(BF16), tri_ref[...]) + carry_ref[:, 0:1]
    r1 = jnp.sum(jnp.where(sel1, before, 0.0), axis=0, keepdims=True)
    r2 = jnp.sum(jnp.where(sel2, before, 0.0), axis=0, keepdims=True)
    carry_ref[...] = carry_ref[...] + jnp.sum(onehot, axis=1, keepdims=True)

    ids_ref[0:1, :] = i1
    ids_ref[1:2, :] = i2
    ids_ref[2:3, :] = r1.astype(jnp.int32)
    ids_ref[3:4, :] = r2.astype(jnp.int32)
    cw_ref[0:1, :] = p_group * p1
    cw_ref[1:2, :] = p_group * p2
    cnt_ref[...] = carry_ref[...]


def _out_proj_router(merged, x2, wo_bf, ln_g, ln_b, w_hi, w_lo, bias_col):
    n, d = x2.shape
    tm = min(512, n)
    tile = pl.BlockSpec((tm, d), lambda i: (i, 0))
    row = pl.BlockSpec((1, d), lambda i: (0, 0))
    full = pl.BlockSpec((ROUTER_ROWS, d), lambda i: (0, 0))
    return pl.pallas_call(
        functools.partial(_outproj_router_body, tm=tm), grid=(n // tm,),
        in_specs=[tile, tile, pl.BlockSpec((d, d), lambda i: (0, 0), pipeline_mode=pl.Buffered(1)), row, row,
                  full, full, pl.BlockSpec((ROUTER_ROWS, 1), lambda i: (0, 0))],
        out_specs=[tile, pl.BlockSpec((tm, 1, d), lambda i: (i, 0, 0)),
                   pl.BlockSpec((4, tm), lambda i: (0, i)),
                   pl.BlockSpec((2, tm), lambda i: (0, i)),
                   pl.BlockSpec((N_EXPERTS, LANES), lambda i: (0, 0))],
        out_shape=[jax.ShapeDtypeStruct((n, d), F32), jax.ShapeDtypeStruct((n, 1, d), F32),
                   jax.ShapeDtypeStruct((4, n), jnp.int32),
                   jax.ShapeDtypeStruct((2, n), F32),
                   jax.ShapeDtypeStruct((N_EXPERTS, LANES), F32)],
        scratch_shapes=[pltpu.VMEM((N_EXPERTS, LANES), F32), pltpu.VMEM((tm, tm), BF16)],
        compiler_params=_params(("arbitrary",)), name="out_proj_ln1_router",
    )(merged, x2, wo_bf, ln_g, ln_b, w_hi, w_lo, bias_col)


def _routing_tables(ids, counts, n, max_pairs):
    te = EXPERT_TILE
    n_tiles = 2 * n // te
    i32 = jnp.int32
    row_end = jnp.cumsum(counts)
    row_start = row_end - counts
    dest1 = row_start[ids[0]] + ids[2]
    dest2 = row_start[ids[1]] + ids[3]
    ne_cum = jnp.cumsum((counts > 0).astype(i32))
    tile_lo = jnp.arange(n_tiles, dtype=i32) * te
    first_e = jnp.sum((row_end[None, :] <= tile_lo[:, None]).astype(i32), axis=1)
    last_e = jnp.sum((row_end[None, :] <= (tile_lo + te - 1)[:, None]).astype(i32), axis=1)
    pairs_t = ne_cum[last_e] - ne_cum[first_e] + 1
    pair_end = jnp.cumsum(pairs_t)
    pair_start = pair_end - pairs_t
    n_pairs = pair_end[-1]
    p = jnp.minimum(jnp.arange(max_pairs, dtype=i32), n_pairs - 1)
    p_tile = jnp.sum((pair_end[None, :] <= p[:, None]).astype(i32), axis=1)
    rank = ne_cum[first_e[p_tile]] - 1 + (p - pair_start[p_tile])
    p_exp = jnp.sum((ne_cum[None, :] <= rank[:, None]).astype(i32), axis=1)
    p_lo = jnp.maximum(row_start[p_exp], p_tile * te) - p_tile * te
    p_hi = jnp.minimum(row_end[p_exp], (p_tile + 1) * te) - p_tile * te
    return (dest1, dest2, p_tile.astype(i32), p_exp.astype(i32), p_lo.astype(i32), p_hi.astype(i32),
            n_pairs.reshape(1).astype(i32))


def _row_copy(src, dst, sem):
    return pltpu.make_async_copy(src, dst, sem)


def _expert_body(ptile_ref, pexp_ref, plo_ref, phi_ref, np_ref, d1_ref, d2_ref,
                 x1_hbm, w1_ref, w3_ref, w2_ref, y2_hbm,
                 w1b, w3b, w2b, xbuf0, xbuf1, ybuf0, ybuf1, tok_s, inv_s, gsem, ssem, *, n, n_tiles):
    te = EXPERT_TILE
    p = pl.program_id(0)
    valid = p < np_ref[0]
    t = ptile_ref[p]
    pm1 = jnp.maximum(p - 1, 0)
    first = (p == 0) | (ptile_ref[pm1] != t)
    new_expert = (p == 0) | (pexp_ref[pm1] != pexp_ref[p])
    lo, hi = plo_ref[p], phi_ref[p]

    def gather(tile, dst, j, priority=0):
        _row_copy(x1_hbm.at[tok_s[tile * te + j]], dst.at[pl.ds(j, 1)], gsem).start(priority=priority)

    def scatter(tile, src, j, priority=0):
        _row_copy(src.at[pl.ds(j, 1)], y2_hbm.at[inv_s[tile * te + j]], ssem).start(priority=priority)

    def wait_gather(dst):
        _row_copy(x1_hbm.at[pl.ds(0, te), 0], dst, gsem).wait()

    def wait_scatter(src):
        _row_copy(src, y2_hbm.at[pl.ds(0, te), 0], ssem).wait()

    def issue_loop(fn, tile, buf):
        def one(j, c):
            fn(tile, buf, j)
            return c
        lax.fori_loop(0, te, one, 0, unroll=8)

    @pl.when(p == 0)
    def _():
        def invert(tk, c):
            a, b = d1_ref[tk], d2_ref[tk]
            tok_s[a] = tk
            inv_s[a] = tk
            tok_s[b] = tk
            inv_s[b] = tk + n
            return c
        lax.fori_loop(0, n, invert, 0, unroll=8)
        ybuf0[...] = jnp.zeros_like(ybuf0)
        ybuf1[...] = jnp.zeros_like(ybuf1)
        issue_loop(gather, 0, xbuf0)

    @pl.when(valid & new_expert)
    def _():
        w1b[...] = w1_ref[0].astype(BF16)
        w3b[...] = w3_ref[0].astype(BF16)
        w2b[...] = w2_ref[0].astype(BF16)

    def ffn(xb_ref, yb_ref):
        xb = xb_ref[...].astype(BF16)
        a = _dot(xb, w1b[...])
        b = _dot(xb, w3b[...])
        y = _dot((a * jax.nn.sigmoid(a) * b).astype(BF16), w2b[...])
        rows = lax.broadcasted_iota(jnp.int32, (te, 1), 0)
        yb_ref[...] = jnp.where((rows >= lo) & (rows < hi), y, yb_ref[...])

    for par, (xb, yb, xo, yo) in enumerate(((xbuf0, ybuf0, xbuf1, ybuf1), (xbuf1, ybuf1, xbuf0, ybuf0))):
        mine = valid & ((t & 1) == par)
        interior = first & (t >= 1) & (t < n_tiles - 1)

        @pl.when(mine & interior)
        def _():
            wait_gather(xb)

            @pl.when(t >= 2)
            def _():
                wait_scatter(yb)

            for j in range(te):
                gather(t + 1, xo, j, priority=j % 2)

            @pl.when(hi > 0)
            def _():
                for j in range(te):
                    scatter(t - 1, yo, j, priority=j % 2)
                ffn(xb, yb)

        @pl.when(mine & jnp.logical_not(interior))
        def _():
            @pl.when(first & (t == 0))
            def _():
                wait_gather(xb)
                if n_tiles > 1:
                    issue_loop(gather, 1, xo)

            @pl.when(first & (t == n_tiles - 1) & (t >= 1))
            def _():
                wait_gather(xb)

                @pl.when(t >= 2)
                def _():
                    wait_scatter(yb)

                issue_loop(scatter, t - 1, yo)

            ffn(xb, yb)

    @pl.when(p == pl.num_programs(0) - 1)
    def _():
        last = n_tiles - 1
        y_last, y_prev = (ybuf0, ybuf1) if last % 2 == 0 else (ybuf1, ybuf0)
        if n_tiles >= 2:
            wait_scatter(y_prev)
        issue_loop(scatter, last, y_last)
        wait_scatter(y_last)


def _experts(x1_rows, tables, w1, w3, w2):
    n, _, d = x1_rows.shape
    de = w1.shape[2]
    te = EXPERT_TILE
    n_tiles = 2 * n // te
    max_pairs = n_tiles + N_EXPERTS - 1
    wmap = lambda p, pt, pe, *_: (pe[p], 0, 0)
    return pl.pallas_call(
        functools.partial(_expert_body, n=n, n_tiles=n_tiles),
        grid_spec=pltpu.PrefetchScalarGridSpec(
            num_scalar_prefetch=7, grid=(max_pairs,),
            in_specs=[pl.BlockSpec(memory_space=pl.ANY),
                      pl.BlockSpec((1, d, de), wmap), pl.BlockSpec((1, d, de), wmap), pl.BlockSpec((1, de, d), wmap)],
            out_specs=pl.BlockSpec(memory_space=pl.ANY),
            scratch_shapes=[pltpu.VMEM((d, de), BF16), pltpu.VMEM((d, de), BF16), pltpu.VMEM((de, d), BF16),
                            pltpu.VMEM((te, d), F32), pltpu.VMEM((te, d), F32),
                            pltpu.VMEM((te, d), F32), pltpu.VMEM((te, d), F32),
                            pltpu.SMEM((2 * n,), jnp.int32), pltpu.SMEM((2 * n,), jnp.int32),
                            pltpu.SemaphoreType.DMA(()), pltpu.SemaphoreType.DMA(())]),
        out_shape=jax.ShapeDtypeStruct((2 * n, 1, d), F32),
        compiler_params=_params(("arbitrary",)), name="experts",
    )(*tables, x1_rows, w1, w3, w2)


def _combine_body(x_ref, ya_ref, yb_ref, cw_ref, g_ref, b_ref, o_ref):
    y = cw_ref[:, 0:1] * ya_ref[:, 0, :] + cw_ref[:, 1:2] * yb_ref[:, 0, :]
    o_ref[...] = _layer_norm(DN_ALPHA * x_ref[...] + y, g_ref[...], b_ref[...])


def _combine(y2, x1, cw_col, ln_g, ln_b):
    n, d = x1.shape
    tm = min(512, n)
    nt = n // tm
    tile = pl.BlockSpec((tm, d), lambda i: (i, 0))
    row = pl.BlockSpec((1, d), lambda i: (0, 0))
    return pl.pallas_call(
        _combine_body, grid=(nt,),
        in_specs=[tile, pl.BlockSpec((tm, 1, d), lambda i: (i, 0, 0)),
                  pl.BlockSpec((tm, 1, d), lambda i: (i + nt, 0, 0)),
                  pl.BlockSpec((tm, 2), lambda i: (i, 0)), row, row],
        out_specs=tile,
        out_shape=jax.ShapeDtypeStruct((n, d), F32),
        compiler_params=_params(("arbitrary",)), name="combine_ln2",
    )(x1, y2, y2, cw_col, ln_g, ln_b)


def _retention_consts():
    h, c = RET_HEADS, CHUNK
    log_gamma = jnp.log1p(-jnp.exp2(-5.0 - jnp.arange(h, dtype=F32)))
    idx = jnp.arange(c, dtype=F32)
    rel = idx[:, None] - idx[None, :]
    mask = jnp.where(rel >= 0, jnp.exp(log_gamma[:, None, None] * jnp.maximum(rel, 0.0)), 0.0)
    xi = jnp.exp(log_gamma[:, None] * (idx + 1.0))
    zeta = jnp.exp(log_gamma[:, None] * (c - 1.0 - idx))
    chunk_decay = jnp.exp(log_gamma * c)
    xi_b = jnp.broadcast_to(xi[:, :, None], (h, c, RET_QK_DIM))
    zeta_b = jnp.broadcast_to(zeta[:, :, None], (h, c, RET_QK_DIM))
    cd_b = jnp.broadcast_to(chunk_decay[:, None, None], (h, 1, RET_V_DIM))
    return mask, xi_b, zeta_b, cd_b


def _rope_consts():
    half = RET_QK_DIM // 2
    freq = ROPE_THETA ** (-jnp.arange(half, dtype=F32) / half)
    freq2 = jnp.concatenate([freq, freq])[None, :]
    sign2 = jnp.concatenate([-jnp.ones((half,), F32), jnp.ones((half,), F32)])[None, :]
    return freq2, sign2


def kernel(x, positions, w_in, b_gate, ret_gn_g, sgu_ln_g, sgu_ln_b, sgu_w, sgu_b, w_proj_ret, w_proj_sgu,
           w_out, ln1_g, ln1_b, w_group, b_group, w_er, b_er, w1, w3, w2, ln2_g, ln2_b):
    batch, seq, d = x.shape
    n = batch * seq
    qk_w, v_w = RET_HEADS * RET_QK_DIM, RET_HEADS * RET_V_DIM
    assert d == v_w == SGU_GROUPS * CHUNK and seq % CHUNK == 0
    ret_cols = 2 * qk_w + 2 * v_w
    max_pairs = (2 * n) // EXPERT_TILE + N_EXPERTS - 1

    freq2, sign2 = _rope_consts()
    pos_col = positions.reshape(n, 1)
    ret_consts = _retention_consts()

    xc = x.reshape(n, d)
    for l in range(w_in.shape[0]):
        w_bf = w_in[l].astype(BF16)
        y_a = _retention_branch(xc, pos_col, freq2, sign2, w_bf, w_proj_ret[l].astype(BF16), ret_consts,
                                ret_gn_g[l][None, :], batch, seq)
        h = _in_proj(xc, w_bf, ret_cols)
        bias_b = jnp.broadcast_to(sgu_b[l][:, :, None], (SGU_GROUPS, CHUNK, CHUNK))
        merged = _sgu_merge(h, y_a, sgu_ln_g[l][None, :], sgu_ln_b[l][None, :], sgu_w[l], bias_b, b_gate[l],
                            w_proj_sgu[l].astype(BF16))

        w_r = jnp.concatenate([w_er[l], w_group[l]], axis=1).T
        w_r = jnp.pad(w_r, ((0, ROUTER_ROWS - w_r.shape[0]), (0, 0)))
        w_hi = w_r.astype(BF16)
        w_lo = (w_r - w_hi.astype(F32)).astype(BF16)
        bias = jnp.pad(jnp.concatenate([b_er[l], b_group[l]]), (0, ROUTER_ROWS - N_EXPERTS - N_GROUPS))[:, None]
        x1, x1_rows, ids, cw, cnt = _out_proj_router(merged, xc, w_out[l].astype(BF16), ln1_g[l][None, :],
                                                     ln1_b[l][None, :], w_hi, w_lo, bias)

        counts = cnt[:, 0].astype(jnp.int32)
        dest1, dest2, p_tile, p_exp, p_lo, p_hi, n_pairs = _routing_tables(ids, counts, n, max_pairs)
        y2 = _experts(x1_rows, (p_tile, p_exp, p_lo, p_hi, n_pairs, dest1, dest2), w1[l], w3[l], w2[l])
        xc = _combine(y2, x1, cw.T, ln2_g[l][None, :], ln2_b[l][None, :])
    return xc.reshape(batch, seq, d)
```

```python
import functools

import numpy as np
import jax
import jax.numpy as jnp
from jax import lax
from jax.experimental import pallas as pl
from jax.experimental.pallas import tpu as pltpu

F32 = jnp.float32
BF16 = jnp.bfloat16

RET_HEADS = 8
RET_QK_DIM = 128
RET_V_DIM = 256
CHUNK = 128
ROPE_THETA = 10000.0
SGU_GROUPS = 16
N_GROUPS = 4
EXPERTS_PER_GROUP = 8
N_EXPERTS = N_GROUPS * EXPERTS_PER_GROUP
ROUTER_ROWS = 40
LN_EPS = 1e-5
DEPTH = 1
DN_ALPHA = (2 * DEPTH) ** 0.25
SQRT_HALF = np.sqrt(0.5).astype(np.float32)

LANES = 128
VMEM_LIMIT = 52 * 1024 * 1024
EXPERT_TILE = 256


def _params(sem, vmem=VMEM_LIMIT):
    return pltpu.CompilerParams(dimension_semantics=sem, vmem_limit_bytes=vmem)


def _dot(a, b):
    return jnp.dot(a, b, preferred_element_type=F32)


def _dot_nt(a, b):
    return lax.dot_general(a, b, (((1,), (1,)), ((), ())), preferred_element_type=F32)


def _dot_tn(a, b):
    return lax.dot_general(a, b, (((0,), (0,)), ((), ())), preferred_element_type=F32)


def _layer_norm(r, g, b):
    mu = jnp.mean(r, axis=-1, keepdims=True)
    d = r - mu
    var = jnp.mean(d * d, axis=-1, keepdims=True)
    return d * lax.rsqrt(var + LN_EPS) * g + b


def _gelu(x):
    return 0.5 * x * (1.0 + lax.erf(x * SQRT_HALF))


def _inproj_body(x_ref, w_ref, o_ref, wb_ref):
    @pl.when(pl.program_id(1) == 0)
    def _():
        wb_ref[...] = w_ref[...].astype(BF16)

    o_ref[...] = _dot(x_ref[...], wb_ref[...]).astype(BF16)


def _in_proj(xb, w, col0):
    n, d = xb.shape
    width = w.shape[1] - col0
    tm, tn = min(1024, n), 1024
    assert col0 % tn == 0 and width % tn == 0
    return pl.pallas_call(
        _inproj_body, grid=(width // tn, n // tm),
        in_specs=[pl.BlockSpec((tm, d), lambda j, i: (i, 0)),
                  pl.BlockSpec((d, tn), lambda j, i: (0, j + col0 // tn))],
        out_specs=pl.BlockSpec((tm, tn), lambda j, i: (i, j)),
        out_shape=jax.ShapeDtypeStruct((n, width), BF16),
        scratch_shapes=[pltpu.VMEM((d, tn), BF16)],
        compiler_params=_params(("arbitrary", "arbitrary")), name="in_proj",
    )(xb, w)


HEAD_COLS = 2 * RET_QK_DIM + 2 * RET_V_DIM


def _regroup_body(q_ref, k_ref, v_ref, g_ref, o_ref):
    dk, dv = RET_QK_DIM, RET_V_DIM
    o_ref[0, :, 0:dk] = q_ref[...].astype(BF16)
    o_ref[0, :, dk:2 * dk] = k_ref[...].astype(BF16)
    o_ref[0, :, 2 * dk:2 * dk + dv] = v_ref[...].astype(BF16)
    o_ref[0, :, 2 * dk + dv:] = g_ref[...].astype(BF16)


def _regroup_retention_weights(w):
    d = w.shape[0]
    k_blk = RET_HEADS
    v_blk = 2 * RET_HEADS * RET_QK_DIM // RET_V_DIM
    return pl.pallas_call(
        _regroup_body, grid=(RET_HEADS,),
        in_specs=[pl.BlockSpec((d, RET_QK_DIM), lambda h: (0, h)),
                  pl.BlockSpec((d, RET_QK_DIM), lambda h: (0, k_blk + h)),
                  pl.BlockSpec((d, RET_V_DIM), lambda h: (0, v_blk + h)),
                  pl.BlockSpec((d, RET_V_DIM), lambda h: (0, v_blk + RET_HEADS + h))],
        out_specs=pl.BlockSpec((1, d, HEAD_COLS), lambda h: (h, 0, 0)),
        out_shape=jax.ShapeDtypeStruct((RET_HEADS, d, HEAD_COLS), BF16),
        compiler_params=_params(("arbitrary",)), name="regroup_retention_weights",
    )(w, w, w, w)


def _retention_body(x_ref, pos_ref, freq_ref, sign_ref, wa_ref, wp_ref, mask_ref, xi_ref, zeta_ref, cd_ref,
                    gain_ref, o_ref, xb_ref, state_ref, yg_ref, cos_ref, sin_ref, *, n_chunks):
    @pl.when(pl.program_id(1) == 0)
    def _():
        state_ref[...] = jnp.zeros_like(state_ref)

    ang = pos_ref[...].astype(F32) * freq_ref[...]
    cos_ref[...] = jnp.cos(ang)
    sin_ref[...] = jnp.sin(ang) * sign_ref[...]

    scale = RET_QK_DIM ** -0.5
    half = RET_QK_DIM // 2
    dk, dv = RET_QK_DIM, RET_V_DIM
    xb = x_ref[...].astype(BF16)
    xb_ref[...] = xb
    for hd in range(RET_HEADS):
        proj = _dot(xb, wa_ref[hd])
        gain = gain_ref[:, hd * dv:(hd + 1) * dv]
        for c in range(n_chunks):
            lo, hi = c * CHUNK, (c + 1) * CHUNK
            cos, sin = cos_ref[lo:hi, :], sin_ref[lo:hi, :]
            q, k = proj[lo:hi, 0:dk], proj[lo:hi, dk:2 * dk]
            vb = proj[lo:hi, 2 * dk:2 * dk + dv].astype(BF16)
            g = proj[lo:hi, 2 * dk + dv:]
            qr = q * cos + pltpu.roll(q, half, 1) * sin
            kr = (k * cos + pltpu.roll(k, half, 1) * sin) * scale
            scores = _dot_nt(qr.astype(BF16), kr.astype(BF16)) * mask_ref[hd]
            state = state_ref[hd]
            lhs = jnp.concatenate([scores.astype(BF16), (qr * xi_ref[hd]).astype(BF16)], axis=1)
            rhs = jnp.concatenate([vb, state.astype(BF16)], axis=0)
            out = _dot(lhs, rhs)
            state_ref[hd] = state * cd_ref[hd] + _dot_tn((kr * zeta_ref[hd]).astype(BF16), vb)
            mu = jnp.mean(out, axis=-1, keepdims=True)
            d = out - mu
            var = jnp.mean(d * d, axis=-1, keepdims=True)
            yn = d * lax.rsqrt(var + LN_EPS) * gain
            yg_ref[lo:hi, hd * dv:(hd + 1) * dv] = (g * jax.nn.sigmoid(g) * yn).astype(BF16)
    o_ref[...] = _dot(yg_ref[...], wp_ref[...]).astype(BF16)


def _retention_branch(x2, pos_col, freq2, sign2, wa_bf, wp_bf, consts, gain, batch, seq):
    n, d = x2.shape
    tr = min(256, seq)
    nr = seq // tr
    mask, xi_b, zeta_b, cd_b = consts
    vw = RET_HEADS * RET_V_DIM

    def once(shape):
        return pl.BlockSpec(shape, lambda b, r: (0,) * len(shape), pipeline_mode=pl.Buffered(1))

    rows = lambda w: pl.BlockSpec((tr, w), lambda b, r: (b * nr + r, 0))
    return pl.pallas_call(
        functools.partial(_retention_body, n_chunks=tr // CHUNK),
        grid=(batch, nr),
        in_specs=[rows(d), rows(1), once((1, LANES)), once((1, LANES)),
                  once((RET_HEADS, d, HEAD_COLS)), once((vw, d)),
                  once((RET_HEADS, CHUNK, CHUNK)), once((RET_HEADS, CHUNK, RET_QK_DIM)),
                  once((RET_HEADS, CHUNK, RET_QK_DIM)), once((RET_HEADS, 1, RET_V_DIM)), once((1, vw))],
        out_specs=[rows(d), rows(d)],
        out_shape=[jax.ShapeDtypeStruct((n, d), BF16), jax.ShapeDtypeStruct((n, d), BF16)],
        scratch_shapes=[pltpu.VMEM((RET_HEADS, RET_QK_DIM, RET_V_DIM), F32), pltpu.VMEM((tr, vw), BF16),
                        pltpu.VMEM((tr, LANES), F32), pltpu.VMEM((tr, LANES), F32)],
        compiler_params=_params(("arbitrary", "arbitrary")), name="retention_branch",
    )(x2, pos_col, freq2, sign2, wa_bf, wp_bf, mask, xi_b, zeta_b, cd_b, gain)


def _sgu_merge_body(u_ref, v_ref, ga_ref, gb_ref, ya_ref, lng_ref, lnb_ref, ws_ref, bs_ref, bg_ref, wp_ref,
                    o_ref, wtril_ref, ys_ref, *, n_chunks):
    @pl.when(pl.program_id(0) == 0)
    def _():
        r = lax.broadcasted_iota(jnp.int32, (CHUNK, CHUNK), 0)
        c = lax.broadcasted_iota(jnp.int32, (CHUNK, CHUNK), 1)
        for g in range(SGU_GROUPS):
            wtril_ref[g] = jnp.where(r >= c, ws_ref[g], 0.0).astype(BF16)

    v = _gelu(v_ref[...].astype(F32))
    vn = _layer_norm(v, lng_ref[...], lnb_ref[...]).astype(BF16)
    for g in range(SGU_GROUPS):
        cols = pl.ds(g * CHUNK, CHUNK)
        lo = g * CHUNK
        rhs = jnp.concatenate([vn[c * CHUNK:(c + 1) * CHUNK, lo:lo + CHUNK] for c in range(n_chunks)], axis=1)
        mix = _dot(wtril_ref[g], rhs)
        bias = bs_ref[g]
        for c in range(n_chunks):
            rows = pl.ds(c * CHUNK, CHUNK)
            u = _gelu(u_ref[rows, cols].astype(F32))
            ys_ref[rows, cols] = (u * (mix[:, c * CHUNK:(c + 1) * CHUNK] + bias)).astype(BF16)

    yb = _dot(ys_ref[...], wp_ref[...])
    sa = jax.nn.sigmoid(ga_ref[...].astype(F32) + bg_ref[0:1, :])
    sb = jax.nn.sigmoid(gb_ref[...].astype(F32) + bg_ref[1:2, :])
    o_ref[...] = (sa * ya_ref[...].astype(F32) + sb * yb).astype(BF16)


def _sgu_merge(h, y_a, ln_g, ln_b, w_s, b_s_b, b_gate, wp_bf):
    n, d = y_a.shape
    ts = min(256, n)
    tile = lambda blk: pl.BlockSpec((ts, d), lambda i: (i, blk))
    row = pl.BlockSpec((1, d), lambda i: (0, 0))
    full3 = pl.BlockSpec((SGU_GROUPS, CHUNK, CHUNK), lambda i: (0, 0, 0))
    return pl.pallas_call(
        functools.partial(_sgu_merge_body, n_chunks=ts // CHUNK),
        grid=(n // ts,),
        in_specs=[tile(0), tile(1), tile(2), tile(3), tile(0), row, row, full3, full3,
                  pl.BlockSpec((2, d), lambda i: (0, 0)),
                  pl.BlockSpec((d, d), lambda i: (0, 0), pipeline_mode=pl.Buffered(1))],
        out_specs=tile(0),
        out_shape=jax.ShapeDtypeStruct((n, d), BF16),
        scratch_shapes=[pltpu.VMEM((SGU_GROUPS, CHUNK, CHUNK), BF16), pltpu.VMEM((ts, d), BF16)],
        compiler_params=_params(("arbitrary",)), name="sgu_merge",
    )(h, h, h, h, y_a, ln_g, ln_b, w_s, b_s_b, b_gate, wp_bf)


def _outproj_router_body(m_ref, x_ref, w_ref, g_ref, b_ref, whi_ref, wlo_ref, rb_ref,
                         o_ref, orow_ref, ids_ref, cw_ref, cnt_ref, carry_ref, tri_ref, *, tm):
    @pl.when(pl.program_id(0) == 0)
    def _():
        carry_ref[...] = jnp.zeros_like(carry_ref)
        r = lax.broadcasted_iota(jnp.int32, (tm, tm), 0)
        c = lax.broadcasted_iota(jnp.int32, (tm, tm), 1)
        tri_ref[...] = jnp.where(r < c, 1.0, 0.0).astype(BF16)

    r = DN_ALPHA * x_ref[...] + _dot(m_ref[...], w_ref[...])
    x = _layer_norm(r, g_ref[...], b_ref[...])
    o_ref[...] = x
    orow_ref[:, 0, :] = x

    hi = x.astype(BF16)
    lo = (x - hi.astype(F32)).astype(BF16)
    whi = whi_ref[...]
    logits = _dot_nt(whi, hi) + _dot_nt(whi, lo) + _dot_nt(wlo_ref[...], hi) + rb_ref[...]
    el = logits[0:N_EXPERTS, :]
    gl = logits[N_EXPERTS:N_EXPERTS + N_GROUPS, :]

    gi = lax.broadcasted_iota(jnp.int32, gl.shape, 0)
    gmax = jnp.max(gl, axis=0, keepdims=True)
    gidx = jnp.min(jnp.where(gl == gmax, gi, N_GROUPS), axis=0, keepdims=True)
    p_group = 1.0 / jnp.sum(jnp.exp(gl - gmax), axis=0, keepdims=True)

    ei = lax.broadcasted_iota(jnp.int32, el.shape, 0)
    first = gidx * EXPERTS_PER_GROUP
    in_group = (ei >= first) & (ei < first + EXPERTS_PER_GROUP)
    m1 = jnp.where(in_group, el, -jnp.inf)
    v1 = jnp.max(m1, axis=0, keepdims=True)
    i1 = jnp.min(jnp.where(in_group & (m1 == v1), ei, N_EXPERTS), axis=0, keepdims=True)
    rest = in_group & (ei != i1)
    m2 = jnp.where(rest, el, -jnp.inf)
    v2 = jnp.max(m2, axis=0, keepdims=True)
    i2 = jnp.min(jnp.where(rest & (m2 == v2), ei, N_EXPERTS), axis=0, keepdims=True)
    t = jnp.exp(v2 - v1)
    p1 = 1.0 / (1.0 + t)
    p2 = t * p1

    sel1, sel2 = ei == i1, ei == i2
    onehot = jnp.where(sel1 | sel2, 1.0, 0.0)
    before = _dot(onehot.astype(BF16), tri_ref[...]) + carry_ref[:, 0:1]
    r1 = jnp.sum(jnp.where(sel1, before, 0.0), axis=0, keepdims=True)
    r2 = jnp.sum(jnp.where(sel2, before, 0.0), axis=0, keepdims=True)
    carry_ref[...] = carry_ref[...] + jnp.sum(onehot, axis=1, keepdims=True)

    ids_ref[0:1, :] = i1
    ids_ref[1:2, :] = i2
    ids_ref[2:3, :] = r1.astype(jnp.int32)
    ids_ref[3:4, :] = r2.astype(jnp.int32)
    cw_ref[0:1, :] = p_group * p1
    cw_ref[1:2, :] = p_group * p2
    cnt_ref[...] = carry_ref[...]


def _out_proj_router(merged, x2, wo_bf, ln_g, ln_b, w_hi, w_lo, bias_col):
    n, d = x2.shape
    tm = min(512, n)
    tile = pl.BlockSpec((tm, d), lambda i: (i, 0))
    row = pl.BlockSpec((1, d), lambda i: (0, 0))
    full = pl.BlockSpec((ROUTER_ROWS, d), lambda i: (0, 0))
    return pl.pallas_call(
        functools.partial(_outproj_router_body, tm=tm), grid=(n // tm,),
        in_specs=[tile, tile, pl.BlockSpec((d, d), lambda i: (0, 0), pipeline_mode=pl.Buffered(1)), row, row,
                  full, full, pl.BlockSpec((ROUTER_ROWS, 1), lambda i: (0, 0))],
        out_specs=[tile, pl.BlockSpec((tm, 1, d), lambda i: (i, 0, 0)),
                   pl.BlockSpec((4, tm), lambda i: (0, i)),
                   pl.BlockSpec((2, tm), lambda i: (0, i)),
                   pl.BlockSpec((N_EXPERTS, LANES), lambda i: (0, 0))],
        out_shape=[jax.ShapeDtypeStruct((n, d), F32), jax.ShapeDtypeStruct((n, 1, d), F32),
                   jax.ShapeDtypeStruct((4, n), jnp.int32),
                   jax.ShapeDtypeStruct((2, n), F32),
                   jax.ShapeDtypeStruct((N_EXPERTS, LANES), F32)],
        scratch_shapes=[pltpu.VMEM((N_EXPERTS, LANES), F32), pltpu.VMEM((tm, tm), BF16)],
        compiler_params=_params(("arbitrary",)), name="out_proj_ln1_router",
    )(merged, x2, wo_bf, ln_g, ln_b, w_hi, w_lo, bias_col)


def _routing_tables(ids, counts, n, max_pairs):
    te = EXPERT_TILE
    n_tiles = 2 * n // te
    i32 = jnp.int32
    row_end = jnp.cumsum(counts)
    row_start = row_end - counts
    dest1 = row_start[ids[0]] + ids[2]
    dest2 = row_start[ids[1]] + ids[3]
    ne_cum = jnp.cumsum((counts > 0).astype(i32))
    tile_lo = jnp.arange(n_tiles, dtype=i32) * te
    first_e = jnp.sum((row_end[None, :] <= tile_lo[:, None]).astype(i32), axis=1)
    last_e = jnp.sum((row_end[None, :] <= (tile_lo + te - 1)[:, None]).astype(i32), axis=1)
    pairs_t = ne_cum[last_e] - ne_cum[first_e] + 1
    pair_end = jnp.cumsum(pairs_t)
    pair_start = pair_end - pairs_t
    n_pairs = pair_end[-1]
    p = jnp.minimum(jnp.arange(max_pairs, dtype=i32), n_pairs - 1)
    p_tile = jnp.sum((pair_end[None, :] <= p[:, None]).astype(i32), axis=1)
    rank = ne_cum[first_e[p_tile]] - 1 + (p - pair_start[p_tile])
    p_exp = jnp.sum((ne_cum[None, :] <= rank[:, None]).astype(i32), axis=1)
    p_lo = jnp.maximum(row_start[p_exp], p_tile * te) - p_tile * te
    p_hi = jnp.minimum(row_end[p_exp], (p_tile + 1) * te) - p_tile * te
    return (dest1, dest2, p_tile.astype(i32), p_exp.astype(i32), p_lo.astype(i32), p_hi.astype(i32),
            n_pairs.reshape(1).astype(i32))


def _row_copy(src, dst, sem):
    return pltpu.make_async_copy(src, dst, sem)


def _expert_body(ptile_ref, pexp_ref, plo_ref, phi_ref, np_ref, d1_ref, d2_ref,
                 x1_hbm, w1_ref, w3_ref, w2_ref, y2_hbm,
                 w1b, w3b, w2b, xbuf0, xbuf1, ybuf0, ybuf1, tok_s, inv_s, gsem, ssem, *, n, n_tiles):
    te = EXPERT_TILE
    p = pl.program_id(0)
    valid = p < np_ref[0]
    t = ptile_ref[p]
    pm1 = jnp.maximum(p - 1, 0)
    first = (p == 0) | (ptile_ref[pm1] != t)
    new_expert = (p == 0) | (pexp_ref[pm1] != pexp_ref[p])
    lo, hi = plo_ref[p], phi_ref[p]

    def gather(tile, dst, j, priority=0):
        _row_copy(x1_hbm.at[tok_s[tile * te + j]], dst.at[pl.ds(j, 1)], gsem).start(priority=priority)

    def scatter(tile, src, j, priority=0):
        _row_copy(src.at[pl.ds(j, 1)], y2_hbm.at[inv_s[tile * te + j]], ssem).start(priority=priority)

    def wait_gather(dst):
        _row_copy(x1_hbm.at[pl.ds(0, te), 0], dst, gsem).wait()

    def wait_scatter(src):
        _row_copy(src, y2_hbm.at[pl.ds(0, te), 0], ssem).wait()

    def issue_loop(fn, tile, buf):
        def one(j, c):
            fn(tile, buf, j)
            return c
        lax.fori_loop(0, te, one, 0, unroll=8)

    @pl.when(p == 0)
    def _():
        def invert(tk, c):
            a, b = d1_ref[tk], d2_ref[tk]
            tok_s[a] = tk
            inv_s[a] = tk
            tok_s[b] = tk
            inv_s[b] = tk + n
            return c
        lax.fori_loop(0, n, invert, 0, unroll=8)
        ybuf0[...] = jnp.zeros_like(ybuf0)
        ybuf1[...] = jnp.zeros_like(ybuf1)
        issue_loop(gather, 0, xbuf0)

    @pl.when(valid & new_expert)
    def _():
        w1b[...] = w1_ref[0].astype(BF16)
        w3b[...] = w3_ref[0].astype(BF16)
        w2b[...] = w2_ref[0].astype(BF16)

    def ffn(xb_ref, yb_ref):
        xb = xb_ref[...].astype(BF16)
        a = _dot(xb, w1b[...])
        b = _dot(xb, w3b[...])
        y = _dot((a * jax.nn.sigmoid(a) * b).astype(BF16), w2b[...])
        rows = lax.broadcasted_iota(jnp.int32, (te, 1), 0)
        yb_ref[...] = jnp.where((rows >= lo) & (rows < hi), y, yb_ref[...])

    for par, (xb, yb, xo, yo) in enumerate(((xbuf0, ybuf0, xbuf1, ybuf1), (xbuf1, ybuf1, xbuf0, ybuf0))):
        mine = valid & ((t & 1) == par)
        interior = first & (t >= 1) & (t < n_tiles - 1)

        @pl.when(mine & interior)
        def _():
            wait_gather(xb)

            @pl.when(t >= 2)
            def _():
                wait_scatter(yb)

            for j in range(te):
                gather(t + 1, xo, j, priority=j % 2)

            @pl.when(hi > 0)
            def _():
                for j in range(te):
                    scatter(t - 1, yo, j, priority=j % 2)
                ffn(xb, yb)

        @pl.when(mine & jnp.logical_not(interior))
        def _():
            @pl.when(first & (t == 0))
            def _():
                wait_gather(xb)
                if n_tiles > 1:
                    issue_loop(gather, 1, xo)

            @pl.when(first & (t == n_tiles - 1) & (t >= 1))
            def _():
                wait_gather(xb)

                @pl.when(t >= 2)
                def _():
                    wait_scatter(yb)

                issue_loop(scatter, t - 1, yo)

            ffn(xb, yb)

    @pl.when(p == pl.num_programs(0) - 1)
    def _():
        last = n_tiles - 1
        y_last, y_prev = (ybuf0, ybuf1) if last % 2 == 0 else (ybuf1, ybuf0)
        if n_tiles >= 2:
            wait_scatter(y_prev)
        issue_loop(scatter, last, y_last)
        wait_scatter(y_last)


def _experts(x1_rows, tables, w1, w3, w2):
    n, _, d = x1_rows.shape
    de = w1.shape[2]
    te = EXPERT_TILE
    n_tiles = 2 * n // te
    max_pairs = n_tiles + N_EXPERTS - 1
    wmap = lambda p, pt, pe, *_: (pe[p], 0, 0)
    return pl.pallas_call(
        functools.partial(_expert_body, n=n, n_tiles=n_tiles),
        grid_spec=pltpu.PrefetchScalarGridSpec(
            num_scalar_prefetch=7, grid=(max_pairs,),
            in_specs=[pl.BlockSpec(memory_space=pl.ANY),
                      pl.BlockSpec((1, d, de), wmap), pl.BlockSpec((1, d, de), wmap), pl.BlockSpec((1, de, d), wmap)],
            out_specs=pl.BlockSpec(memory_space=pl.ANY),
            scratch_shapes=[pltpu.VMEM((d, de), BF16), pltpu.VMEM((d, de), BF16), pltpu.VMEM((de, d), BF16),
                            pltpu.VMEM((te, d), F32), pltpu.VMEM((te, d), F32),
                            pltpu.VMEM((te, d), F32), pltpu.VMEM((te, d), F32),
                            pltpu.SMEM((2 * n,), jnp.int32), pltpu.SMEM((2 * n,), jnp.int32),
                            pltpu.SemaphoreType.DMA(()), pltpu.SemaphoreType.DMA(())]),
        out_shape=jax.ShapeDtypeStruct((2 * n, 1, d), F32),
        compiler_params=_params(("arbitrary",)), name="experts",
    )(*tables, x1_rows, w1, w3, w2)


def _combine_body(x_ref, ya_ref, yb_ref, cw_ref, g_ref, b_ref, o_ref):
    y = cw_ref[:, 0:1] * ya_ref[:, 0, :] + cw_ref[:, 1:2] * yb_ref[:, 0, :]
    o_ref[...] = _layer_norm(DN_ALPHA * x_ref[...] + y, g_ref[...], b_ref[...])


def _combine(y2, x1, cw_col, ln_g, ln_b):
    n, d = x1.shape
    tm = min(512, n)
    nt = n // tm
    tile = pl.BlockSpec((tm, d), lambda i: (i, 0))
    row = pl.BlockSpec((1, d), lambda i: (0, 0))
    return pl.pallas_call(
        _combine_body, grid=(nt,),
        in_specs=[tile, pl.BlockSpec((tm, 1, d), lambda i: (i, 0, 0)),
                  pl.BlockSpec((tm, 1, d), lambda i: (i + nt, 0, 0)),
                  pl.BlockSpec((tm, 2), lambda i: (i, 0)), row, row],
        out_specs=tile,
        out_shape=jax.ShapeDtypeStruct((n, d), F32),
        compiler_params=_params(("arbitrary",)), name="combine_ln2",
    )(x1, y2, y2, cw_col, ln_g, ln_b)


def _retention_consts():
    h, c = RET_HEADS, CHUNK
    log_gamma = jnp.log1p(-jnp.exp2(-5.0 - jnp.arange(h, dtype=F32)))
    idx = jnp.arange(c, dtype=F32)
    rel = idx[:, None] - idx[None, :]
    mask = jnp.where(rel >= 0, jnp.exp(log_gamma[:, None, None] * jnp.maximum(rel, 0.0)), 0.0)
    xi = jnp.exp(log_gamma[:, None] * (idx + 1.0))
    zeta = jnp.exp(log_gamma[:, None] * (c - 1.0 - idx))
    chunk_decay = jnp.exp(log_gamma * c)
    xi_b = jnp.broadcast_to(xi[:, :, None], (h, c, RET_QK_DIM))
    zeta_b = jnp.broadcast_to(zeta[:, :, None], (h, c, RET_QK_DIM))
    cd_b = jnp.broadcast_to(chunk_decay[:, None, None], (h, 1, RET_V_DIM))
    return mask, xi_b, zeta_b, cd_b


def _rope_consts():
    half = RET_QK_DIM // 2
    freq = ROPE_THETA ** (-jnp.arange(half, dtype=F32) / half)
    freq2 = jnp.concatenate([freq, freq])[None, :]
    sign2 = jnp.concatenate([-jnp.ones((half,), F32), jnp.ones((half,), F32)])[None, :]
    return freq2, sign2


def kernel(x, positions, w_in, b_gate, ret_gn_g, sgu_ln_g, sgu_ln_b, sgu_w, sgu_b, w_proj_ret, w_proj_sgu,
           w_out, ln1_g, ln1_b, w_group, b_group, w_er, b_er, w1, w3, w2, ln2_g, ln2_b):
    batch, seq, d = x.shape
    n = batch * seq
    qk_w, v_w = RET_HEADS * RET_QK_DIM, RET_HEADS * RET_V_DIM
    assert d == v_w == SGU_GROUPS * CHUNK and seq % CHUNK == 0
    ret_cols = 2 * qk_w + 2 * v_w
    max_pairs = (2 * n) // EXPERT_TILE + N_EXPERTS - 1

    freq2, sign2 = _rope_consts()
    pos_col = positions.reshape(n, 1)
    ret_consts = _retention_consts()

    xc = x.reshape(n, d)
    for l in range(w_in.shape[0]):
        wa = _regroup_retention_weights(w_in[l])
        y_a, xb = _retention_branch(xc, pos_col, freq2, sign2, wa, w_proj_ret[l].astype(BF16), ret_consts,
                                    ret_gn_g[l][None, :], batch, seq)
        h = _in_proj(xb, w_in[l], ret_cols)
        bias_b = jnp.broadcast_to(sgu_b[l][:, :, None], (SGU_GROUPS, CHUNK, CHUNK))
        merged = _sgu_merge(h, y_a, sgu_ln_g[l][None, :], sgu_ln_b[l][None, :], sgu_w[l], bias_b, b_gate[l],
                            w_proj_sgu[l].astype(BF16))

        w_r = jnp.concatenate([w_er[l], w_group[l]], axis=1).T
        w_r = jnp.pad(w_r, ((0, ROUTER_ROWS - w_r.shape[0]), (0, 0)))
        w_hi = w_r.astype(BF16)
        w_lo = (w_r - w_hi.astype(F32)).astype(BF16)
        bias = jnp.pad(jnp.concatenate([b_er[l], b_group[l]]), (0, ROUTER_ROWS - N_EXPERTS - N_GROUPS))[:, None]
        x1, x1_rows, ids, cw, cnt = _out_proj_router(merged, xc, w_out[l].astype(BF16), ln1_g[l][None, :],
                                                     ln1_b[l][None, :], w_hi, w_lo, bias)

        counts = cnt[:, 0].astype(jnp.int32)
        dest1, dest2, p_tile, p_exp, p_lo, p_hi, n_pairs = _routing_tables(ids, counts, n, max_pairs)
        y2 = _experts(x1_rows, (p_tile, p_exp, p_lo, p_hi, n_pairs, dest1, dest2), w1[l], w3[l], w2[l])
        xc = _combine(y2, x1, cw.T, ln2_g[l][None, :], ln2_b[l][None, :])
    return xc.reshape(batch, seq, d)
```

```python
import functools

import numpy as np
import jax
import jax.numpy as jnp
from jax import lax
from jax.experimental import pallas as pl
from jax.experimental.pallas import tpu as pltpu

F32 = jnp.float32
BF16 = jnp.bfloat16

RET_HEADS = 8
RET_QK_DIM = 128
RET_V_DIM = 256
CHUNK = 128
ROPE_THETA = 10000.0
SGU_GROUPS = 16
N_GROUPS = 4
EXPERTS_PER_GROUP = 8
N_EXPERTS = N_GROUPS * EXPERTS_PER_GROUP
ROUTER_ROWS = 40
LN_EPS = 1e-5
DEPTH = 1
DN_ALPHA = (2 * DEPTH) ** 0.25
SQRT_HALF = np.sqrt(0.5).astype(np.float32)

LANES = 128
VMEM_LIMIT = 52 * 1024 * 1024
EXPERT_TILE = 256


def _params(sem, vmem=VMEM_LIMIT):
    return pltpu.CompilerParams(dimension_semantics=sem, vmem_limit_bytes=vmem)


def _dot(a, b):
    return jnp.dot(a, b, preferred_element_type=F32)


def _dot_nt(a, b):
    return lax.dot_general(a, b, (((1,), (1,)), ((), ())), preferred_element_type=F32)


def _dot_tn(a, b):
    return lax.dot_general(a, b, (((0,), (0,)), ((), ())), preferred_element_type=F32)


def _layer_norm(r, g, b):
    mu = jnp.mean(r, axis=-1, keepdims=True)
    d = r - mu
    var = jnp.mean(d * d, axis=-1, keepdims=True)
    return d * lax.rsqrt(var + LN_EPS) * g + b


def _gelu(x):
    return 0.5 * x * (1.0 + lax.erf(x * SQRT_HALF))


def _inproj_body(x_ref, w_ref, b_ref, o_ref, wb_ref, *, gelu_blocks):
    @pl.when(pl.program_id(1) == 0)
    def _():
        wb_ref[...] = w_ref[...].astype(BF16)

    @pl.when(pl.program_id(0) < gelu_blocks)
    def _():
        o_ref[...] = _gelu(_dot(x_ref[...], wb_ref[...])).astype(BF16)

    @pl.when(pl.program_id(0) >= gelu_blocks)
    def _():
        o_ref[...] = jax.nn.sigmoid(_dot(x_ref[...], wb_ref[...]) + b_ref[...]).astype(BF16)


def _in_proj(xb, w, col0, gate_bias):
    n, d = xb.shape
    width = w.shape[1] - col0
    tm, tn = min(1024, n), 1024
    assert col0 % tn == 0 and width % (2 * tn) == 0
    return pl.pallas_call(
        functools.partial(_inproj_body, gelu_blocks=width // (2 * tn)),
        grid=(width // tn, n // tm),
        in_specs=[pl.BlockSpec((tm, d), lambda j, i: (i, 0)),
                  pl.BlockSpec((d, tn), lambda j, i: (0, j + col0 // tn)),
                  pl.BlockSpec((1, tn), lambda j, i: (0, j))],
        out_specs=pl.BlockSpec((tm, tn), lambda j, i: (i, j)),
        out_shape=jax.ShapeDtypeStruct((n, width), BF16),
        scratch_shapes=[pltpu.VMEM((d, tn), BF16)],
        compiler_params=_params(("arbitrary", "arbitrary")), name="in_proj",
    )(xb, w, gate_bias)


HEAD_COLS = 2 * RET_QK_DIM + 2 * RET_V_DIM


def _regroup_body(q_ref, k_ref, v_ref, g_ref, o_ref):
    dk, dv = RET_QK_DIM, RET_V_DIM
    o_ref[0, :, 0:dk] = q_ref[...].astype(BF16)
    o_ref[0, :, dk:2 * dk] = k_ref[...].astype(BF16)
    o_ref[0, :, 2 * dk:2 * dk + dv] = v_ref[...].astype(BF16)
    o_ref[0, :, 2 * dk + dv:] = g_ref[...].astype(BF16)


def _regroup_retention_weights(w):
    d = w.shape[0]
    k_blk = RET_HEADS
    v_blk = 2 * RET_HEADS * RET_QK_DIM // RET_V_DIM
    return pl.pallas_call(
        _regroup_body, grid=(RET_HEADS,),
        in_specs=[pl.BlockSpec((d, RET_QK_DIM), lambda h: (0, h)),
                  pl.BlockSpec((d, RET_QK_DIM), lambda h: (0, k_blk + h)),
                  pl.BlockSpec((d, RET_V_DIM), lambda h: (0, v_blk + h)),
                  pl.BlockSpec((d, RET_V_DIM), lambda h: (0, v_blk + RET_HEADS + h))],
        out_specs=pl.BlockSpec((1, d, HEAD_COLS), lambda h: (h, 0, 0)),
        out_shape=jax.ShapeDtypeStruct((RET_HEADS, d, HEAD_COLS), BF16),
        compiler_params=_params(("arbitrary",)), name="regroup_retention_weights",
    )(w, w, w, w)


def _retention_body(x_ref, pos_ref, freq_ref, sign_ref, wa_ref, wp_ref, mask_ref, xi_ref, zeta_ref, cd_ref,
                    gain_ref, o_ref, xb_ref, state_ref, yg_ref, cos_ref, sin_ref, *, n_chunks):
    @pl.when(pl.program_id(1) == 0)
    def _():
        state_ref[...] = jnp.zeros_like(state_ref)

    ang = pos_ref[...].astype(F32) * freq_ref[...]
    cos_ref[...] = jnp.cos(ang)
    sin_ref[...] = jnp.sin(ang) * sign_ref[...]

    scale = RET_QK_DIM ** -0.5
    half = RET_QK_DIM // 2
    dk, dv = RET_QK_DIM, RET_V_DIM
    xb = x_ref[...].astype(BF16)
    xb_ref[...] = xb
    for hd in range(RET_HEADS):
        proj = _dot(xb, wa_ref[hd])
        gain = gain_ref[:, hd * dv:(hd + 1) * dv]
        for c in range(n_chunks):
            lo, hi = c * CHUNK, (c + 1) * CHUNK
            cos, sin = cos_ref[lo:hi, :], sin_ref[lo:hi, :]
            q, k = proj[lo:hi, 0:dk], proj[lo:hi, dk:2 * dk]
            vb = proj[lo:hi, 2 * dk:2 * dk + dv].astype(BF16)
            g = proj[lo:hi, 2 * dk + dv:]
            qr = q * cos + pltpu.roll(q, half, 1) * sin
            kr = (k * cos + pltpu.roll(k, half, 1) * sin) * scale
            scores = _dot_nt(qr.astype(BF16), kr.astype(BF16)) * mask_ref[hd]
            state = state_ref[hd]
            lhs = jnp.concatenate([scores.astype(BF16), (qr * xi_ref[hd]).astype(BF16)], axis=1)
            rhs = jnp.concatenate([vb, state.astype(BF16)], axis=0)
            out = _dot(lhs, rhs)
            state_ref[hd] = state * cd_ref[hd] + _dot_tn((kr * zeta_ref[hd]).astype(BF16), vb)
            mu = jnp.mean(out, axis=-1, keepdims=True)
            d = out - mu
            var = jnp.mean(d * d, axis=-1, keepdims=True)
            yn = d * lax.rsqrt(var + LN_EPS) * gain
            yg_ref[lo:hi, hd * dv:(hd + 1) * dv] = (g * jax.nn.sigmoid(g) * yn).astype(BF16)
    o_ref[...] = _dot(yg_ref[...], wp_ref[...]).astype(BF16)


def _retention_branch(x2, pos_col, freq2, sign2, wa_bf, wp_bf, consts, gain, batch, seq):
    n, d = x2.shape
    tr = min(256, seq)
    nr = seq // tr
    mask, xi_b, zeta_b, cd_b = consts
    vw = RET_HEADS * RET_V_DIM

    def once(shape):
        return pl.BlockSpec(shape, lambda b, r: (0,) * len(shape), pipeline_mode=pl.Buffered(1))

    rows = lambda w: pl.BlockSpec((tr, w), lambda b, r: (b * nr + r, 0))
    return pl.pallas_call(
        functools.partial(_retention_body, n_chunks=tr // CHUNK),
        grid=(batch, nr),
        in_specs=[rows(d), rows(1), once((1, LANES)), once((1, LANES)),
                  once((RET_HEADS, d, HEAD_COLS)), once((vw, d)),
                  once((RET_HEADS, CHUNK, CHUNK)), once((RET_HEADS, CHUNK, RET_QK_DIM)),
                  once((RET_HEADS, CHUNK, RET_QK_DIM)), once((RET_HEADS, 1, RET_V_DIM)), once((1, vw))],
        out_specs=[rows(d), rows(d)],
        out_shape=[jax.ShapeDtypeStruct((n, d), BF16), jax.ShapeDtypeStruct((n, d), BF16)],
        scratch_shapes=[pltpu.VMEM((RET_HEADS, RET_QK_DIM, RET_V_DIM), F32), pltpu.VMEM((tr, vw), BF16),
                        pltpu.VMEM((tr, LANES), F32), pltpu.VMEM((tr, LANES), F32)],
        compiler_params=_params(("arbitrary", "arbitrary")), name="retention_branch",
    )(x2, pos_col, freq2, sign2, wa_bf, wp_bf, mask, xi_b, zeta_b, cd_b, gain)


def _sgu_merge_body(u_ref, v_ref, sa_ref, sb_ref, ya_ref, lng_ref, lnb_ref, ws_ref, bs_ref, wp_ref,
                    o_ref, wtril_ref, ys_ref, *, n_chunks):
    @pl.when(pl.program_id(0) == 0)
    def _():
        r = lax.broadcasted_iota(jnp.int32, (CHUNK, CHUNK), 0)
        c = lax.broadcasted_iota(jnp.int32, (CHUNK, CHUNK), 1)
        for g in range(SGU_GROUPS):
            wtril_ref[g] = jnp.where(r >= c, ws_ref[g], 0.0).astype(BF16)

    vn = _layer_norm(v_ref[...].astype(F32), lng_ref[...], lnb_ref[...]).astype(BF16)
    for g in range(SGU_GROUPS):
        cols = pl.ds(g * CHUNK, CHUNK)
        lo = g * CHUNK
        rhs = jnp.concatenate([vn[c * CHUNK:(c + 1) * CHUNK, lo:lo + CHUNK] for c in range(n_chunks)], axis=1)
        mix = _dot(wtril_ref[g], rhs)
        bias = bs_ref[g]
        for c in range(n_chunks):
            rows = pl.ds(c * CHUNK, CHUNK)
            u = u_ref[rows, cols].astype(F32)
            ys_ref[rows, cols] = (u * (mix[:, c * CHUNK:(c + 1) * CHUNK] + bias)).astype(BF16)

    yb = _dot(ys_ref[...], wp_ref[...])
    o_ref[...] = (sa_ref[...].astype(F32) * ya_ref[...].astype(F32) + sb_ref[...].astype(F32) * yb).astype(BF16)


def _sgu_merge(h, y_a, ln_g, ln_b, w_s, b_s_b, wp_bf):
    n, d = y_a.shape
    ts = min(256, n)
    tile = lambda blk: pl.BlockSpec((ts, d), lambda i: (i, blk))
    row = pl.BlockSpec((1, d), lambda i: (0, 0))
    full3 = pl.BlockSpec((SGU_GROUPS, CHUNK, CHUNK), lambda i: (0, 0, 0))
    return pl.pallas_call(
        functools.partial(_sgu_merge_body, n_chunks=ts // CHUNK),
        grid=(n // ts,),
        in_specs=[tile(0), tile(1), tile(2), tile(3), tile(0), row, row, full3, full3,
                  pl.BlockSpec((d, d), lambda i: (0, 0), pipeline_mode=pl.Buffered(1))],
        out_specs=tile(0),
        out_shape=jax.ShapeDtypeStruct((n, d), BF16),
        scratch_shapes=[pltpu.VMEM((SGU_GROUPS, CHUNK, CHUNK), BF16), pltpu.VMEM((ts, d), BF16)],
        compiler_params=_params(("arbitrary",)), name="sgu_merge",
    )(h, h, h, h, y_a, ln_g, ln_b, w_s, b_s_b, wp_bf)


def _outproj_router_body(m_ref, x_ref, w_ref, g_ref, b_ref, whi_ref, wlo_ref, rb_ref,
                         o_ref, orow_ref, ids_ref, cw_ref, cnt_ref, carry_ref, tri_ref, *, tm):
    @pl.when(pl.program_id(0) == 0)
    def _():
        carry_ref[...] = jnp.zeros_like(carry_ref)
        r = lax.broadcasted_iota(jnp.int32, (tm, tm), 0)
        c = lax.broadcasted_iota(jnp.int32, (tm, tm), 1)
        tri_ref[...] = jnp.where(r < c, 1.0, 0.0).astype(BF16)

    r = DN_ALPHA * x_ref[...] + _dot(m_ref[...], w_ref[...])
    x = _layer_norm(r, g_ref[...], b_ref[...])
    o_ref[...] = x
    orow_ref[:, 0, :] = x

    hi = x.astype(BF16)
    lo = (x - hi.astype(F32)).astype(BF16)
    whi = whi_ref[...]
    logits = _dot_nt(whi, hi) + _dot_nt(whi, lo) + _dot_nt(wlo_ref[...], hi) + rb_ref[...]
    el = logits[0:N_EXPERTS, :]
    gl = logits[N_EXPERTS:N_EXPERTS + N_GROUPS, :]

    gi = lax.broadcasted_iota(jnp.int32, gl.shape, 0)
    gmax = jnp.max(gl, axis=0, keepdims=True)
    gidx = jnp.min(jnp.where(gl == gmax, gi, N_GROUPS), axis=0, keepdims=True)
    p_group = 1.0 / jnp.sum(jnp.exp(gl - gmax), axis=0, keepdims=True)

    ei = lax.broadcasted_iota(jnp.int32, el.shape, 0)
    first = gidx * EXPERTS_PER_GROUP
    in_group = (ei >= first) & (ei < first + EXPERTS_PER_GROUP)
    m1 = jnp.where(in_group, el, -jnp.inf)
    v1 = jnp.max(m1, axis=0, keepdims=True)
    i1 = jnp.min(jnp.where(in_group & (m1 == v1), ei, N_EXPERTS), axis=0, keepdims=True)
    rest = in_group & (ei != i1)
    m2 = jnp.where(rest, el, -jnp.inf)
    v2 = jnp.max(m2, axis=0, keepdims=True)
    i2 = jnp.min(jnp.where(rest & (m2 == v2), ei, N_EXPERTS), axis=0, keepdims=True)
    t = jnp.exp(v2 - v1)
    p1 = 1.0 / (1.0 + t)
    p2 = t * p1

    sel1, sel2 = ei == i1, ei == i2
    onehot = jnp.where(sel1 | sel2, 1.0, 0.0)
    before = _dot(onehot.astype(BF16), tri_ref[...]) + carry_ref[:, 0:1]
    r1 = jnp.sum(jnp.where(sel1, before, 0.0), axis=0, keepdims=True)
    r2 = jnp.sum(jnp.where(sel2, before, 0.0), axis=0, keepdims=True)
    carry_ref[...] = carry_ref[...] + jnp.sum(onehot, axis=1, keepdims=True)

    ids_ref[0:1, :] = i1
    ids_ref[1:2, :] = i2
    ids_ref[2:3, :] = r1.astype(jnp.int32)
    ids_ref[3:4, :] = r2.astype(jnp.int32)
    cw_ref[0:1, :] = p_group * p1
    cw_ref[1:2, :] = p_group * p2
    cnt_ref[...] = carry_ref[...]


def _out_proj_router(merged, x2, wo_bf, ln_g, ln_b, w_hi, w_lo, bias_col):
    n, d = x2.shape
    tm = min(512, n)
    tile = pl.BlockSpec((tm, d), lambda i: (i, 0))
    row = pl.BlockSpec((1, d), lambda i: (0, 0))
    full = pl.BlockSpec((ROUTER_ROWS, d), lambda i: (0, 0))
    return pl.pallas_call(
        functools.partial(_outproj_router_body, tm=tm), grid=(n // tm,),
        in_specs=[tile, tile, pl.BlockSpec((d, d), lambda i: (0, 0), pipeline_mode=pl.Buffered(1)), row, row,
                  full, full, pl.BlockSpec((ROUTER_ROWS, 1), lambda i: (0, 0))],
        out_specs=[tile, pl.BlockSpec((tm, 1, d), lambda i: (i, 0, 0)),
                   pl.BlockSpec((4, tm), lambda i: (0, i)),
                   pl.BlockSpec((2, tm), lambda i: (0, i)),
                   pl.BlockSpec((N_EXPERTS, LANES), lambda i: (0, 0))],
        out_shape=[jax.ShapeDtypeStruct((n, d), F32), jax.ShapeDtypeStruct((n, 1, d), F32),
                   jax.ShapeDtypeStruct((4, n), jnp.int32),
                   jax.ShapeDtypeStruct((2, n), F32),
                   jax.ShapeDtypeStruct((N_EXPERTS, LANES), F32)],
        scratch_shapes=[pltpu.VMEM((N_EXPERTS, LANES), F32), pltpu.VMEM((tm, tm), BF16)],
        compiler_params=_params(("arbitrary",)), name="out_proj_ln1_router",
    )(merged, x2, wo_bf, ln_g, ln_b, w_hi, w_lo, bias_col)


def _routing_tables(ids, counts, n, max_pairs):
    te = EXPERT_TILE
    n_tiles = 2 * n // te
    i32 = jnp.int32
    row_end = jnp.cumsum(counts)
    row_start = row_end - counts
    dest1 = row_start[ids[0]] + ids[2]
    dest2 = row_start[ids[1]] + ids[3]
    ne_cum = jnp.cumsum((counts > 0).astype(i32))
    tile_lo = jnp.arange(n_tiles, dtype=i32) * te
    first_e = jnp.sum((row_end[None, :] <= tile_lo[:, None]).astype(i32), axis=1)
    last_e = jnp.sum((row_end[None, :] <= (tile_lo + te - 1)[:, None]).astype(i32), axis=1)
    pairs_t = ne_cum[last_e] - ne_cum[first_e] + 1
    pair_end = jnp.cumsum(pairs_t)
    pair_start = pair_end - pairs_t
    n_pairs = pair_end[-1]
    p = jnp.minimum(jnp.arange(max_pairs, dtype=i32), n_pairs - 1)
    p_tile = jnp.sum((pair_end[None, :] <= p[:, None]).astype(i32), axis=1)
    rank = ne_cum[first_e[p_tile]] - 1 + (p - pair_start[p_tile])
    p_exp = jnp.sum((ne_cum[None, :] <= rank[:, None]).astype(i32), axis=1)
    p_lo = jnp.maximum(row_start[p_exp], p_tile * te) - p_tile * te
    p_hi = jnp.minimum(row_end[p_exp], (p_tile + 1) * te) - p_tile * te
    return (dest1, dest2, p_tile.astype(i32), p_exp.astype(i32), p_lo.astype(i32), p_hi.astype(i32),
            n_pairs.reshape(1).astype(i32))


def _row_copy(src, dst, sem):
    return pltpu.make_async_copy(src, dst, sem)


def _expert_body(ptile_ref, pexp_ref, plo_ref, phi_ref, np_ref, d1_ref, d2_ref,
                 x1_hbm, w1_ref, w3_ref, w2_ref, y2_hbm,
                 w1b, w3b, w2b, xbuf0, xbuf1, ybuf0, ybuf1, tok_s, inv_s, gsem, ssem, *, n, n_tiles):
    te = EXPERT_TILE
    p = pl.program_id(0)
    valid = p < np_ref[0]
    t = ptile_ref[p]
    pm1 = jnp.maximum(p - 1, 0)
    first = (p == 0) | (ptile_ref[pm1] != t)
    new_expert = (p == 0) | (pexp_ref[pm1] != pexp_ref[p])
    lo, hi = plo_ref[p], phi_ref[p]

    def gather(tile, dst, j, priority=0):
        _row_copy(x1_hbm.at[tok_s[tile * te + j]], dst.at[pl.ds(j, 1)], gsem).start(priority=priority)

    def scatter(tile, src, j, priority=0):
        _row_copy(src.at[pl.ds(j, 1)], y2_hbm.at[inv_s[tile * te + j]], ssem).start(priority=priority)

    def wait_gather(dst):
        _row_copy(x1_hbm.at[pl.ds(0, te), 0], dst, gsem).wait()

    def wait_scatter(src):
        _row_copy(src, y2_hbm.at[pl.ds(0, te), 0], ssem).wait()

    def issue_loop(fn, tile, buf):
        def one(j, c):
            fn(tile, buf, j)
            return c
        lax.fori_loop(0, te, one, 0, unroll=8)

    @pl.when(p == 0)
    def _():
        def invert(tk, c):
            a, b = d1_ref[tk], d2_ref[tk]
            tok_s[a] = tk
            inv_s[a] = tk
            tok_s[b] = tk
            inv_s[b] = tk + n
            return c
        lax.fori_loop(0, n, invert, 0, unroll=8)
        ybuf0[...] = jnp.zeros_like(ybuf0)
        ybuf1[...] = jnp.zeros_like(ybuf1)
        issue_loop(gather, 0, xbuf0)

    @pl.when(valid & new_expert)
    def _():
        w1b[...] = w1_ref[0].astype(BF16)
        w3b[...] = w3_ref[0].astype(BF16)
        w2b[...] = w2_ref[0].astype(BF16)

    def ffn(xb_ref, yb_ref):
        xb = xb_ref[...].astype(BF16)
        a = _dot(xb, w1b[...])
        b = _dot(xb, w3b[...])
        y = _dot((a * jax.nn.sigmoid(a) * b).astype(BF16), w2b[...])
        rows = lax.broadcasted_iota(jnp.int32, (te, 1), 0)
        yb_ref[...] = jnp.where((rows >= lo) & (rows < hi), y, yb_ref[...])

    for par, (xb, yb, xo, yo) in enumerate(((xbuf0, ybuf0, xbuf1, ybuf1), (xbuf1, ybuf1, xbuf0, ybuf0))):
        mine = valid & ((t & 1) == par)
        interior = first & (t >= 1) & (t < n_tiles - 1)

        @pl.when(mine & interior)
        def _():
            wait_gather(xb)

            @pl.when(t >= 2)
            def _():
                wait_scatter(yb)

            for j in range(te):
                gather(t + 1, xo, j, priority=j % 2)

            @pl.when(hi > 0)
            def _():
                for j in range(te):
                    scatter(t - 1, yo, j, priority=j % 2)
                ffn(xb, yb)

        @pl.when(mine & jnp.logical_not(interior))
        def _():
            @pl.when(first & (t == 0))
            def _():
                wait_gather(xb)
                if n_tiles > 1:
                    issue_loop(gather, 1, xo)

            @pl.when(first & (t == n_tiles - 1) & (t >= 1))
            def _():
                wait_gather(xb)

                @pl.when(t >= 2)
                def _():
                    wait_scatter(yb)

                issue_loop(scatter, t - 1, yo)

            ffn(xb, yb)

    @pl.when(p == pl.num_programs(0) - 1)
    def _():
        last = n_tiles - 1
        y_last, y_prev = (ybuf0, ybuf1) if last % 2 == 0 else (ybuf1, ybuf0)
        if n_tiles >= 2:
            wait_scatter(y_prev)
        issue_loop(scatter, last, y_last)
        wait_scatter(y_last)


def _experts(x1_rows, tables, w1, w3, w2):
    n, _, d = x1_rows.shape
    de = w1.shape[2]
    te = EXPERT_TILE
    n_tiles = 2 * n // te
    max_pairs = n_tiles + N_EXPERTS - 1
    wmap = lambda p, pt, pe, *_: (pe[p], 0, 0)
    return pl.pallas_call(
        functools.partial(_expert_body, n=n, n_tiles=n_tiles),
        grid_spec=pltpu.PrefetchScalarGridSpec(
            num_scalar_prefetch=7, grid=(max_pairs,),
            in_specs=[pl.BlockSpec(memory_space=pl.ANY),
                      pl.BlockSpec((1, d, de), wmap), pl.BlockSpec((1, d, de), wmap), pl.BlockSpec((1, de, d), wmap)],
            out_specs=pl.BlockSpec(memory_space=pl.ANY),
            scratch_shapes=[pltpu.VMEM((d, de), BF16), pltpu.VMEM((d, de), BF16), pltpu.VMEM((de, d), BF16),
                            pltpu.VMEM((te, d), F32), pltpu.VMEM((te, d), F32),
                            pltpu.VMEM((te, d), F32), pltpu.VMEM((te, d), F32),
                            pltpu.SMEM((2 * n,), jnp.int32), pltpu.SMEM((2 * n,), jnp.int32),
                            pltpu.SemaphoreType.DMA(()), pltpu.SemaphoreType.DMA(())]),
        out_shape=jax.ShapeDtypeStruct((2 * n, 1, d), F32),
        compiler_params=_params(("arbitrary",)), name="experts",
    )(*tables, x1_rows, w1, w3, w2)


def _combine_body(x_ref, ya_ref, yb_ref, cw_ref, g_ref, b_ref, o_ref):
    y = cw_ref[:, 0:1] * ya_ref[:, 0, :] + cw_ref[:, 1:2] * yb_ref[:, 0, :]
    o_ref[...] = _layer_norm(DN_ALPHA * x_ref[...] + y, g_ref[...], b_ref[...])


def _combine(y2, x1, cw_col, ln_g, ln_b):
    n, d = x1.shape
    tm = min(512, n)
    nt = n // tm
    tile = pl.BlockSpec((tm, d), lambda i: (i, 0))
    row = pl.BlockSpec((1, d), lambda i: (0, 0))
    return pl.pallas_call(
        _combine_body, grid=(nt,),
        in_specs=[tile, pl.BlockSpec((tm, 1, d), lambda i: (i, 0, 0)),
                  pl.BlockSpec((tm, 1, d), lambda i: (i + nt, 0, 0)),
                  pl.BlockSpec((tm, 2), lambda i: (i, 0)), row, row],
        out_specs=tile,
        out_shape=jax.ShapeDtypeStruct((n, d), F32),
        compiler_params=_params(("arbitrary",)), name="combine_ln2",
    )(x1, y2, y2, cw_col, ln_g, ln_b)


def _retention_consts():
    h, c = RET_HEADS, CHUNK
    log_gamma = jnp.log1p(-jnp.exp2(-5.0 - jnp.arange(h, dtype=F32)))
    idx = jnp.arange(c, dtype=F32)
    rel = idx[:, None] - idx[None, :]
    mask = jnp.where(rel >= 0, jnp.exp(log_gamma[:, None, None] * jnp.maximum(rel, 0.0)), 0.0)
    xi = jnp.exp(log_gamma[:, None] * (idx + 1.0))
    zeta = jnp.exp(log_gamma[:, None] * (c - 1.0 - idx))
    chunk_decay = jnp.exp(log_gamma * c)
    xi_b = jnp.broadcast_to(xi[:, :, None], (h, c, RET_QK_DIM))
    zeta_b = jnp.broadcast_to(zeta[:, :, None], (h, c, RET_QK_DIM))
    cd_b = jnp.broadcast_to(chunk_decay[:, None, None], (h, 1, RET_V_DIM))
    return mask, xi_b, zeta_b, cd_b


def _rope_consts():
    half = RET_QK_DIM // 2
    freq = ROPE_THETA ** (-jnp.arange(half, dtype=F32) / half)
    freq2 = jnp.concatenate([freq, freq])[None, :]
    sign2 = jnp.concatenate([-jnp.ones((half,), F32), jnp.ones((half,), F32)])[None, :]
    return freq2, sign2


def kernel(x, positions, w_in, b_gate, ret_gn_g, sgu_ln_g, sgu_ln_b, sgu_w, sgu_b, w_proj_ret, w_proj_sgu,
           w_out, ln1_g, ln1_b, w_group, b_group, w_er, b_er, w1, w3, w2, ln2_g, ln2_b):
    batch, seq, d = x.shape
    n = batch * seq
    qk_w, v_w = RET_HEADS * RET_QK_DIM, RET_HEADS * RET_V_DIM
    assert d == v_w == SGU_GROUPS * CHUNK and seq % CHUNK == 0
    ret_cols = 2 * qk_w + 2 * v_w
    max_pairs = (2 * n) // EXPERT_TILE + N_EXPERTS - 1

    freq2, sign2 = _rope_consts()
    pos_col = positions.reshape(n, 1)
    ret_consts = _retention_consts()

    xc = x.reshape(n, d)
    for l in range(w_in.shape[0]):
        wa = _regroup_retention_weights(w_in[l])
        y_a, xb = _retention_branch(xc, pos_col, freq2, sign2, wa, w_proj_ret[l].astype(BF16), ret_consts,
                                    ret_gn_g[l][None, :], batch, seq)
        gate_bias = jnp.concatenate([jnp.zeros((2 * d,), F32), b_gate[l].reshape(-1)])[None, :]
        h = _in_proj(xb, w_in[l], ret_cols, gate_bias)
        bias_b = jnp.broadcast_to(sgu_b[l][:, :, None], (SGU_GROUPS, CHUNK, CHUNK))
        merged = _sgu_merge(h, y_a, sgu_ln_g[l][None, :], sgu_ln_b[l][None, :], sgu_w[l], bias_b,
                            w_proj_sgu[l].astype(BF16))

        w_r = jnp.concatenate([w_er[l], w_group[l]], axis=1).T
        w_r = jnp.pad(w_r, ((0, ROUTER_ROWS - w_r.shape[0]), (0, 0)))
        w_hi = w_r.astype(BF16)
        w_lo = (w_r - w_hi.astype(F32)).astype(BF16)
        bias = jnp.pad(jnp.concatenate([b_er[l], b_group[l]]), (0, ROUTER_ROWS - N_EXPERTS - N_GROUPS))[:, None]
        x1, x1_rows, ids, cw, cnt = _out_proj_router(merged, xc, w_out[l].astype(BF16), ln1_g[l][None, :],
                                                     ln1_b[l][None, :], w_hi, w_lo, bias)

        counts = cnt[:, 0].astype(jnp.int32)
        dest1, dest2, p_tile, p_exp, p_lo, p_hi, n_pairs = _routing_tables(ids, counts, n, max_pairs)
        y2 = _experts(x1_rows, (p_tile, p_exp, p_lo, p_hi, n_pairs, dest1, dest2), w1[l], w3[l], w2[l])
        xc = _combine(y2, x1, cw.T, ln2_g[l][None, :], ln2_b[l][None, :])
    return xc.reshape(batch, seq, d)
```

```python
import functools

import numpy as np
import jax
import jax.numpy as jnp
from jax import lax
from jax.experimental import pallas as pl
from jax.experimental.pallas import tpu as pltpu

F32 = jnp.float32
BF16 = jnp.bfloat16

RET_HEADS = 8
RET_QK_DIM = 128
RET_V_DIM = 256
CHUNK = 128
ROPE_THETA = 10000.0
SGU_GROUPS = 16
N_GROUPS = 4
EXPERTS_PER_GROUP = 8
N_EXPERTS = N_GROUPS * EXPERTS_PER_GROUP
ROUTER_ROWS = 40
LN_EPS = 1e-5
DEPTH = 1
DN_ALPHA = (2 * DEPTH) ** 0.25
SQRT_HALF = np.sqrt(0.5).astype(np.float32)

LANES = 128
VMEM_LIMIT = 52 * 1024 * 1024
EXPERT_TILE = 256


def _params(sem, vmem=VMEM_LIMIT, flags=None):
    return pltpu.CompilerParams(dimension_semantics=sem, vmem_limit_bytes=vmem, flags=flags)


def _dot(a, b):
    return jnp.dot(a, b, preferred_element_type=F32)


def _dot_nt(a, b):
    return lax.dot_general(a, b, (((1,), (1,)), ((), ())), preferred_element_type=F32)


def _dot_tn(a, b):
    return lax.dot_general(a, b, (((0,), (0,)), ((), ())), preferred_element_type=F32)


def _layer_norm(r, g, b):
    mu = jnp.mean(r, axis=-1, keepdims=True)
    d = r - mu
    var = jnp.mean(d * d, axis=-1, keepdims=True)
    return d * lax.rsqrt(var + LN_EPS) * g + b


def _gelu(x):
    return 0.5 * x * (1.0 + lax.erf(x * SQRT_HALF))


def _inproj_body(x_ref, w_ref, b_ref, o_ref, wb_ref, *, gelu_blocks):
    @pl.when(pl.program_id(1) == 0)
    def _():
        wb_ref[...] = w_ref[...].astype(BF16)

    @pl.when(pl.program_id(0) < gelu_blocks)
    def _():
        o_ref[...] = _gelu(_dot(x_ref[...], wb_ref[...])).astype(BF16)

    @pl.when(pl.program_id(0) >= gelu_blocks)
    def _():
        o_ref[...] = jax.nn.sigmoid(_dot(x_ref[...], wb_ref[...]) + b_ref[...]).astype(BF16)


def _in_proj(xb, w, col0, gate_bias):
    n, d = xb.shape
    width = w.shape[1] - col0
    tm, tn = min(1024, n), 1024
    assert col0 % tn == 0 and width % (2 * tn) == 0
    return pl.pallas_call(
        functools.partial(_inproj_body, gelu_blocks=width // (2 * tn)),
        grid=(width // tn, n // tm),
        in_specs=[pl.BlockSpec((tm, d), lambda j, i: (i, 0)),
                  pl.BlockSpec((d, tn), lambda j, i: (0, j + col0 // tn)),
                  pl.BlockSpec((1, tn), lambda j, i: (0, j))],
        out_specs=pl.BlockSpec((tm, tn), lambda j, i: (i, j)),
        out_shape=jax.ShapeDtypeStruct((n, width), BF16),
        scratch_shapes=[pltpu.VMEM((d, tn), BF16)],
        compiler_params=_params(("arbitrary", "arbitrary")), name="in_proj",
    )(xb, w, gate_bias)


HEAD_COLS = 2 * RET_QK_DIM + 2 * RET_V_DIM


def _regroup_body(q_ref, k_ref, v_ref, g_ref, o_ref):
    dk, dv = RET_QK_DIM, RET_V_DIM
    o_ref[0, :, 0:dk] = q_ref[...].astype(BF16)
    o_ref[0, :, dk:2 * dk] = k_ref[...].astype(BF16)
    o_ref[0, :, 2 * dk:2 * dk + dv] = v_ref[...].astype(BF16)
    o_ref[0, :, 2 * dk + dv:] = g_ref[...].astype(BF16)


def _regroup_retention_weights(w):
    d = w.shape[0]
    k_blk = RET_HEADS
    v_blk = 2 * RET_HEADS * RET_QK_DIM // RET_V_DIM
    return pl.pallas_call(
        _regroup_body, grid=(RET_HEADS,),
        in_specs=[pl.BlockSpec((d, RET_QK_DIM), lambda h: (0, h)),
                  pl.BlockSpec((d, RET_QK_DIM), lambda h: (0, k_blk + h)),
                  pl.BlockSpec((d, RET_V_DIM), lambda h: (0, v_blk + h)),
                  pl.BlockSpec((d, RET_V_DIM), lambda h: (0, v_blk + RET_HEADS + h))],
        out_specs=pl.BlockSpec((1, d, HEAD_COLS), lambda h: (h, 0, 0)),
        out_shape=jax.ShapeDtypeStruct((RET_HEADS, d, HEAD_COLS), BF16),
        compiler_params=_params(("arbitrary",)), name="regroup_retention_weights",
    )(w, w, w, w)


def _retention_body(x_ref, pos_ref, freq_ref, sign_ref, wa_ref, wp_ref, mask_ref, xi_ref, zeta_ref, cd_ref,
                    gain_ref, o_ref, xb_ref, state_ref, yg_ref, cos_ref, sin_ref, *, n_chunks):
    @pl.when(pl.program_id(1) == 0)
    def _():
        state_ref[...] = jnp.zeros_like(state_ref)

    ang = pos_ref[...].astype(F32) * freq_ref[...]
    cos_ref[...] = jnp.cos(ang)
    sin_ref[...] = jnp.sin(ang) * sign_ref[...]

    scale = RET_QK_DIM ** -0.5
    half = RET_QK_DIM // 2
    dk, dv = RET_QK_DIM, RET_V_DIM
    xb = x_ref[...].astype(BF16)
    xb_ref[...] = xb
    ahead = 1
    projs = [_dot(xb, wa_ref[h]) for h in range(ahead)]
    for hd in range(RET_HEADS):
        proj = projs[hd]
        if hd + ahead < RET_HEADS:
            projs.append(_dot(xb, wa_ref[hd + ahead]))
        gain = gain_ref[:, hd * dv:(hd + 1) * dv]
        for c in range(n_chunks):
            lo, hi = c * CHUNK, (c + 1) * CHUNK
            cos, sin = cos_ref[lo:hi, :], sin_ref[lo:hi, :]
            q, k = proj[lo:hi, 0:dk], proj[lo:hi, dk:2 * dk]
            vb = proj[lo:hi, 2 * dk:2 * dk + dv].astype(BF16)
            g = proj[lo:hi, 2 * dk + dv:]
            qr = q * cos + pltpu.roll(q, half, 1) * sin
            kr = (k * cos + pltpu.roll(k, half, 1) * sin) * scale
            scores = _dot_nt(qr.astype(BF16), kr.astype(BF16)) * mask_ref[hd]
            state = state_ref[hd]
            lhs = jnp.concatenate([scores.astype(BF16), (qr * xi_ref[hd]).astype(BF16)], axis=1)
            rhs = jnp.concatenate([vb, state.astype(BF16)], axis=0)
            out = _dot(lhs, rhs)
            state_ref[hd] = state * cd_ref[hd] + _dot_tn((kr * zeta_ref[hd]).astype(BF16), vb)
            mu = jnp.mean(out, axis=-1, keepdims=True)
            d = out - mu
            var = jnp.mean(d * d, axis=-1, keepdims=True)
            yn = d * lax.rsqrt(var + LN_EPS) * gain
            yg_ref[lo:hi, hd * dv:(hd + 1) * dv] = (g * jax.nn.sigmoid(g) * yn).astype(BF16)
    o_ref[...] = _dot(yg_ref[...], wp_ref[...]).astype(BF16)


def _retention_branch(x2, pos_col, freq2, sign2, wa_bf, wp_bf, consts, gain, batch, seq):
    n, d = x2.shape
    tr = min(256, seq)
    nr = seq // tr
    mask, xi_b, zeta_b, cd_b = consts
    vw = RET_HEADS * RET_V_DIM

    def once(shape):
        return pl.BlockSpec(shape, lambda b, r: (0,) * len(shape), pipeline_mode=pl.Buffered(1))

    rows = lambda w: pl.BlockSpec((tr, w), lambda b, r: (b * nr + r, 0))
    return pl.pallas_call(
        functools.partial(_retention_body, n_chunks=tr // CHUNK),
        grid=(batch, nr),
        in_specs=[rows(d), rows(1), once((1, LANES)), once((1, LANES)),
                  once((RET_HEADS, d, HEAD_COLS)), once((vw, d)),
                  once((RET_HEADS, CHUNK, CHUNK)), once((RET_HEADS, CHUNK, RET_QK_DIM)),
                  once((RET_HEADS, CHUNK, RET_QK_DIM)), once((RET_HEADS, 1, RET_V_DIM)), once((1, vw))],
        out_specs=[rows(d), rows(d)],
        out_shape=[jax.ShapeDtypeStruct((n, d), BF16), jax.ShapeDtypeStruct((n, d), BF16)],
        scratch_shapes=[pltpu.VMEM((RET_HEADS, RET_QK_DIM, RET_V_DIM), F32), pltpu.VMEM((tr, vw), BF16),
                        pltpu.VMEM((tr, LANES), F32), pltpu.VMEM((tr, LANES), F32)],
        compiler_params=_params(("arbitrary", "arbitrary")), name="retention_branch",
    )(x2, pos_col, freq2, sign2, wa_bf, wp_bf, mask, xi_b, zeta_b, cd_b, gain)


def _sgu_merge_body(u_ref, v_ref, sa_ref, sb_ref, ya_ref, lng_ref, lnb_ref, ws_ref, bs_ref, wp_ref,
                    o_ref, wtril_ref, ys_ref, *, n_chunks):
    @pl.when(pl.program_id(0) == 0)
    def _():
        r = lax.broadcasted_iota(jnp.int32, (CHUNK, CHUNK), 0)
        c = lax.broadcasted_iota(jnp.int32, (CHUNK, CHUNK), 1)
        for g in range(SGU_GROUPS):
            wtril_ref[g] = jnp.where(r >= c, ws_ref[g], 0.0).astype(BF16)

    vn = _layer_norm(v_ref[...].astype(F32), lng_ref[...], lnb_ref[...]).astype(BF16)
    for g in range(SGU_GROUPS):
        cols = pl.ds(g * CHUNK, CHUNK)
        lo = g * CHUNK
        rhs = jnp.concatenate([vn[c * CHUNK:(c + 1) * CHUNK, lo:lo + CHUNK] for c in range(n_chunks)], axis=1)
        mix = _dot(wtril_ref[g], rhs)
        bias = bs_ref[g]
        for c in range(n_chunks):
            rows = pl.ds(c * CHUNK, CHUNK)
            u = u_ref[rows, cols].astype(F32)
            ys_ref[rows, cols] = (u * (mix[:, c * CHUNK:(c + 1) * CHUNK] + bias)).astype(BF16)

    yb = _dot(ys_ref[...], wp_ref[...])
    o_ref[...] = (sa_ref[...].astype(F32) * ya_ref[...].astype(F32) + sb_ref[...].astype(F32) * yb).astype(BF16)


def _sgu_merge(h, y_a, ln_g, ln_b, w_s, b_s_b, wp_bf):
    n, d = y_a.shape
    ts = min(256, n)
    tile = lambda blk: pl.BlockSpec((ts, d), lambda i: (i, blk))
    row = pl.BlockSpec((1, d), lambda i: (0, 0))
    full3 = pl.BlockSpec((SGU_GROUPS, CHUNK, CHUNK), lambda i: (0, 0, 0))
    return pl.pallas_call(
        functools.partial(_sgu_merge_body, n_chunks=ts // CHUNK),
        grid=(n // ts,),
        in_specs=[tile(0), tile(1), tile(2), tile(3), tile(0), row, row, full3, full3,
                  pl.BlockSpec((d, d), lambda i: (0, 0), pipeline_mode=pl.Buffered(1))],
        out_specs=tile(0),
        out_shape=jax.ShapeDtypeStruct((n, d), BF16),
        scratch_shapes=[pltpu.VMEM((SGU_GROUPS, CHUNK, CHUNK), BF16), pltpu.VMEM((ts, d), BF16)],
        compiler_params=_params(("arbitrary",)), name="sgu_merge",
    )(h, h, h, h, y_a, ln_g, ln_b, w_s, b_s_b, wp_bf)


def _outproj_router_body(m_ref, x_ref, w_ref, g_ref, b_ref, whi_ref, wlo_ref, rb_ref,
                         o_ref, orow_ref, ids_ref, cw_ref, cnt_ref, carry_ref, tri_ref, *, tm):
    @pl.when(pl.program_id(0) == 0)
    def _():
        carry_ref[...] = jnp.zeros_like(carry_ref)
        r = lax.broadcasted_iota(jnp.int32, (tm, tm), 0)
        c = lax.broadcasted_iota(jnp.int32, (tm, tm), 1)
        tri_ref[...] = jnp.where(r < c, 1.0, 0.0).astype(BF16)

    r = DN_ALPHA * x_ref[...] + _dot(m_ref[...], w_ref[...])
    x = _layer_norm(r, g_ref[...], b_ref[...])
    o_ref[...] = x
    orow_ref[:, 0, :] = x

    hi = x.astype(BF16)
    lo = (x - hi.astype(F32)).astype(BF16)
    whi = whi_ref[...]
    logits = _dot_nt(whi, hi) + _dot_nt(whi, lo) + _dot_nt(wlo_ref[...], hi) + rb_ref[...]
    el = logits[0:N_EXPERTS, :]
    gl = logits[N_EXPERTS:N_EXPERTS + N_GROUPS, :]

    gi = lax.broadcasted_iota(jnp.int32, gl.shape, 0)
    gmax = jnp.max(gl, axis=0, keepdims=True)
    gidx = jnp.min(jnp.where(gl == gmax, gi, N_GROUPS), axis=0, keepdims=True)
    p_group = 1.0 / jnp.sum(jnp.exp(gl - gmax), axis=0, keepdims=True)

    ei = lax.broadcasted_iota(jnp.int32, el.shape, 0)
    first = gidx * EXPERTS_PER_GROUP
    in_group = (ei >= first) & (ei < first + EXPERTS_PER_GROUP)
    m1 = jnp.where(in_group, el, -jnp.inf)
    v1 = jnp.max(m1, axis=0, keepdims=True)
    i1 = jnp.min(jnp.where(in_group & (m1 == v1), ei, N_EXPERTS), axis=0, keepdims=True)
    rest = in_group & (ei != i1)
    m2 = jnp.where(rest, el, -jnp.inf)
    v2 = jnp.max(m2, axis=0, keepdims=True)
    i2 = jnp.min(jnp.where(rest & (m2 == v2), ei, N_EXPERTS), axis=0, keepdims=True)
    t = jnp.exp(v2 - v1)
    p1 = 1.0 / (1.0 + t)
    p2 = t * p1

    sel1, sel2 = ei == i1, ei == i2
    onehot = jnp.where(sel1 | sel2, 1.0, 0.0)
    before = _dot(onehot.astype(BF16), tri_ref[...]) + carry_ref[:, 0:1]
    r1 = jnp.sum(jnp.where(sel1, before, 0.0), axis=0, keepdims=True)
    r2 = jnp.sum(jnp.where(sel2, before, 0.0), axis=0, keepdims=True)
    carry_ref[...] = carry_ref[...] + jnp.sum(onehot, axis=1, keepdims=True)

    ids_ref[0:1, :] = i1
    ids_ref[1:2, :] = i2
    ids_ref[2:3, :] = r1.astype(jnp.int32)
    ids_ref[3:4, :] = r2.astype(jnp.int32)
    cw_ref[0:1, :] = p_group * p1
    cw_ref[1:2, :] = p_group * p2
    cnt_ref[...] = carry_ref[...]


def _out_proj_router(merged, x2, wo_bf, ln_g, ln_b, w_hi, w_lo, bias_col):
    n, d = x2.shape
    tm = min(512, n)
    tile = pl.BlockSpec((tm, d), lambda i: (i, 0))
    row = pl.BlockSpec((1, d), lambda i: (0, 0))
    full = pl.BlockSpec((ROUTER_ROWS, d), lambda i: (0, 0))
    return pl.pallas_call(
        functools.partial(_outproj_router_body, tm=tm), grid=(n // tm,),
        in_specs=[tile, tile, pl.BlockSpec((d, d), lambda i: (0, 0), pipeline_mode=pl.Buffered(1)), row, row,
                  full, full, pl.BlockSpec((ROUTER_ROWS, 1), lambda i: (0, 0))],
        out_specs=[tile, pl.BlockSpec((tm, 1, d), lambda i: (i, 0, 0)),
                   pl.BlockSpec((4, tm), lambda i: (0, i)),
                   pl.BlockSpec((2, tm), lambda i: (0, i)),
                   pl.BlockSpec((N_EXPERTS, LANES), lambda i: (0, 0))],
        out_shape=[jax.ShapeDtypeStruct((n, d), F32), jax.ShapeDtypeStruct((n, 1, d), F32),
                   jax.ShapeDtypeStruct((4, n), jnp.int32),
                   jax.ShapeDtypeStruct((2, n), F32),
                   jax.ShapeDtypeStruct((N_EXPERTS, LANES), F32)],
        scratch_shapes=[pltpu.VMEM((N_EXPERTS, LANES), F32), pltpu.VMEM((tm, tm), BF16)],
        compiler_params=_params(("arbitrary",)), name="out_proj_ln1_router",
    )(merged, x2, wo_bf, ln_g, ln_b, w_hi, w_lo, bias_col)


def _routing_tables(ids, counts, n, max_pairs):
    te = EXPERT_TILE
    n_tiles = 2 * n // te
    i32 = jnp.int32
    row_end = jnp.cumsum(counts)
    row_start = row_end - counts
    dest1 = row_start[ids[0]] + ids[2]
    dest2 = row_start[ids[1]] + ids[3]
    ne_cum = jnp.cumsum((counts > 0).astype(i32))
    tile_lo = jnp.arange(n_tiles, dtype=i32) * te
    first_e = jnp.sum((row_end[None, :] <= tile_lo[:, None]).astype(i32), axis=1)
    last_e = jnp.sum((row_end[None, :] <= (tile_lo + te - 1)[:, None]).astype(i32), axis=1)
    pairs_t = ne_cum[last_e] - ne_cum[first_e] + 1
    pair_end = jnp.cumsum(pairs_t)
    pair_start = pair_end - pairs_t
    n_pairs = pair_end[-1]
    p = jnp.minimum(jnp.arange(max_pairs, dtype=i32), n_pairs - 1)
    p_tile = jnp.sum((pair_end[None, :] <= p[:, None]).astype(i32), axis=1)
    rank = ne_cum[first_e[p_tile]] - 1 + (p - pair_start[p_tile])
    p_exp = jnp.sum((ne_cum[None, :] <= rank[:, None]).astype(i32), axis=1)
    p_lo = jnp.maximum(row_start[p_exp], p_tile * te) - p_tile * te
    p_hi = jnp.minimum(row_end[p_exp], (p_tile + 1) * te) - p_tile * te
    p_next = jnp.where(rank + 1 < ne_cum[-1],
                       jnp.sum((ne_cum[None, :] <= (rank + 1)[:, None]).astype(i32), axis=1), -1)
    p_slot = rank & 1
    return (dest1, dest2, p_tile.astype(i32), p_exp.astype(i32), p_lo.astype(i32), p_hi.astype(i32),
            n_pairs.reshape(1).astype(i32), p_next.astype(i32), p_slot.astype(i32))


def _row_copy(src, dst, sem):
    return pltpu.make_async_copy(src, dst, sem)


def _expert_body(ptile_ref, pexp_ref, plo_ref, phi_ref, np_ref, pnext_ref, pslot_ref, d1_ref, d2_ref,
                 x1_hbm, w1_hbm, w3_hbm, w2_hbm, y2_hbm,
                 w1b, w3b, w2b, wf1, wf3, wf2, xbuf0, xbuf1, ybuf0, ybuf1, tok_s, inv_s, gsem, ssem, wsem,
                 *, n, n_tiles):
    te = EXPERT_TILE
    p = pl.program_id(0)
    valid = p < np_ref[0]
    t = ptile_ref[p]
    pm1 = jnp.maximum(p - 1, 0)
    first = (p == 0) | (ptile_ref[pm1] != t)
    new_expert = (p == 0) | (pexp_ref[pm1] != pexp_ref[p])
    lo, hi = plo_ref[p], phi_ref[p]

    def gather(tile, dst, j, priority=0):
        _row_copy(x1_hbm.at[tok_s[tile * te + j]], dst.at[pl.ds(j, 1)], gsem).start(priority=priority)

    def scatter(tile, src, j, priority=0):
        _row_copy(src.at[pl.ds(j, 1)], y2_hbm.at[inv_s[tile * te + j]], ssem).start(priority=priority)

    def wait_gather(dst):
        _row_copy(x1_hbm.at[pl.ds(0, te), 0], dst, gsem).wait()

    def wait_scatter(src):
        _row_copy(src, y2_hbm.at[pl.ds(0, te), 0], ssem).wait()

    def issue_loop(fn, tile, buf):
        def one(j, c):
            fn(tile, buf, j)
            return c
        lax.fori_loop(0, te, one, 0, unroll=8)

    @pl.when(p == 0)
    def _():
        def invert(tk, c):
            a, b = d1_ref[tk], d2_ref[tk]
            tok_s[a] = tk
            inv_s[a] = tk
            tok_s[b] = tk
            inv_s[b] = tk + n
            return c
        lax.fori_loop(0, n, invert, 0, unroll=8)
        ybuf0[...] = jnp.zeros_like(ybuf0)
        ybuf1[...] = jnp.zeros_like(ybuf1)
        issue_loop(gather, 0, xbuf0)

    def weight_copies(e, slot):
        return [_row_copy(w1_hbm.at[e], wf1.at[slot], wsem.at[slot]),
                _row_copy(w3_hbm.at[e], wf3.at[slot], wsem.at[slot]),
                _row_copy(w2_hbm.at[e], wf2.at[slot], wsem.at[slot])]

    @pl.when(p == 0)
    def _():
        for c in weight_copies(pexp_ref[0], pslot_ref[0]):
            c.start()

    @pl.when(valid & new_expert)
    def _():
        slot = pslot_ref[p]
        for c in weight_copies(pexp_ref[p], slot):
            c.wait()
        w1b[...] = wf1[slot].astype(BF16)
        w3b[...] = wf3[slot].astype(BF16)
        w2b[...] = wf2[slot].astype(BF16)

        @pl.when(pnext_ref[p] >= 0)
        def _():
            for c in weight_copies(pnext_ref[p], 1 - slot):
                c.start()

    def ffn(xb_ref, yb_ref):
        xb = xb_ref[...].astype(BF16)
        a = _dot(xb, w1b[...])
        b = _dot(xb, w3b[...])
        y = _dot((a * jax.nn.sigmoid(a) * b).astype(BF16), w2b[...])
        rows = lax.broadcasted_iota(jnp.int32, (te, 1), 0)
        yb_ref[...] = jnp.where((rows >= lo) & (rows < hi), y, yb_ref[...])

    for par, (xb, yb, xo, yo) in enumerate(((xbuf0, ybuf0, xbuf1, ybuf1), (xbuf1, ybuf1, xbuf0, ybuf0))):
        mine = valid & ((t & 1) == par)
        interior = first & (t >= 1) & (t < n_tiles - 1)

        @pl.when(mine & interior)
        def _():
            wait_gather(xb)

            @pl.when(t >= 2)
            def _():
                wait_scatter(yb)

            for j in range(te):
                gather(t + 1, xo, j, priority=j % 2)

            @pl.when(hi > 0)
            def _():
                for j in range(te):
                    scatter(t - 1, yo, j, priority=j % 2)
                ffn(xb, yb)

        @pl.when(mine & jnp.logical_not(interior))
        def _():
            @pl.when(first & (t == 0))
            def _():
                wait_gather(xb)
                if n_tiles > 1:
                    issue_loop(gather, 1, xo)

            @pl.when(first & (t == n_tiles - 1) & (t >= 1))
            def _():
                wait_gather(xb)

                @pl.when(t >= 2)
                def _():
                    wait_scatter(yb)

                issue_loop(scatter, t - 1, yo)

            ffn(xb, yb)

    @pl.when(p == pl.num_programs(0) - 1)
    def _():
        last = n_tiles - 1
        y_last, y_prev = (ybuf0, ybuf1) if last % 2 == 0 else (ybuf1, ybuf0)
        if n_tiles >= 2:
            wait_scatter(y_prev)
        issue_loop(scatter, last, y_last)
        wait_scatter(y_last)


def _experts(x1_rows, tables, w1, w3, w2):
    n, _, d = x1_rows.shape
    de = w1.shape[2]
    te = EXPERT_TILE
    n_tiles = 2 * n // te
    max_pairs = n_tiles + N_EXPERTS - 1
    hbm = pl.BlockSpec(memory_space=pl.ANY)
    return pl.pallas_call(
        functools.partial(_expert_body, n=n, n_tiles=n_tiles),
        grid_spec=pltpu.PrefetchScalarGridSpec(
            num_scalar_prefetch=9, grid=(max_pairs,),
            in_specs=[hbm, hbm, hbm, hbm],
            out_specs=hbm,
            scratch_shapes=[pltpu.VMEM((d, de), BF16), pltpu.VMEM((d, de), BF16), pltpu.VMEM((de, d), BF16),
                            pltpu.VMEM((2, d, de), F32), pltpu.VMEM((2, d, de), F32), pltpu.VMEM((2, de, d), F32),
                            pltpu.VMEM((te, d), F32), pltpu.VMEM((te, d), F32),
                            pltpu.VMEM((te, d), F32), pltpu.VMEM((te, d), F32),
                            pltpu.SMEM((2 * n,), jnp.int32), pltpu.SMEM((2 * n,), jnp.int32),
                            pltpu.SemaphoreType.DMA(()), pltpu.SemaphoreType.DMA(()),
                            pltpu.SemaphoreType.DMA((2,))]),
        out_shape=jax.ShapeDtypeStruct((2 * n, 1, d), F32),
        compiler_params=_params(("arbitrary",)), name="experts",
    )(*tables, x1_rows, w1, w3, w2)


def _combine_body(x_ref, ya_ref, yb_ref, cw_ref, g_ref, b_ref, o_ref):
    y = cw_ref[:, 0:1] * ya_ref[:, 0, :] + cw_ref[:, 1:2] * yb_ref[:, 0, :]
    o_ref[...] = _layer_norm(DN_ALPHA * x_ref[...] + y, g_ref[...], b_ref[...])


def _combine(y2, x1, cw_col, ln_g, ln_b):
    n, d = x1.shape
    tm = min(512, n)
    nt = n // tm
    tile = pl.BlockSpec((tm, d), lambda i: (i, 0))
    row = pl.BlockSpec((1, d), lambda i: (0, 0))
    return pl.pallas_call(
        _combine_body, grid=(nt,),
        in_specs=[tile, pl.BlockSpec((tm, 1, d), lambda i: (i, 0, 0)),
                  pl.BlockSpec((tm, 1, d), lambda i: (i + nt, 0, 0)),
                  pl.BlockSpec((tm, 2), lambda i: (i, 0)), row, row],
        out_specs=tile,
        out_shape=jax.ShapeDtypeStruct((n, d), F32),
        compiler_params=_params(("arbitrary",)), name="combine_ln2",
    )(x1, y2, y2, cw_col, ln_g, ln_b)


def _retention_consts():
    h, c = RET_HEADS, CHUNK
    log_gamma = jnp.log1p(-jnp.exp2(-5.0 - jnp.arange(h, dtype=F32)))
    idx = jnp.arange(c, dtype=F32)
    rel = idx[:, None] - idx[None, :]
    mask = jnp.where(rel >= 0, jnp.exp(log_gamma[:, None, None] * jnp.maximum(rel, 0.0)), 0.0)
    xi = jnp.exp(log_gamma[:, None] * (idx + 1.0))
    zeta = jnp.exp(log_gamma[:, None] * (c - 1.0 - idx))
    chunk_decay = jnp.exp(log_gamma * c)
    xi_b = jnp.broadcast_to(xi[:, :, None], (h, c, RET_QK_DIM))
    zeta_b = jnp.broadcast_to(zeta[:, :, None], (h, c, RET_QK_DIM))
    cd_b = jnp.broadcast_to(chunk_decay[:, None, None], (h, 1, RET_V_DIM))
    return mask, xi_b, zeta_b, cd_b


def _rope_consts():
    half = RET_QK_DIM // 2
    freq = ROPE_THETA ** (-jnp.arange(half, dtype=F32) / half)
    freq2 = jnp.concatenate([freq, freq])[None, :]
    sign2 = jnp.concatenate([-jnp.ones((half,), F32), jnp.ones((half,), F32)])[None, :]
    return freq2, sign2


def kernel(x, positions, w_in, b_gate, ret_gn_g, sgu_ln_g, sgu_ln_b, sgu_w, sgu_b, w_proj_ret, w_proj_sgu,
           w_out, ln1_g, ln1_b, w_group, b_group, w_er, b_er, w1, w3, w2, ln2_g, ln2_b):
    batch, seq, d = x.shape
    n = batch * seq
    qk_w, v_w = RET_HEADS * RET_QK_DIM, RET_HEADS * RET_V_DIM
    assert d == v_w == SGU_GROUPS * CHUNK and seq % CHUNK == 0
    ret_cols = 2 * qk_w + 2 * v_w
    max_pairs = (2 * n) // EXPERT_TILE + N_EXPERTS - 1

    freq2, sign2 = _rope_consts()
    pos_col = positions.reshape(n, 1)
    ret_consts = _retention_consts()

    xc = x.reshape(n, d)
    for l in range(w_in.shape[0]):
        wa = _regroup_retention_weights(w_in[l])
        y_a, xb = _retention_branch(xc, pos_col, freq2, sign2, wa, w_proj_ret[l].astype(BF16), ret_consts,
                                    ret_gn_g[l][None, :], batch, seq)
        gate_bias = jnp.concatenate([jnp.zeros((2 * d,), F32), b_gate[l].reshape(-1)])[None, :]
        h = _in_proj(xb, w_in[l], ret_cols, gate_bias)
        bias_b = jnp.broadcast_to(sgu_b[l][:, :, None], (SGU_GROUPS, CHUNK, CHUNK))
        merged = _sgu_merge(h, y_a, sgu_ln_g[l][None, :], sgu_ln_b[l][None, :], sgu_w[l], bias_b,
                            w_proj_sgu[l].astype(BF16))

        w_r = jnp.concatenate([w_er[l], w_group[l]], axis=1).T
        w_r = jnp.pad(w_r, ((0, ROUTER_ROWS - w_r.shape[0]), (0, 0)))
        w_hi = w_r.astype(BF16)
        w_lo = (w_r - w_hi.astype(F32)).astype(BF16)
        bias = jnp.pad(jnp.concatenate([b_er[l], b_group[l]]), (0, ROUTER_ROWS - N_EXPERTS - N_GROUPS))[:, None]
        x1, x1_rows, ids, cw, cnt = _out_proj_router(merged, xc, w_out[l].astype(BF16), ln1_g[l][None, :],
                                                     ln1_b[l][None, :], w_hi, w_lo, bias)

        counts = cnt[:, 0].astype(jnp.int32)
        dest1, dest2, p_tile, p_exp, p_lo, p_hi, n_pairs, p_next, p_slot = _routing_tables(ids, counts, n, max_pairs)
        y2 = _experts(x1_rows, (p_tile, p_exp, p_lo, p_hi, n_pairs, p_next, p_slot, dest1, dest2),
                      w1[l], w3[l], w2[l])
        xc = _combine(y2, x1, cw.T, ln2_g[l][None, :], ln2_b[l][None, :])
    return xc.reshape(batch, seq, d)
```

```python
import functools

import numpy as np
import jax
import jax.numpy as jnp
from jax import lax
from jax.experimental import pallas as pl
from jax.experimental.pallas import tpu as pltpu

F32 = jnp.float32
BF16 = jnp.bfloat16

RET_HEADS = 8
RET_QK_DIM = 128
RET_V_DIM = 256
CHUNK = 128
ROPE_THETA = 10000.0
SGU_GROUPS = 16
N_GROUPS = 4
EXPERTS_PER_GROUP = 8
N_EXPERTS = N_GROUPS * EXPERTS_PER_GROUP
ROUTER_ROWS = 40
LN_EPS = 1e-5
DEPTH = 1
DN_ALPHA = (2 * DEPTH) ** 0.25
SQRT_HALF = np.sqrt(0.5).astype(np.float32)

LANES = 128
VMEM_LIMIT = 52 * 1024 * 1024
EXPERT_TILE = 256


def _params(sem, vmem=VMEM_LIMIT, flags=None):
    return pltpu.CompilerParams(dimension_semantics=sem, vmem_limit_bytes=vmem, flags=flags)


def _dot(a, b):
    return jnp.dot(a, b, preferred_element_type=F32)


def _dot_nt(a, b):
    return lax.dot_general(a, b, (((1,), (1,)), ((), ())), preferred_element_type=F32)


def _dot_tn(a, b):
    return lax.dot_general(a, b, (((0,), (0,)), ((), ())), preferred_element_type=F32)


def _layer_norm(r, g, b):
    mu = jnp.mean(r, axis=-1, keepdims=True)
    d = r - mu
    var = jnp.mean(d * d, axis=-1, keepdims=True)
    return d * lax.rsqrt(var + LN_EPS) * g + b


def _gelu(x):
    return 0.5 * x * (1.0 + lax.erf(x * SQRT_HALF))


def _inproj_body(x_ref, w_ref, b_ref, o_ref, wb_ref, *, gelu_blocks):
    @pl.when(pl.program_id(1) == 0)
    def _():
        wb_ref[...] = w_ref[...].astype(BF16)

    @pl.when(pl.program_id(0) < gelu_blocks)
    def _():
        o_ref[...] = _gelu(_dot(x_ref[...], wb_ref[...])).astype(BF16)

    @pl.when(pl.program_id(0) >= gelu_blocks)
    def _():
        o_ref[...] = jax.nn.sigmoid(_dot(x_ref[...], wb_ref[...]) + b_ref[...]).astype(BF16)


def _in_proj(xb, w, col0, gate_bias):
    n, d = xb.shape
    width = w.shape[1] - col0
    tm, tn = min(1024, n), 1024
    assert col0 % tn == 0 and width % (2 * tn) == 0
    return pl.pallas_call(
        functools.partial(_inproj_body, gelu_blocks=width // (2 * tn)),
        grid=(width // tn, n // tm),
        in_specs=[pl.BlockSpec((tm, d), lambda j, i: (i, 0)),
                  pl.BlockSpec((d, tn), lambda j, i: (0, j + col0 // tn)),
                  pl.BlockSpec((1, tn), lambda j, i: (0, j))],
        out_specs=pl.BlockSpec((tm, tn), lambda j, i: (i, j)),
        out_shape=jax.ShapeDtypeStruct((n, width), BF16),
        scratch_shapes=[pltpu.VMEM((d, tn), BF16)],
        compiler_params=_params(("arbitrary", "arbitrary")), name="in_proj",
    )(xb, w, gate_bias)


HEAD_COLS = 2 * RET_QK_DIM + 2 * RET_V_DIM


def _regroup_body(q_ref, k_ref, v_ref, g_ref, o_ref):
    dk, dv = RET_QK_DIM, RET_V_DIM
    o_ref[0, :, 0:dk] = q_ref[...].astype(BF16)
    o_ref[0, :, dk:2 * dk] = k_ref[...].astype(BF16)
    o_ref[0, :, 2 * dk:2 * dk + dv] = v_ref[...].astype(BF16)
    o_ref[0, :, 2 * dk + dv:] = g_ref[...].astype(BF16)


def _regroup_retention_weights(w):
    d = w.shape[0]
    k_blk = RET_HEADS
    v_blk = 2 * RET_HEADS * RET_QK_DIM // RET_V_DIM
    return pl.pallas_call(
        _regroup_body, grid=(RET_HEADS,),
        in_specs=[pl.BlockSpec((d, RET_QK_DIM), lambda h: (0, h)),
                  pl.BlockSpec((d, RET_QK_DIM), lambda h: (0, k_blk + h)),
                  pl.BlockSpec((d, RET_V_DIM), lambda h: (0, v_blk + h)),
                  pl.BlockSpec((d, RET_V_DIM), lambda h: (0, v_blk + RET_HEADS + h))],
        out_specs=pl.BlockSpec((1, d, HEAD_COLS), lambda h: (h, 0, 0)),
        out_shape=jax.ShapeDtypeStruct((RET_HEADS, d, HEAD_COLS), BF16),
        compiler_params=_params(("arbitrary",)), name="regroup_retention_weights",
    )(w, w, w, w)


def _retention_body(x_ref, pos_ref, freq_ref, sign_ref, wa_ref, wp_ref, mask_ref, xi_ref, zeta_ref, cd_ref,
                    gain_ref, o_ref, xb_ref, state_ref, yg_ref, cos_ref, sin_ref, *, n_chunks):
    @pl.when(pl.program_id(1) == 0)
    def _():
        state_ref[...] = jnp.zeros_like(state_ref)

    ang = pos_ref[...].astype(F32) * freq_ref[...]
    cos_ref[...] = jnp.cos(ang)
    sin_ref[...] = jnp.sin(ang) * sign_ref[...]

    scale = RET_QK_DIM ** -0.5
    half = RET_QK_DIM // 2
    dk, dv = RET_QK_DIM, RET_V_DIM
    xb = x_ref[...].astype(BF16)
    xb_ref[...] = xb
    ahead = 1
    projs = [_dot(xb, wa_ref[h]) for h in range(ahead)]
    for hd in range(RET_HEADS):
        proj = projs[hd]
        if hd + ahead < RET_HEADS:
            projs.append(_dot(xb, wa_ref[hd + ahead]))
        gain = gain_ref[:, hd * dv:(hd + 1) * dv]
        for c in range(n_chunks):
            lo, hi = c * CHUNK, (c + 1) * CHUNK
            cos, sin = cos_ref[lo:hi, :], sin_ref[lo:hi, :]
            q, k = proj[lo:hi, 0:dk], proj[lo:hi, dk:2 * dk]
            vb = proj[lo:hi, 2 * dk:2 * dk + dv].astype(BF16)
            g = proj[lo:hi, 2 * dk + dv:]
            qr = q * cos + pltpu.roll(q, half, 1) * sin
            kr = (k * cos + pltpu.roll(k, half, 1) * sin) * scale
            scores = _dot_nt(qr.astype(BF16), kr.astype(BF16)) * mask_ref[hd]
            state = state_ref[hd]
            lhs = jnp.concatenate([scores.astype(BF16), (qr * xi_ref[hd]).astype(BF16)], axis=1)
            rhs = jnp.concatenate([vb, state.astype(BF16)], axis=0)
            out = _dot(lhs, rhs)
            state_ref[hd] = state * cd_ref[hd] + _dot_tn((kr * zeta_ref[hd]).astype(BF16), vb)
            mu = jnp.mean(out, axis=-1, keepdims=True)
            d = out - mu
            var = jnp.mean(d * d, axis=-1, keepdims=True)
            yn = d * lax.rsqrt(var + LN_EPS) * gain
            yg_ref[lo:hi, hd * dv:(hd + 1) * dv] = (g * jax.nn.sigmoid(g) * yn).astype(BF16)
    o_ref[...] = _dot(yg_ref[...], wp_ref[...]).astype(BF16)


def _retention_branch(x2, pos_col, freq2, sign2, wa_bf, wp_bf, consts, gain, batch, seq):
    n, d = x2.shape
    tr = min(256, seq)
    nr = seq // tr
    mask, xi_b, zeta_b, cd_b = consts
    vw = RET_HEADS * RET_V_DIM

    def once(shape):
        return pl.BlockSpec(shape, lambda b, r: (0,) * len(shape), pipeline_mode=pl.Buffered(1))

    rows = lambda w: pl.BlockSpec((tr, w), lambda b, r: (b * nr + r, 0))
    return pl.pallas_call(
        functools.partial(_retention_body, n_chunks=tr // CHUNK),
        grid=(batch, nr),
        in_specs=[rows(d), rows(1), once((1, LANES)), once((1, LANES)),
                  once((RET_HEADS, d, HEAD_COLS)), once((vw, d)),
                  once((RET_HEADS, CHUNK, CHUNK)), once((RET_HEADS, CHUNK, RET_QK_DIM)),
                  once((RET_HEADS, CHUNK, RET_QK_DIM)), once((RET_HEADS, 1, RET_V_DIM)), once((1, vw))],
        out_specs=[rows(d), rows(d)],
        out_shape=[jax.ShapeDtypeStruct((n, d), BF16), jax.ShapeDtypeStruct((n, d), BF16)],
        scratch_shapes=[pltpu.VMEM((RET_HEADS, RET_QK_DIM, RET_V_DIM), F32), pltpu.VMEM((tr, vw), BF16),
                        pltpu.VMEM((tr, LANES), F32), pltpu.VMEM((tr, LANES), F32)],
        compiler_params=_params(("arbitrary", "arbitrary")), name="retention_branch",
    )(x2, pos_col, freq2, sign2, wa_bf, wp_bf, mask, xi_b, zeta_b, cd_b, gain)


def _sgu_merge_body(u_ref, v_ref, sa_ref, sb_ref, ya_ref, lng_ref, lnb_ref, ws_ref, bs_ref, wp_ref,
                    o_ref, wtril_ref, ys_ref, *, n_chunks):
    @pl.when(pl.program_id(0) == 0)
    def _():
        r = lax.broadcasted_iota(jnp.int32, (CHUNK, CHUNK), 0)
        c = lax.broadcasted_iota(jnp.int32, (CHUNK, CHUNK), 1)
        for g in range(SGU_GROUPS):
            wtril_ref[g] = jnp.where(r >= c, ws_ref[g], 0.0).astype(BF16)

    vn = _layer_norm(v_ref[...].astype(F32), lng_ref[...], lnb_ref[...]).astype(BF16)
    for g in range(SGU_GROUPS):
        cols = pl.ds(g * CHUNK, CHUNK)
        lo = g * CHUNK
        rhs = jnp.concatenate([vn[c * CHUNK:(c + 1) * CHUNK, lo:lo + CHUNK] for c in range(n_chunks)], axis=1)
        mix = _dot(wtril_ref[g], rhs)
        bias = bs_ref[g]
        for c in range(n_chunks):
            rows = pl.ds(c * CHUNK, CHUNK)
            u = u_ref[rows, cols].astype(F32)
            ys_ref[rows, cols] = (u * (mix[:, c * CHUNK:(c + 1) * CHUNK] + bias)).astype(BF16)

    yb = _dot(ys_ref[...], wp_ref[...])
    o_ref[...] = (sa_ref[...].astype(F32) * ya_ref[...].astype(F32) + sb_ref[...].astype(F32) * yb).astype(BF16)


def _sgu_merge(h, y_a, ln_g, ln_b, w_s, b_s_b, wp_bf):
    n, d = y_a.shape
    ts = min(256, n)
    tile = lambda blk: pl.BlockSpec((ts, d), lambda i: (i, blk))
    row = pl.BlockSpec((1, d), lambda i: (0, 0))
    full3 = pl.BlockSpec((SGU_GROUPS, CHUNK, CHUNK), lambda i: (0, 0, 0))
    return pl.pallas_call(
        functools.partial(_sgu_merge_body, n_chunks=ts // CHUNK),
        grid=(n // ts,),
        in_specs=[tile(0), tile(1), tile(2), tile(3), tile(0), row, row, full3, full3,
                  pl.BlockSpec((d, d), lambda i: (0, 0), pipeline_mode=pl.Buffered(1))],
        out_specs=tile(0),
        out_shape=jax.ShapeDtypeStruct((n, d), BF16),
        scratch_shapes=[pltpu.VMEM((SGU_GROUPS, CHUNK, CHUNK), BF16), pltpu.VMEM((ts, d), BF16)],
        compiler_params=_params(("arbitrary",)), name="sgu_merge",
    )(h, h, h, h, y_a, ln_g, ln_b, w_s, b_s_b, wp_bf)


def _outproj_router_body(m_ref, x_ref, w_ref, g_ref, b_ref, whi_ref, wlo_ref, rb_ref,
                         o_ref, orow_ref, ids_ref, cw_ref, cnt_ref, carry_ref, tri_ref, *, tm):
    @pl.when(pl.program_id(0) == 0)
    def _():
        carry_ref[...] = jnp.zeros_like(carry_ref)
        r = lax.broadcasted_iota(jnp.int32, (tm, tm), 0)
        c = lax.broadcasted_iota(jnp.int32, (tm, tm), 1)
        tri_ref[...] = jnp.where(r < c, 1.0, 0.0).astype(BF16)

    r = DN_ALPHA * x_ref[...] + _dot(m_ref[...], w_ref[...])
    x = _layer_norm(r, g_ref[...], b_ref[...])
    o_ref[...] = x
    orow_ref[:, 0, :] = x

    hi = x.astype(BF16)
    lo = (x - hi.astype(F32)).astype(BF16)
    whi = whi_ref[...]
    logits = _dot_nt(whi, hi) + _dot_nt(whi, lo) + _dot_nt(wlo_ref[...], hi) + rb_ref[...]
    el = logits[0:N_EXPERTS, :]
    gl = logits[N_EXPERTS:N_EXPERTS + N_GROUPS, :]

    gi = lax.broadcasted_iota(jnp.int32, gl.shape, 0)
    gmax = jnp.max(gl, axis=0, keepdims=True)
    gidx = jnp.min(jnp.where(gl == gmax, gi, N_GROUPS), axis=0, keepdims=True)
    p_group = 1.0 / jnp.sum(jnp.exp(gl - gmax), axis=0, keepdims=True)

    ei = lax.broadcasted_iota(jnp.int32, el.shape, 0)
    first = gidx * EXPERTS_PER_GROUP
    in_group = (ei >= first) & (ei < first + EXPERTS_PER_GROUP)
    m1 = jnp.where(in_group, el, -jnp.inf)
    v1 = jnp.max(m1, axis=0, keepdims=True)
    i1 = jnp.min(jnp.where(in_group & (m1 == v1), ei, N_EXPERTS), axis=0, keepdims=True)
    rest = in_group & (ei != i1)
    m2 = jnp.where(rest, el, -jnp.inf)
    v2 = jnp.max(m2, axis=0, keepdims=True)
    i2 = jnp.min(jnp.where(rest & (m2 == v2), ei, N_EXPERTS), axis=0, keepdims=True)
    t = jnp.exp(v2 - v1)
    p1 = 1.0 / (1.0 + t)
    p2 = t * p1

    sel1, sel2 = ei == i1, ei == i2
    onehot = jnp.where(sel1 | sel2, 1.0, 0.0)
    before = _dot(onehot.astype(BF16), tri_ref[...]) + carry_ref[:, 0:1]
    r1 = jnp.sum(jnp.where(sel1, before, 0.0), axis=0, keepdims=True)
    r2 = jnp.sum(jnp.where(sel2, before, 0.0), axis=0, keepdims=True)
    carry_ref[...] = carry_ref[...] + jnp.sum(onehot, axis=1, keepdims=True)

    ids_ref[0:1, :] = i1
    ids_ref[1:2, :] = i2
    ids_ref[2:3, :] = r1.astype(jnp.int32)
    ids_ref[3:4, :] = r2.astype(jnp.int32)
    cw_ref[0:1, :] = p_group * p1
    cw_ref[1:2, :] = p_group * p2
    cnt_ref[...] = carry_ref[...]


def _out_proj_router(merged, x2, wo_bf, ln_g, ln_b, w_hi, w_lo, bias_col):
    n, d = x2.shape
    tm = min(512, n)
    tile = pl.BlockSpec((tm, d), lambda i: (i, 0))
    row = pl.BlockSpec((1, d), lambda i: (0, 0))
    full = pl.BlockSpec((ROUTER_ROWS, d), lambda i: (0, 0))
    return pl.pallas_call(
        functools.partial(_outproj_router_body, tm=tm), grid=(n // tm,),
        in_specs=[tile, tile, pl.BlockSpec((d, d), lambda i: (0, 0), pipeline_mode=pl.Buffered(1)), row, row,
                  full, full, pl.BlockSpec((ROUTER_ROWS, 1), lambda i: (0, 0))],
        out_specs=[tile, pl.BlockSpec((tm, 1, d), lambda i: (i, 0, 0)),
                   pl.BlockSpec((4, tm), lambda i: (0, i)),
                   pl.BlockSpec((2, tm), lambda i: (0, i)),
                   pl.BlockSpec((N_EXPERTS, LANES), lambda i: (0, 0))],
        out_shape=[jax.ShapeDtypeStruct((n, d), F32), jax.ShapeDtypeStruct((n, 1, d), F32),
                   jax.ShapeDtypeStruct((4, n), jnp.int32),
                   jax.ShapeDtypeStruct((2, n), F32),
                   jax.ShapeDtypeStruct((N_EXPERTS, LANES), F32)],
        scratch_shapes=[pltpu.VMEM((N_EXPERTS, LANES), F32), pltpu.VMEM((tm, tm), BF16)],
        compiler_params=_params(("arbitrary",)), name="out_proj_ln1_router",
    )(merged, x2, wo_bf, ln_g, ln_b, w_hi, w_lo, bias_col)


def _routing_tables(ids, counts, n, max_pairs):
    te = EXPERT_TILE
    n_tiles = 2 * n // te
    i32 = jnp.int32
    row_end = jnp.cumsum(counts)
    row_start = row_end - counts
    dest1 = row_start[ids[0]] + ids[2]
    dest2 = row_start[ids[1]] + ids[3]
    ne_cum = jnp.cumsum((counts > 0).astype(i32))
    tile_lo = jnp.arange(n_tiles, dtype=i32) * te
    first_e = jnp.sum((row_end[None, :] <= tile_lo[:, None]).astype(i32), axis=1)
    last_e = jnp.sum((row_end[None, :] <= (tile_lo + te - 1)[:, None]).astype(i32), axis=1)
    pairs_t = ne_cum[last_e] - ne_cum[first_e] + 1
    pair_end = jnp.cumsum(pairs_t)
    pair_start = pair_end - pairs_t
    n_pairs = pair_end[-1]
    p = jnp.minimum(jnp.arange(max_pairs, dtype=i32), n_pairs - 1)
    p_tile = jnp.sum((pair_end[None, :] <= p[:, None]).astype(i32), axis=1)
    rank = ne_cum[first_e[p_tile]] - 1 + (p - pair_start[p_tile])
    p_exp = jnp.sum((ne_cum[None, :] <= rank[:, None]).astype(i32), axis=1)
    p_lo = jnp.maximum(row_start[p_exp], p_tile * te) - p_tile * te
    p_hi = jnp.minimum(row_end[p_exp], (p_tile + 1) * te) - p_tile * te
    p_next = jnp.where(rank + 1 < ne_cum[-1],
                       jnp.sum((ne_cum[None, :] <= (rank + 1)[:, None]).astype(i32), axis=1), -1)
    p_slot = rank & 1
    return (dest1, dest2, p_tile.astype(i32), p_exp.astype(i32), p_lo.astype(i32), p_hi.astype(i32),
            n_pairs.reshape(1).astype(i32), p_next.astype(i32), p_slot.astype(i32))


def _row_copy(src, dst, sem):
    return pltpu.make_async_copy(src, dst, sem)


def _expert_body(ptile_ref, pexp_ref, plo_ref, phi_ref, np_ref, pnext_ref, pslot_ref, d1_ref, d2_ref,
                 x1_hbm, w1_hbm, w3_hbm, w2_hbm, y2_hbm,
                 w1b, w3b, w2b, wf1, wf3, wf2, xbuf0, xbuf1, xbuf2, ybuf0, ybuf1, ybuf2, inv_s, gsem, ssem, wsem,
                 *, n, n_tiles):
    te = EXPERT_TILE
    p = pl.program_id(0)
    valid = p < np_ref[0]
    t = ptile_ref[p]
    pm1 = jnp.maximum(p - 1, 0)
    first = (p == 0) | (ptile_ref[pm1] != t)
    new_expert = (p == 0) | (pexp_ref[pm1] != pexp_ref[p])
    lo, hi = plo_ref[p], phi_ref[p]
    xbufs, ybufs = (xbuf0, xbuf1, xbuf2), (ybuf0, ybuf1, ybuf2)

    def gather(tile, slot, j, priority=0):
        v = inv_s[tile * te + j]
        tok = jnp.where(v >= n, v - n, v)
        _row_copy(x1_hbm.at[tok], xbufs[slot].at[pl.ds(j, 1)], gsem.at[slot]).start(priority=priority)

    def scatter(tile, slot, j, priority=0):
        _row_copy(ybufs[slot].at[pl.ds(j, 1)], y2_hbm.at[inv_s[tile * te + j]], ssem).start(priority=priority)

    def wait_gather(slot):
        _row_copy(x1_hbm.at[pl.ds(0, te), 0], xbufs[slot], gsem.at[slot]).wait()

    def wait_scatter(slot):
        _row_copy(ybufs[slot], y2_hbm.at[pl.ds(0, te), 0], ssem).wait()

    def issue_loop(fn, tile, slot):
        def one(j, c):
            fn(tile, slot, j)
            return c
        lax.fori_loop(0, te, one, 0, unroll=8)

    def weight_copies(e, slot):
        return [_row_copy(w1_hbm.at[e], wf1.at[slot], wsem.at[slot]),
                _row_copy(w3_hbm.at[e], wf3.at[slot], wsem.at[slot]),
                _row_copy(w2_hbm.at[e], wf2.at[slot], wsem.at[slot])]

    @pl.when(p == 0)
    def _():
        for c in weight_copies(pexp_ref[0], pslot_ref[0]):
            c.start()

        def invert(tk, c):
            inv_s[d1_ref[tk]] = tk
            inv_s[d2_ref[tk]] = tk + n
            return c
        lax.fori_loop(0, n, invert, 0, unroll=8)
        for yb in ybufs:
            yb[...] = jnp.zeros_like(yb)
        issue_loop(gather, 0, 0)
        issue_loop(gather, 1, 1)

    @pl.when(valid & new_expert)
    def _():
        slot = pslot_ref[p]
        for c in weight_copies(pexp_ref[p], slot):
            c.wait()
        w1b[...] = wf1[slot].astype(BF16)
        w3b[...] = wf3[slot].astype(BF16)
        w2b[...] = wf2[slot].astype(BF16)

        @pl.when(pnext_ref[p] >= 0)
        def _():
            for c in weight_copies(pnext_ref[p], 1 - slot):
                c.start()

    def ffn(slot):
        xb = xbufs[slot][...].astype(BF16)
        a = _dot(xb, w1b[...])
        b = _dot(xb, w3b[...])
        y = _dot((a * jax.nn.sigmoid(a) * b).astype(BF16), w2b[...])
        rows = lax.broadcasted_iota(jnp.int32, (te, 1), 0)
        ybufs[slot][...] = jnp.where((rows >= lo) & (rows < hi), y, ybufs[slot][...])

    for r in range(3):
        ahead, behind = (r + 2) % 3, (r + 1) % 3
        mine = valid & (lax.rem(t, 3) == r)
        interior = first & (t >= 1) & (t < n_tiles - 2)

        @pl.when(mine & first)
        def _():
            wait_gather(r)

            @pl.when(t >= 2)
            def _():
                wait_scatter(behind)

        @pl.when(mine & interior)
        def _():
            for j in range(te):
                gather(t + 2, ahead, j, priority=j % 2)
                scatter(t - 1, ahead, j, priority=j % 2)
            ffn(r)

        @pl.when(mine & jnp.logical_not(interior))
        def _():
            @pl.when(first & (t == 0))
            def _():
                issue_loop(gather, 2, ahead)

            @pl.when(first & (t >= n_tiles - 2))
            def _():
                issue_loop(scatter, t - 1, ahead)

            ffn(r)

    @pl.when(p == pl.num_programs(0) - 1)
    def _():
        last = n_tiles - 1
        wait_scatter((last - 1) % 3)
        issue_loop(scatter, last, last % 3)
        wait_scatter(last % 3)


def _experts(x1_rows, tables, w1, w3, w2):
    n, _, d = x1_rows.shape
    de = w1.shape[2]
    te = EXPERT_TILE
    n_tiles = 2 * n // te
    assert n_tiles >= 4
    max_pairs = n_tiles + N_EXPERTS - 1
    hbm = pl.BlockSpec(memory_space=pl.ANY)
    return pl.pallas_call(
        functools.partial(_expert_body, n=n, n_tiles=n_tiles),
        grid_spec=pltpu.PrefetchScalarGridSpec(
            num_scalar_prefetch=9, grid=(max_pairs,),
            in_specs=[hbm, hbm, hbm, hbm],
            out_specs=hbm,
            scratch_shapes=[pltpu.VMEM((d, de), BF16), pltpu.VMEM((d, de), BF16), pltpu.VMEM((de, d), BF16),
                            pltpu.VMEM((2, d, de), F32), pltpu.VMEM((2, d, de), F32), pltpu.VMEM((2, de, d), F32),
                            *[pltpu.VMEM((te, d), F32)] * 6,
                            pltpu.SMEM((2 * n,), jnp.int32),
                            pltpu.SemaphoreType.DMA((3,)), pltpu.SemaphoreType.DMA(()),
                            pltpu.SemaphoreType.DMA((2,))]),
        out_shape=jax.ShapeDtypeStruct((2 * n, 1, d), F32),
        compiler_params=_params(("arbitrary",)), name="experts",
    )(*tables, x1_rows, w1, w3, w2)


def _combine_body(x_ref, ya_ref, yb_ref, cw_ref, g_ref, b_ref, o_ref):
    y = cw_ref[:, 0:1] * ya_ref[:, 0, :] + cw_ref[:, 1:2] * yb_ref[:, 0, :]
    o_ref[...] = _layer_norm(DN_ALPHA * x_ref[...] + y, g_ref[...], b_ref[...])


def _combine(y2, x1, cw_col, ln_g, ln_b):
    n, d = x1.shape
    tm = min(512, n)
    nt = n // tm
    tile = pl.BlockSpec((tm, d), lambda i: (i, 0))
    row = pl.BlockSpec((1, d), lambda i: (0, 0))
    return pl.pallas_call(
        _combine_body, grid=(nt,),
        in_specs=[tile, pl.BlockSpec((tm, 1, d), lambda i: (i, 0, 0)),
                  pl.BlockSpec((tm, 1, d), lambda i: (i + nt, 0, 0)),
                  pl.BlockSpec((tm, 2), lambda i: (i, 0)), row, row],
        out_specs=tile,
        out_shape=jax.ShapeDtypeStruct((n, d), F32),
        compiler_params=_params(("arbitrary",)), name="combine_ln2",
    )(x1, y2, y2, cw_col, ln_g, ln_b)


def _retention_consts():
    h, c = RET_HEADS, CHUNK
    log_gamma = jnp.log1p(-jnp.exp2(-5.0 - jnp.arange(h, dtype=F32)))
    idx = jnp.arange(c, dtype=F32)
    rel = idx[:, None] - idx[None, :]
    mask = jnp.where(rel >= 0, jnp.exp(log_gamma[:, None, None] * jnp.maximum(rel, 0.0)), 0.0)
    xi = jnp.exp(log_gamma[:, None] * (idx + 1.0))
    zeta = jnp.exp(log_gamma[:, None] * (c - 1.0 - idx))
    chunk_decay = jnp.exp(log_gamma * c)
    xi_b = jnp.broadcast_to(xi[:, :, None], (h, c, RET_QK_DIM))
    zeta_b = jnp.broadcast_to(zeta[:, :, None], (h, c, RET_QK_DIM))
    cd_b = jnp.broadcast_to(chunk_decay[:, None, None], (h, 1, RET_V_DIM))
    return mask, xi_b, zeta_b, cd_b


def _rope_consts():
    half = RET_QK_DIM // 2
    freq = ROPE_THETA ** (-jnp.arange(half, dtype=F32) / half)
    freq2 = jnp.concatenate([freq, freq])[None, :]
    sign2 = jnp.concatenate([-jnp.ones((half,), F32), jnp.ones((half,), F32)])[None, :]
    return freq2, sign2


def kernel(x, positions, w_in, b_gate, ret_gn_g, sgu_ln_g, sgu_ln_b, sgu_w, sgu_b, w_proj_ret, w_proj_sgu,
           w_out, ln1_g, ln1_b, w_group, b_group, w_er, b_er, w1, w3, w2, ln2_g, ln2_b):
    batch, seq, d = x.shape
    n = batch * seq
    qk_w, v_w = RET_HEADS * RET_QK_DIM, RET_HEADS * RET_V_DIM
    assert d == v_w == SGU_GROUPS * CHUNK and seq % CHUNK == 0
    ret_cols = 2 * qk_w + 2 * v_w
    max_pairs = (2 * n) // EXPERT_TILE + N_EXPERTS - 1

    freq2, sign2 = _rope_consts()
    pos_col = positions.reshape(n, 1)
    ret_consts = _retention_consts()

    xc = x.reshape(n, d)
    for l in range(w_in.shape[0]):
        wa = _regroup_retention_weights(w_in[l])
        y_a, xb = _retention_branch(xc, pos_col, freq2, sign2, wa, w_proj_ret[l].astype(BF16), ret_consts,
                                    ret_gn_g[l][None, :], batch, seq)
        gate_bias = jnp.concatenate([jnp.zeros((2 * d,), F32), b_gate[l].reshape(-1)])[None, :]
        h = _in_proj(xb, w_in[l], ret_cols, gate_bias)
        bias_b = jnp.broadcast_to(sgu_b[l][:, :, None], (SGU_GROUPS, CHUNK, CHUNK))
        merged = _sgu_merge(h, y_a, sgu_ln_g[l][None, :], sgu_ln_b[l][None, :], sgu_w[l], bias_b,
                            w_proj_sgu[l].astype(BF16))

        w_r = jnp.concatenate([w_er[l], w_group[l]], axis=1).T
        w_r = jnp.pad(w_r, ((0, ROUTER_ROWS - w_r.shape[0]), (0, 0)))
        w_hi = w_r.astype(BF16)
        w_lo = (w_r - w_hi.astype(F32)).astype(BF16)
        bias = jnp.pad(jnp.concatenate([b_er[l], b_group[l]]), (0, ROUTER_ROWS - N_EXPERTS - N_GROUPS))[:, None]
        x1, x1_rows, ids, cw, cnt = _out_proj_router(merged, xc, w_out[l].astype(BF16), ln1_g[l][None, :],
                                                     ln1_b[l][None, :], w_hi, w_lo, bias)

        counts = cnt[:, 0].astype(jnp.int32)
        dest1, dest2, p_tile, p_exp, p_lo, p_hi, n_pairs, p_next, p_slot = _routing_tables(ids, counts, n, max_pairs)
        y2 = _experts(x1_rows, (p_tile, p_exp, p_lo, p_hi, n_pairs, p_next, p_slot, dest1, dest2),
                      w1[l], w3[l], w2[l])
        xc = _combine(y2, x1, cw.T, ln2_g[l][None, :], ln2_b[l][None, :])
    return xc.reshape(batch, seq, d)
```

```python
import functools

import numpy as np
import jax
import jax.numpy as jnp
from jax import lax
from jax.experimental import pallas as pl
from jax.experimental.pallas import tpu as pltpu

F32 = jnp.float32
BF16 = jnp.bfloat16

RET_HEADS = 8
RET_QK_DIM = 128
RET_V_DIM = 256
CHUNK = 128
ROPE_THETA = 10000.0
SGU_GROUPS = 16
N_GROUPS = 4
EXPERTS_PER_GROUP = 8
N_EXPERTS = N_GROUPS * EXPERTS_PER_GROUP
ROUTER_ROWS = 40
LN_EPS = 1e-5
DEPTH = 1
DN_ALPHA = (2 * DEPTH) ** 0.25
SQRT_HALF = np.sqrt(0.5).astype(np.float32)

LANES = 128
VMEM_LIMIT = 52 * 1024 * 1024
EXPERT_TILE = 256


def _params(sem, vmem=VMEM_LIMIT, flags=None):
    return pltpu.CompilerParams(dimension_semantics=sem, vmem_limit_bytes=vmem, flags=flags)


def _dot(a, b):
    return jnp.dot(a, b, preferred_element_type=F32)


def _dot_nt(a, b):
    return lax.dot_general(a, b, (((1,), (1,)), ((), ())), preferred_element_type=F32)


def _dot_tn(a, b):
    return lax.dot_general(a, b, (((0,), (0,)), ((), ())), preferred_element_type=F32)


def _layer_norm(r, g, b):
    mu = jnp.mean(r, axis=-1, keepdims=True)
    d = r - mu
    var = jnp.mean(d * d, axis=-1, keepdims=True)
    return d * lax.rsqrt(var + LN_EPS) * g + b


def _gelu(x):
    return 0.5 * x * (1.0 + lax.erf(x * SQRT_HALF))


def _inproj_body(x_ref, w_ref, b_ref, o_ref, wb_ref, *, gelu_blocks):
    @pl.when(pl.program_id(1) == 0)
    def _():
        wb_ref[...] = w_ref[...].astype(BF16)

    @pl.when(pl.program_id(0) < gelu_blocks)
    def _():
        o_ref[...] = _gelu(_dot(x_ref[...], wb_ref[...])).astype(BF16)

    @pl.when(pl.program_id(0) >= gelu_blocks)
    def _():
        o_ref[...] = jax.nn.sigmoid(_dot(x_ref[...], wb_ref[...]) + b_ref[...]).astype(BF16)


def _in_proj(xb, w, col0, gate_bias):
    n, d = xb.shape
    width = w.shape[1] - col0
    tm, tn = min(1024, n), 1024
    assert col0 % tn == 0 and width % (2 * tn) == 0
    return pl.pallas_call(
        functools.partial(_inproj_body, gelu_blocks=width // (2 * tn)),
        grid=(width // tn, n // tm),
        in_specs=[pl.BlockSpec((tm, d), lambda j, i: (i, 0)),
                  pl.BlockSpec((d, tn), lambda j, i: (0, j + col0 // tn)),
                  pl.BlockSpec((1, tn), lambda j, i: (0, j))],
        out_specs=pl.BlockSpec((tm, tn), lambda j, i: (i, j)),
        out_shape=jax.ShapeDtypeStruct((n, width), BF16),
        scratch_shapes=[pltpu.VMEM((d, tn), BF16)],
        compiler_params=_params(("arbitrary", "arbitrary")), name="in_proj",
    )(xb, w, gate_bias)


HEAD_COLS = 2 * RET_QK_DIM + 2 * RET_V_DIM


def _regroup_body(q_ref, k_ref, v_ref, g_ref, o_ref):
    dk, dv = RET_QK_DIM, RET_V_DIM
    o_ref[0, :, 0:dk] = q_ref[...].astype(BF16)
    o_ref[0, :, dk:2 * dk] = k_ref[...].astype(BF16)
    o_ref[0, :, 2 * dk:2 * dk + dv] = v_ref[...].astype(BF16)
    o_ref[0, :, 2 * dk + dv:] = g_ref[...].astype(BF16)


def _regroup_retention_weights(w):
    d = w.shape[0]
    k_blk = RET_HEADS
    v_blk = 2 * RET_HEADS * RET_QK_DIM // RET_V_DIM
    return pl.pallas_call(
        _regroup_body, grid=(RET_HEADS,),
        in_specs=[pl.BlockSpec((d, RET_QK_DIM), lambda h: (0, h)),
                  pl.BlockSpec((d, RET_QK_DIM), lambda h: (0, k_blk + h)),
                  pl.BlockSpec((d, RET_V_DIM), lambda h: (0, v_blk + h)),
                  pl.BlockSpec((d, RET_V_DIM), lambda h: (0, v_blk + RET_HEADS + h))],
        out_specs=pl.BlockSpec((1, d, HEAD_COLS), lambda h: (h, 0, 0)),
        out_shape=jax.ShapeDtypeStruct((RET_HEADS, d, HEAD_COLS), BF16),
        compiler_params=_params(("arbitrary",)), name="regroup_retention_weights",
    )(w, w, w, w)


def _retention_body(x_ref, pos_ref, freq_ref, sign_ref, wa_ref, wp_ref, mask_ref, xi_ref, zeta_ref, cd_ref,
                    gain_ref, o_ref, xb_ref, state_ref, yg_ref, cos_ref, sin_ref, *, n_chunks):
    @pl.when(pl.program_id(1) == 0)
    def _():
        state_ref[...] = jnp.zeros_like(state_ref)

    ang = pos_ref[...].astype(F32) * freq_ref[...]
    cos_ref[...] = jnp.cos(ang)
    sin_ref[...] = jnp.sin(ang) * sign_ref[...]

    scale = RET_QK_DIM ** -0.5
    half = RET_QK_DIM // 2
    dk, dv = RET_QK_DIM, RET_V_DIM
    xb = x_ref[...].astype(BF16)
    xb_ref[...] = xb
    ahead = 1
    projs = [_dot(xb, wa_ref[h]) for h in range(ahead)]
    for hd in range(RET_HEADS):
        proj = projs[hd]
        if hd + ahead < RET_HEADS:
            projs.append(_dot(xb, wa_ref[hd + ahead]))
        gain = gain_ref[:, hd * dv:(hd + 1) * dv]
        for c in range(n_chunks):
            lo, hi = c * CHUNK, (c + 1) * CHUNK
            cos, sin = cos_ref[lo:hi, :], sin_ref[lo:hi, :]
            q, k = proj[lo:hi, 0:dk], proj[lo:hi, dk:2 * dk]
            vb = proj[lo:hi, 2 * dk:2 * dk + dv].astype(BF16)
            g = proj[lo:hi, 2 * dk + dv:]
            qr = q * cos + pltpu.roll(q, half, 1) * sin
            kr = (k * cos + pltpu.roll(k, half, 1) * sin) * scale
            scores = _dot_nt(qr.astype(BF16), kr.astype(BF16)) * mask_ref[hd]
            state = state_ref[hd]
            lhs = jnp.concatenate([scores.astype(BF16), (qr * xi_ref[hd]).astype(BF16)], axis=1)
            rhs = jnp.concatenate([vb, state.astype(BF16)], axis=0)
            out = _dot(lhs, rhs)
            state_ref[hd] = state * cd_ref[hd] + _dot_tn((kr * zeta_ref[hd]).astype(BF16), vb)
            mu = jnp.mean(out, axis=-1, keepdims=True)
            d = out - mu
            var = jnp.mean(d * d, axis=-1, keepdims=True)
            yn = d * lax.rsqrt(var + LN_EPS) * gain
            yg_ref[lo:hi, hd * dv:(hd + 1) * dv] = (g * jax.nn.sigmoid(g) * yn).astype(BF16)
    o_ref[...] = _dot(yg_ref[...], wp_ref[...]).astype(BF16)


def _retention_branch(x2, pos_col, freq2, sign2, wa_bf, wp_bf, consts, gain, batch, seq):
    n, d = x2.shape
    tr = min(256, seq)
    nr = seq // tr
    mask, xi_b, zeta_b, cd_b = consts
    vw = RET_HEADS * RET_V_DIM

    def once(shape):
        return pl.BlockSpec(shape, lambda b, r: (0,) * len(shape), pipeline_mode=pl.Buffered(1))

    rows = lambda w: pl.BlockSpec((tr, w), lambda b, r: (b * nr + r, 0))
    return pl.pallas_call(
        functools.partial(_retention_body, n_chunks=tr // CHUNK),
        grid=(batch, nr),
        in_specs=[rows(d), rows(1), once((1, LANES)), once((1, LANES)),
                  once((RET_HEADS, d, HEAD_COLS)), once((vw, d)),
                  once((RET_HEADS, CHUNK, CHUNK)), once((RET_HEADS, CHUNK, RET_QK_DIM)),
                  once((RET_HEADS, CHUNK, RET_QK_DIM)), once((RET_HEADS, 1, RET_V_DIM)), once((1, vw))],
        out_specs=[rows(d), rows(d)],
        out_shape=[jax.ShapeDtypeStruct((n, d), BF16), jax.ShapeDtypeStruct((n, d), BF16)],
        scratch_shapes=[pltpu.VMEM((RET_HEADS, RET_QK_DIM, RET_V_DIM), F32), pltpu.VMEM((tr, vw), BF16),
                        pltpu.VMEM((tr, LANES), F32), pltpu.VMEM((tr, LANES), F32)],
        compiler_params=_params(("arbitrary", "arbitrary")), name="retention_branch",
    )(x2, pos_col, freq2, sign2, wa_bf, wp_bf, mask, xi_b, zeta_b, cd_b, gain)


def _sgu_merge_body(u_ref, v_ref, sa_ref, sb_ref, ya_ref, lng_ref, lnb_ref, ws_ref, bs_ref, wp_ref,
                    o_ref, wtril_ref, ys_ref, wpb_ref, *, n_chunks):
    @pl.when(pl.program_id(0) == 0)
    def _():
        r = lax.broadcasted_iota(jnp.int32, (CHUNK, CHUNK), 0)
        c = lax.broadcasted_iota(jnp.int32, (CHUNK, CHUNK), 1)
        for g in range(SGU_GROUPS):
            wtril_ref[g] = jnp.where(r >= c, ws_ref[g], 0.0).astype(BF16)
        wpb_ref[...] = wp_ref[...].astype(BF16)

    vn = _layer_norm(v_ref[...].astype(F32), lng_ref[...], lnb_ref[...]).astype(BF16)
    for g in range(SGU_GROUPS):
        cols = pl.ds(g * CHUNK, CHUNK)
        lo = g * CHUNK
        rhs = jnp.concatenate([vn[c * CHUNK:(c + 1) * CHUNK, lo:lo + CHUNK] for c in range(n_chunks)], axis=1)
        mix = _dot(wtril_ref[g], rhs)
        bias = bs_ref[g]
        for c in range(n_chunks):
            rows = pl.ds(c * CHUNK, CHUNK)
            u = u_ref[rows, cols].astype(F32)
            ys_ref[rows, cols] = (u * (mix[:, c * CHUNK:(c + 1) * CHUNK] + bias)).astype(BF16)

    yb = _dot(ys_ref[...], wpb_ref[...])
    o_ref[...] = (sa_ref[...].astype(F32) * ya_ref[...].astype(F32) + sb_ref[...].astype(F32) * yb).astype(BF16)


def _sgu_merge(h, y_a, ln_g, ln_b, w_s, b_s_b, wp):
    n, d = y_a.shape
    ts = min(256, n)
    tile = lambda blk: pl.BlockSpec((ts, d), lambda i: (i, blk))
    row = pl.BlockSpec((1, d), lambda i: (0, 0))
    full3 = pl.BlockSpec((SGU_GROUPS, CHUNK, CHUNK), lambda i: (0, 0, 0))
    return pl.pallas_call(
        functools.partial(_sgu_merge_body, n_chunks=ts // CHUNK),
        grid=(n // ts,),
        in_specs=[tile(0), tile(1), tile(2), tile(3), tile(0), row, row, full3, full3,
                  pl.BlockSpec((d, d), lambda i: (0, 0), pipeline_mode=pl.Buffered(1))],
        out_specs=tile(0),
        out_shape=jax.ShapeDtypeStruct((n, d), BF16),
        scratch_shapes=[pltpu.VMEM((SGU_GROUPS, CHUNK, CHUNK), BF16), pltpu.VMEM((ts, d), BF16),
                        pltpu.VMEM((d, d), BF16)],
        compiler_params=_params(("arbitrary",)), name="sgu_merge",
    )(h, h, h, h, y_a, ln_g, ln_b, w_s, b_s_b, wp)


def _outproj_router_body(m_ref, x_ref, w_ref, g_ref, b_ref, whi_ref, wlo_ref, rb_ref,
                         o_ref, orow_hbm, ids_ref, cw_ref, cnt_ref, carry_ref, tri_ref, wb_ref, rsem, *, tm):
    step = pl.program_id(0)

    @pl.when(step == 0)
    def _():
        carry_ref[...] = jnp.zeros_like(carry_ref)
        r = lax.broadcasted_iota(jnp.int32, (tm, tm), 0)
        c = lax.broadcasted_iota(jnp.int32, (tm, tm), 1)
        tri_ref[...] = jnp.where(r < c, 1.0, 0.0).astype(BF16)
        wb_ref[...] = w_ref[...].astype(BF16)

    r = DN_ALPHA * x_ref[...] + _dot(m_ref[...], wb_ref[...])
    x = _layer_norm(r, g_ref[...], b_ref[...])
    o_ref[...] = x

    row_copy = _row_copy(o_ref, orow_hbm.at[pl.ds(pl.multiple_of(step * tm, tm), tm), 0], rsem)
    row_copy.start()

    hi = x.astype(BF16)
    lo = (x - hi.astype(F32)).astype(BF16)
    whi = whi_ref[...]
    logits = _dot_nt(whi, hi) + _dot_nt(whi, lo) + _dot_nt(wlo_ref[...], hi) + rb_ref[...]
    el = logits[0:N_EXPERTS, :]
    gl = logits[N_EXPERTS:N_EXPERTS + N_GROUPS, :]

    gi = lax.broadcasted_iota(jnp.int32, gl.shape, 0)
    gmax = jnp.max(gl, axis=0, keepdims=True)
    gidx = jnp.min(jnp.where(gl == gmax, gi, N_GROUPS), axis=0, keepdims=True)
    p_group = 1.0 / jnp.sum(jnp.exp(gl - gmax), axis=0, keepdims=True)

    ei = lax.broadcasted_iota(jnp.int32, el.shape, 0)
    first = gidx * EXPERTS_PER_GROUP
    in_group = (ei >= first) & (ei < first + EXPERTS_PER_GROUP)
    m1 = jnp.where(in_group, el, -jnp.inf)
    v1 = jnp.max(m1, axis=0, keepdims=True)
    i1 = jnp.min(jnp.where(in_group & (m1 == v1), ei, N_EXPERTS), axis=0, keepdims=True)
    rest = in_group & (ei != i1)
    m2 = jnp.where(rest, el, -jnp.inf)
    v2 = jnp.max(m2, axis=0, keepdims=True)
    i2 = jnp.min(jnp.where(rest & (m2 == v2), ei, N_EXPERTS), axis=0, keepdims=True)
    t = jnp.exp(v2 - v1)
    p1 = 1.0 / (1.0 + t)
    p2 = t * p1

    sel1, sel2 = ei == i1, ei == i2
    onehot = jnp.where(sel1 | sel2, 1.0, 0.0)
    before = _dot(onehot.astype(BF16), tri_ref[...]) + carry_ref[:, 0:1]
    r1 = jnp.sum(jnp.where(sel1, before, 0.0), axis=0, keepdims=True)
    r2 = jnp.sum(jnp.where(sel2, before, 0.0), axis=0, keepdims=True)
    carry_ref[...] = carry_ref[...] + jnp.sum(onehot, axis=1, keepdims=True)

    ids_ref[0:1, :] = i1
    ids_ref[1:2, :] = i2
    ids_ref[2:3, :] = r1.astype(jnp.int32)
    ids_ref[3:4, :] = r2.astype(jnp.int32)
    cw_ref[0:1, :] = p_group * p1
    cw_ref[1:2, :] = p_group * p2
    cnt_ref[...] = carry_ref[...]
    row_copy.wait()


def _out_proj_router(merged, x2, w_out, ln_g, ln_b, w_hi, w_lo, bias_col):
    n, d = x2.shape
    tm = min(512, n)
    tile = pl.BlockSpec((tm, d), lambda i: (i, 0))
    row = pl.BlockSpec((1, d), lambda i: (0, 0))
    full = pl.BlockSpec((ROUTER_ROWS, d), lambda i: (0, 0))
    return pl.pallas_call(
        functools.partial(_outproj_router_body, tm=tm), grid=(n // tm,),
        in_specs=[tile, tile, pl.BlockSpec((d, d), lambda i: (0, 0), pipeline_mode=pl.Buffered(1)), row, row,
                  full, full, pl.BlockSpec((ROUTER_ROWS, 1), lambda i: (0, 0))],
        out_specs=[tile, pl.BlockSpec(memory_space=pl.ANY),
                   pl.BlockSpec((4, tm), lambda i: (0, i)),
                   pl.BlockSpec((2, tm), lambda i: (0, i)),
                   pl.BlockSpec((N_EXPERTS, LANES), lambda i: (0, 0))],
        out_shape=[jax.ShapeDtypeStruct((n, d), F32), jax.ShapeDtypeStruct((n, 1, d), F32),
                   jax.ShapeDtypeStruct((4, n), jnp.int32),
                   jax.ShapeDtypeStruct((2, n), F32),
                   jax.ShapeDtypeStruct((N_EXPERTS, LANES), F32)],
        scratch_shapes=[pltpu.VMEM((N_EXPERTS, LANES), F32), pltpu.VMEM((tm, tm), BF16),
                        pltpu.VMEM((d, d), BF16), pltpu.SemaphoreType.DMA(())],
        compiler_params=_params(("arbitrary",)), name="out_proj_ln1_router",
    )(merged, x2, w_out, ln_g, ln_b, w_hi, w_lo, bias_col)


def _routing_tables(ids, counts, n, max_pairs):
    te = EXPERT_TILE
    n_tiles = 2 * n // te
    i32 = jnp.int32
    row_end = jnp.cumsum(counts)
    row_start = row_end - counts
    dest1 = row_start[ids[0]] + ids[2]
    dest2 = row_start[ids[1]] + ids[3]
    ne_cum = jnp.cumsum((counts > 0).astype(i32))
    tile_lo = jnp.arange(n_tiles, dtype=i32) * te
    first_e = jnp.sum((row_end[None, :] <= tile_lo[:, None]).astype(i32), axis=1)
    last_e = jnp.sum((row_end[None, :] <= (tile_lo + te - 1)[:, None]).astype(i32), axis=1)
    pairs_t = ne_cum[last_e] - ne_cum[first_e] + 1
    pair_end = jnp.cumsum(pairs_t)
    pair_start = pair_end - pairs_t
    n_pairs = pair_end[-1]
    p = jnp.minimum(jnp.arange(max_pairs, dtype=i32), n_pairs - 1)
    p_tile = jnp.sum((pair_end[None, :] <= p[:, None]).astype(i32), axis=1)
    rank = ne_cum[first_e[p_tile]] - 1 + (p - pair_start[p_tile])
    p_exp = jnp.sum((ne_cum[None, :] <= rank[:, None]).astype(i32), axis=1)
    p_lo = jnp.maximum(row_start[p_exp], p_tile * te) - p_tile * te
    p_hi = jnp.minimum(row_end[p_exp], (p_tile + 1) * te) - p_tile * te
    p_next = jnp.where(rank + 1 < ne_cum[-1],
                       jnp.sum((ne_cum[None, :] <= (rank + 1)[:, None]).astype(i32), axis=1), -1)
    p_slot = rank & 1
    return (dest1, dest2, p_tile.astype(i32), p_exp.astype(i32), p_lo.astype(i32), p_hi.astype(i32),
            n_pairs.reshape(1).astype(i32), p_next.astype(i32), p_slot.astype(i32))


def _row_copy(src, dst, sem):
    return pltpu.make_async_copy(src, dst, sem)


def _expert_body(ptile_ref, pexp_ref, plo_ref, phi_ref, np_ref, pnext_ref, pslot_ref, d1_ref, d2_ref,
                 x1_hbm, w1_hbm, w3_hbm, w2_hbm, y2_hbm,
                 w1b, w3b, w2b, wf1, wf3, wf2, xbuf0, xbuf1, xbuf2, ybuf0, ybuf1, ybuf2, inv_s, gsem, ssem, wsem,
                 *, n, n_tiles):
    te = EXPERT_TILE
    p = pl.program_id(0)
    valid = p < np_ref[0]
    t = ptile_ref[p]
    pm1 = jnp.maximum(p - 1, 0)
    first = (p == 0) | (ptile_ref[pm1] != t)
    new_expert = (p == 0) | (pexp_ref[pm1] != pexp_ref[p])
    lo, hi = plo_ref[p], phi_ref[p]
    xbufs, ybufs = (xbuf0, xbuf1, xbuf2), (ybuf0, ybuf1, ybuf2)

    def gather(tile, slot, j, priority=0):
        v = inv_s[tile * te + j]
        tok = jnp.where(v >= n, v - n, v)
        _row_copy(x1_hbm.at[tok], xbufs[slot].at[pl.ds(j, 1)], gsem.at[slot]).start(priority=priority)

    def scatter(tile, slot, j, priority=0):
        _row_copy(ybufs[slot].at[pl.ds(j, 1)], y2_hbm.at[inv_s[tile * te + j]], ssem).start(priority=priority)

    def wait_gather(slot):
        _row_copy(x1_hbm.at[pl.ds(0, te), 0], xbufs[slot], gsem.at[slot]).wait()

    def wait_scatter(slot):
        _row_copy(ybufs[slot], y2_hbm.at[pl.ds(0, te), 0], ssem).wait()

    def issue_loop(fn, tile, slot):
        def one(j, c):
            fn(tile, slot, j)
            return c
        lax.fori_loop(0, te, one, 0, unroll=8)

    def weight_copies(e, slot):
        return [_row_copy(w1_hbm.at[e], wf1.at[slot], wsem.at[slot]),
                _row_copy(w3_hbm.at[e], wf3.at[slot], wsem.at[slot]),
                _row_copy(w2_hbm.at[e], wf2.at[slot], wsem.at[slot])]

    @pl.when(p == 0)
    def _():
        for c in weight_copies(pexp_ref[0], pslot_ref[0]):
            c.start()

        def invert(tk, c):
            inv_s[d1_ref[tk]] = tk
            inv_s[d2_ref[tk]] = tk + n
            return c
        lax.fori_loop(0, n, invert, 0, unroll=8)
        for yb in ybufs:
            yb[...] = jnp.zeros_like(yb)
        issue_loop(gather, 0, 0)
        issue_loop(gather, 1, 1)

    @pl.when(valid & new_expert)
    def _():
        slot = pslot_ref[p]
        for c in weight_copies(pexp_ref[p], slot):
            c.wait()
        w1b[...] = wf1[slot].astype(BF16)
        w3b[...] = wf3[slot].astype(BF16)
        w2b[...] = wf2[slot].astype(BF16)

        @pl.when(pnext_ref[p] >= 0)
        def _():
            for c in weight_copies(pnext_ref[p], 1 - slot):
                c.start()

    def ffn(slot):
        xb = xbufs[slot][...].astype(BF16)
        a = _dot(xb, w1b[...])
        b = _dot(xb, w3b[...])
        y = _dot((a * jax.nn.sigmoid(a) * b).astype(BF16), w2b[...])
        rows = lax.broadcasted_iota(jnp.int32, (te, 1), 0)
        ybufs[slot][...] = jnp.where((rows >= lo) & (rows < hi), y, ybufs[slot][...])

    for r in range(3):
        ahead, behind = (r + 2) % 3, (r + 1) % 3
        mine = valid & (lax.rem(t, 3) == r)
        interior = first & (t >= 1) & (t < n_tiles - 2)

        @pl.when(mine & first)
        def _():
            wait_gather(r)

            @pl.when(t >= 2)
            def _():
                wait_scatter(behind)

        @pl.when(mine & interior)
        def _():
            for j in range(te):
                gather(t + 2, ahead, j, priority=j % 2)
                scatter(t - 1, ahead, j, priority=j % 2)
            ffn(r)

        @pl.when(mine & jnp.logical_not(interior))
        def _():
            @pl.when(first & (t == 0))
            def _():
                issue_loop(gather, 2, ahead)

            @pl.when(first & (t >= n_tiles - 2))
            def _():
                issue_loop(scatter, t - 1, ahead)

            ffn(r)

    @pl.when(p == pl.num_programs(0) - 1)
    def _():
        last = n_tiles - 1
        wait_scatter((last - 1) % 3)
        issue_loop(scatter, last, last % 3)
        wait_scatter(last % 3)


def _experts(x1_rows, tables, w1, w3, w2):
    n, _, d = x1_rows.shape
    de = w1.shape[2]
    te = EXPERT_TILE
    n_tiles = 2 * n // te
    assert n_tiles >= 4
    max_pairs = n_tiles + N_EXPERTS - 1
    hbm = pl.BlockSpec(memory_space=pl.ANY)
    return pl.pallas_call(
        functools.partial(_expert_body, n=n, n_tiles=n_tiles),
        grid_spec=pltpu.PrefetchScalarGridSpec(
            num_scalar_prefetch=9, grid=(max_pairs,),
            in_specs=[hbm, hbm, hbm, hbm],
            out_specs=hbm,
            scratch_shapes=[pltpu.VMEM((d, de), BF16), pltpu.VMEM((d, de), BF16), pltpu.VMEM((de, d), BF16),
                            pltpu.VMEM((2, d, de), F32), pltpu.VMEM((2, d, de), F32), pltpu.VMEM((2, de, d), F32),
                            *[pltpu.VMEM((te, d), F32)] * 6,
                            pltpu.SMEM((2 * n,), jnp.int32),
                            pltpu.SemaphoreType.DMA((3,)), pltpu.SemaphoreType.DMA(()),
                            pltpu.SemaphoreType.DMA((2,))]),
        out_shape=jax.ShapeDtypeStruct((2 * n, 1, d), F32),
        compiler_params=_params(("arbitrary",)), name="experts",
    )(*tables, x1_rows, w1, w3, w2)


def _combine_body(x_ref, ya_ref, yb_ref, cw_ref, g_ref, b_ref, o_ref):
    y = cw_ref[:, 0:1] * ya_ref[:, 0, :] + cw_ref[:, 1:2] * yb_ref[:, 0, :]
    o_ref[...] = _layer_norm(DN_ALPHA * x_ref[...] + y, g_ref[...], b_ref[...])


def _combine(y2, x1, cw_col, ln_g, ln_b):
    n, d = x1.shape
    tm = min(512, n)
    nt = n // tm
    tile = pl.BlockSpec((tm, d), lambda i: (i, 0))
    row = pl.BlockSpec((1, d), lambda i: (0, 0))
    return pl.pallas_call(
        _combine_body, grid=(nt,),
        in_specs=[tile, pl.BlockSpec((tm, 1, d), lambda i: (i, 0, 0)),
                  pl.BlockSpec((tm, 1, d), lambda i: (i + nt, 0, 0)),
                  pl.BlockSpec((tm, 2), lambda i: (i, 0)), row, row],
        out_specs=tile,
        out_shape=jax.ShapeDtypeStruct((n, d), F32),
        compiler_params=_params(("arbitrary",)), name="combine_ln2",
    )(x1, y2, y2, cw_col, ln_g, ln_b)


def _retention_consts():
    h, c = RET_HEADS, CHUNK
    log_gamma = jnp.log1p(-jnp.exp2(-5.0 - jnp.arange(h, dtype=F32)))
    idx = jnp.arange(c, dtype=F32)
    rel = idx[:, None] - idx[None, :]
    mask = jnp.where(rel >= 0, jnp.exp(log_gamma[:, None, None] * jnp.maximum(rel, 0.0)), 0.0)
    xi = jnp.exp(log_gamma[:, None] * (idx + 1.0))
    zeta = jnp.exp(log_gamma[:, None] * (c - 1.0 - idx))
    chunk_decay = jnp.exp(log_gamma * c)
    xi_b = jnp.broadcast_to(xi[:, :, None], (h, c, RET_QK_DIM))
    zeta_b = jnp.broadcast_to(zeta[:, :, None], (h, c, RET_QK_DIM))
    cd_b = jnp.broadcast_to(chunk_decay[:, None, None], (h, 1, RET_V_DIM))
    return mask, xi_b, zeta_b, cd_b


def _rope_consts():
    half = RET_QK_DIM // 2
    freq = ROPE_THETA ** (-jnp.arange(half, dtype=F32) / half)
    freq2 = jnp.concatenate([freq, freq])[None, :]
    sign2 = jnp.concatenate([-jnp.ones((half,), F32), jnp.ones((half,), F32)])[None, :]
    return freq2, sign2


def kernel(x, positions, w_in, b_gate, ret_gn_g, sgu_ln_g, sgu_ln_b, sgu_w, sgu_b, w_proj_ret, w_proj_sgu,
           w_out, ln1_g, ln1_b, w_group, b_group, w_er, b_er, w1, w3, w2, ln2_g, ln2_b):
    batch, seq, d = x.shape
    n = batch * seq
    qk_w, v_w = RET_HEADS * RET_QK_DIM, RET_HEADS * RET_V_DIM
    assert d == v_w == SGU_GROUPS * CHUNK and seq % CHUNK == 0
    ret_cols = 2 * qk_w + 2 * v_w
    max_pairs = (2 * n) // EXPERT_TILE + N_EXPERTS - 1

    freq2, sign2 = _rope_consts()
    pos_col = positions.reshape(n, 1)
    ret_consts = _retention_consts()

    xc = x.reshape(n, d)
    for l in range(w_in.shape[0]):
        wa = _regroup_retention_weights(w_in[l])
        y_a, xb = _retention_branch(xc, pos_col, freq2, sign2, wa, w_proj_ret[l].astype(BF16), ret_consts,
                                    ret_gn_g[l][None, :], batch, seq)
        gate_bias = jnp.concatenate([jnp.zeros((2 * d,), F32), b_gate[l].reshape(-1)])[None, :]
        h = _in_proj(xb, w_in[l], ret_cols, gate_bias)
        bias_b = jnp.broadcast_to(sgu_b[l][:, :, None], (SGU_GROUPS, CHUNK, CHUNK))
        merged = _sgu_merge(h, y_a, sgu_ln_g[l][None, :], sgu_ln_b[l][None, :], sgu_w[l], bias_b, w_proj_sgu[l])

        w_r = jnp.concatenate([w_er[l], w_group[l]], axis=1).T
        w_r = jnp.pad(w_r, ((0, ROUTER_ROWS - w_r.shape[0]), (0, 0)))
        w_hi = w_r.astype(BF16)
        w_lo = (w_r - w_hi.astype(F32)).astype(BF16)
        bias = jnp.pad(jnp.concatenate([b_er[l], b_group[l]]), (0, ROUTER_ROWS - N_EXPERTS - N_GROUPS))[:, None]
        x1, x1_rows, ids, cw, cnt = _out_proj_router(merged, xc, w_out[l], ln1_g[l][None, :],
                                                     ln1_b[l][None, :], w_hi, w_lo, bias)

        counts = cnt[:, 0].astype(jnp.int32)
        dest1, dest2, p_tile, p_exp, p_lo, p_hi, n_pairs, p_next, p_slot = _routing_tables(ids, counts, n, max_pairs)
        y2 = _experts(x1_rows, (p_tile, p_exp, p_lo, p_hi, n_pairs, p_next, p_slot, dest1, dest2),
                      w1[l], w3[l], w2[l])
        xc = _combine(y2, x1, cw.T, ln2_g[l][None, :], ln2_b[l][None, :])
    return xc.reshape(batch, seq, d)
```

```python
import functools

import numpy as np
import jax
import jax.numpy as jnp
from jax import lax
from jax.experimental import pallas as pl
from jax.experimental.pallas import tpu as pltpu

F32 = jnp.float32
BF16 = jnp.bfloat16

RET_HEADS = 8
RET_QK_DIM = 128
RET_V_DIM = 256
CHUNK = 128
ROPE_THETA = 10000.0
SGU_GROUPS = 16
N_GROUPS = 4
EXPERTS_PER_GROUP = 8
N_EXPERTS = N_GROUPS * EXPERTS_PER_GROUP
ROUTER_ROWS = 40
LN_EPS = 1e-5
DEPTH = 1
DN_ALPHA = (2 * DEPTH) ** 0.25
SQRT_HALF = np.sqrt(0.5).astype(np.float32)

LANES = 128
VMEM_LIMIT = 52 * 1024 * 1024
EXPERT_TILE = 256


def _params(sem, vmem=VMEM_LIMIT, flags=None):
    return pltpu.CompilerParams(dimension_semantics=sem, vmem_limit_bytes=vmem, flags=flags)


def _dot(a, b):
    return jnp.dot(a, b, preferred_element_type=F32)


def _dot_nt(a, b):
    return lax.dot_general(a, b, (((1,), (1,)), ((), ())), preferred_element_type=F32)


def _dot_tn(a, b):
    return lax.dot_general(a, b, (((0,), (0,)), ((), ())), preferred_element_type=F32)


def _layer_norm(r, g, b):
    mu = jnp.mean(r, axis=-1, keepdims=True)
    d = r - mu
    var = jnp.mean(d * d, axis=-1, keepdims=True)
    return d * lax.rsqrt(var + LN_EPS) * g + b


def _gelu(x):
    return 0.5 * x * (1.0 + lax.erf(x * SQRT_HALF))


def _inproj_body(x_ref, w_ref, b_ref, o_ref, wb_ref, *, gelu_blocks):
    @pl.when(pl.program_id(1) == 0)
    def _():
        wb_ref[...] = w_ref[...].astype(BF16)

    @pl.when(pl.program_id(0) < gelu_blocks)
    def _():
        o_ref[...] = _gelu(_dot(x_ref[...], wb_ref[...])).astype(BF16)

    @pl.when(pl.program_id(0) >= gelu_blocks)
    def _():
        o_ref[...] = jax.nn.sigmoid(_dot(x_ref[...], wb_ref[...]) + b_ref[...]).astype(BF16)


def _in_proj(xb, w, col0, gate_bias):
    n, d = xb.shape
    width = w.shape[1] - col0
    tm, tn = min(1024, n), 1024
    assert col0 % tn == 0 and width % (2 * tn) == 0
    return pl.pallas_call(
        functools.partial(_inproj_body, gelu_blocks=width // (2 * tn)),
        grid=(width // tn, n // tm),
        in_specs=[pl.BlockSpec((tm, d), lambda j, i: (i, 0)),
                  pl.BlockSpec((d, tn), lambda j, i: (0, j + col0 // tn)),
                  pl.BlockSpec((1, tn), lambda j, i: (0, j))],
        out_specs=pl.BlockSpec((tm, tn), lambda j, i: (i, j)),
        out_shape=jax.ShapeDtypeStruct((n, width), BF16),
        scratch_shapes=[pltpu.VMEM((d, tn), BF16)],
        compiler_params=_params(("arbitrary", "arbitrary")), name="in_proj",
    )(xb, w, gate_bias)


HEAD_COLS = 2 * RET_QK_DIM + 2 * RET_V_DIM


def _regroup_body(q_ref, k_ref, v_ref, g_ref, o_ref):
    dk, dv = RET_QK_DIM, RET_V_DIM
    o_ref[0, :, 0:dk] = q_ref[...].astype(BF16)
    o_ref[0, :, dk:2 * dk] = k_ref[...].astype(BF16)
    o_ref[0, :, 2 * dk:2 * dk + dv] = v_ref[...].astype(BF16)
    o_ref[0, :, 2 * dk + dv:] = g_ref[...].astype(BF16)


def _regroup_retention_weights(w):
    d = w.shape[0]
    k_blk = RET_HEADS
    v_blk = 2 * RET_HEADS * RET_QK_DIM // RET_V_DIM
    return pl.pallas_call(
        _regroup_body, grid=(RET_HEADS,),
        in_specs=[pl.BlockSpec((d, RET_QK_DIM), lambda h: (0, h)),
                  pl.BlockSpec((d, RET_QK_DIM), lambda h: (0, k_blk + h)),
                  pl.BlockSpec((d, RET_V_DIM), lambda h: (0, v_blk + h)),
                  pl.BlockSpec((d, RET_V_DIM), lambda h: (0, v_blk + RET_HEADS + h))],
        out_specs=pl.BlockSpec((1, d, HEAD_COLS), lambda h: (h, 0, 0)),
        out_shape=jax.ShapeDtypeStruct((RET_HEADS, d, HEAD_COLS), BF16),
        compiler_params=_params(("arbitrary",)), name="regroup_retention_weights",
    )(w, w, w, w)


def _retention_body(x_ref, pos_ref, freq_ref, sign_ref, wa_ref, wp_ref, mask_ref, xi_ref, zeta_ref, cd_ref,
                    gain_ref, o_ref, xb_ref, state_ref, yg_ref, cos_ref, sin_ref, *, n_chunks):
    @pl.when(pl.program_id(1) == 0)
    def _():
        state_ref[...] = jnp.zeros_like(state_ref)

    ang = pos_ref[...].astype(F32) * freq_ref[...]
    cos_ref[...] = jnp.cos(ang)
    sin_ref[...] = jnp.sin(ang) * sign_ref[...]

    scale = RET_QK_DIM ** -0.5
    half = RET_QK_DIM // 2
    dk, dv = RET_QK_DIM, RET_V_DIM
    xb = x_ref[...].astype(BF16)
    xb_ref[...] = xb
    ahead = 1
    projs = [_dot(xb, wa_ref[h]) for h in range(ahead)]
    for hd in range(RET_HEADS):
        proj = projs[hd]
        if hd + ahead < RET_HEADS:
            projs.append(_dot(xb, wa_ref[hd + ahead]))
        gain = gain_ref[:, hd * dv:(hd + 1) * dv]
        for c in range(n_chunks):
            lo, hi = c * CHUNK, (c + 1) * CHUNK
            cos, sin = cos_ref[lo:hi, :], sin_ref[lo:hi, :]
            q, k = proj[lo:hi, 0:dk], proj[lo:hi, dk:2 * dk]
            vb = proj[lo:hi, 2 * dk:2 * dk + dv].astype(BF16)
            g = proj[lo:hi, 2 * dk + dv:]
            qr = q * cos + pltpu.roll(q, half, 1) * sin
            kr = (k * cos + pltpu.roll(k, half, 1) * sin) * scale
            scores = _dot_nt(qr.astype(BF16), kr.astype(BF16)) * mask_ref[hd]
            state = state_ref[hd]
            lhs = jnp.concatenate([scores.astype(BF16), (qr * xi_ref[hd]).astype(BF16)], axis=1)
            rhs = jnp.concatenate([vb, state.astype(BF16)], axis=0)
            out = _dot(lhs, rhs)
            state_ref[hd] = state * cd_ref[hd] + _dot_tn((kr * zeta_ref[hd]).astype(BF16), vb)
            mu = jnp.mean(out, axis=-1, keepdims=True)
            d = out - mu
            var = jnp.mean(d * d, axis=-1, keepdims=True)
            yn = d * lax.rsqrt(var + LN_EPS) * gain
            yg_ref[lo:hi, hd * dv:(hd + 1) * dv] = (g * jax.nn.sigmoid(g) * yn).astype(BF16)
    o_ref[...] = _dot(yg_ref[...], wp_ref[...]).astype(BF16)


def _retention_branch(x2, pos_col, freq2, sign2, wa_bf, wp_bf, consts, gain, batch, seq):
    n, d = x2.shape
    tr = min(256, seq)
    nr = seq // tr
    mask, xi_b, zeta_b, cd_b = consts
    vw = RET_HEADS * RET_V_DIM

    def once(shape):
        return pl.BlockSpec(shape, lambda b, r: (0,) * len(shape), pipeline_mode=pl.Buffered(1))

    rows = lambda w: pl.BlockSpec((tr, w), lambda b, r: (b * nr + r, 0))
    return pl.pallas_call(
        functools.partial(_retention_body, n_chunks=tr // CHUNK),
        grid=(batch, nr),
        in_specs=[rows(d), rows(1), once((1, LANES)), once((1, LANES)),
                  once((RET_HEADS, d, HEAD_COLS)), once((vw, d)),
                  once((RET_HEADS, CHUNK, CHUNK)), once((RET_HEADS, CHUNK, RET_QK_DIM)),
                  once((RET_HEADS, CHUNK, RET_QK_DIM)), once((RET_HEADS, 1, RET_V_DIM)), once((1, vw))],
        out_specs=[rows(d), rows(d)],
        out_shape=[jax.ShapeDtypeStruct((n, d), BF16), jax.ShapeDtypeStruct((n, d), BF16)],
        scratch_shapes=[pltpu.VMEM((RET_HEADS, RET_QK_DIM, RET_V_DIM), F32), pltpu.VMEM((tr, vw), BF16),
                        pltpu.VMEM((tr, LANES), F32), pltpu.VMEM((tr, LANES), F32)],
        compiler_params=_params(("arbitrary", "arbitrary")), name="retention_branch",
    )(x2, pos_col, freq2, sign2, wa_bf, wp_bf, mask, xi_b, zeta_b, cd_b, gain)


def _sgu_merge_body(u_ref, v_ref, sa_ref, sb_ref, ya_ref, lng_ref, lnb_ref, ws_ref, bs_ref, wp_ref,
                    o_ref, wtril_ref, ys_ref, wpb_ref, *, n_chunks):
    @pl.when(pl.program_id(0) == 0)
    def _():
        r = lax.broadcasted_iota(jnp.int32, (CHUNK, CHUNK), 0)
        c = lax.broadcasted_iota(jnp.int32, (CHUNK, CHUNK), 1)
        for g in range(SGU_GROUPS):
            wtril_ref[g] = jnp.where(r >= c, ws_ref[g], 0.0).astype(BF16)
        wpb_ref[...] = wp_ref[...].astype(BF16)

    vn = _layer_norm(v_ref[...].astype(F32), lng_ref[...], lnb_ref[...]).astype(BF16)
    for g in range(SGU_GROUPS):
        cols = pl.ds(g * CHUNK, CHUNK)
        lo = g * CHUNK
        rhs = jnp.concatenate([vn[c * CHUNK:(c + 1) * CHUNK, lo:lo + CHUNK] for c in range(n_chunks)], axis=1)
        mix = _dot(wtril_ref[g], rhs)
        bias = bs_ref[g]
        for c in range(n_chunks):
            rows = pl.ds(c * CHUNK, CHUNK)
            u = u_ref[rows, cols].astype(F32)
            ys_ref[rows, cols] = (u * (mix[:, c * CHUNK:(c + 1) * CHUNK] + bias)).astype(BF16)

    yb = _dot(ys_ref[...], wpb_ref[...])
    o_ref[...] = (sa_ref[...].astype(F32) * ya_ref[...].astype(F32) + sb_ref[...].astype(F32) * yb).astype(BF16)


def _sgu_merge(h, y_a, ln_g, ln_b, w_s, b_s_b, wp):
    n, d = y_a.shape
    ts = min(256, n)
    tile = lambda blk: pl.BlockSpec((ts, d), lambda i: (i, blk))
    row = pl.BlockSpec((1, d), lambda i: (0, 0))
    full3 = pl.BlockSpec((SGU_GROUPS, CHUNK, CHUNK), lambda i: (0, 0, 0))
    return pl.pallas_call(
        functools.partial(_sgu_merge_body, n_chunks=ts // CHUNK),
        grid=(n // ts,),
        in_specs=[tile(0), tile(1), tile(2), tile(3), tile(0), row, row, full3, full3,
                  pl.BlockSpec((d, d), lambda i: (0, 0), pipeline_mode=pl.Buffered(1))],
        out_specs=tile(0),
        out_shape=jax.ShapeDtypeStruct((n, d), BF16),
        scratch_shapes=[pltpu.VMEM((SGU_GROUPS, CHUNK, CHUNK), BF16), pltpu.VMEM((ts, d), BF16),
                        pltpu.VMEM((d, d), BF16)],
        compiler_params=_params(("arbitrary",)), name="sgu_merge",
    )(h, h, h, h, y_a, ln_g, ln_b, w_s, b_s_b, wp)


def _outproj_router_body(m_ref, x_ref, w_ref, g_ref, b_ref, whi_ref, wlo_ref, rb_ref,
                         o_ref, orow_hbm, ids_ref, cw_ref, cnt_ref, carry_ref, tri_ref, wb_ref, rsem, *, tm):
    step = pl.program_id(0)

    @pl.when(step == 0)
    def _():
        carry_ref[...] = jnp.zeros_like(carry_ref)
        r = lax.broadcasted_iota(jnp.int32, (tm, tm), 0)
        c = lax.broadcasted_iota(jnp.int32, (tm, tm), 1)
        tri_ref[...] = jnp.where(r < c, 1.0, 0.0).astype(BF16)
        wb_ref[...] = w_ref[...].astype(BF16)

    r = DN_ALPHA * x_ref[...] + _dot(m_ref[...], wb_ref[...])
    x = _layer_norm(r, g_ref[...], b_ref[...])
    o_ref[...] = x

    row_copy = _row_copy(o_ref, orow_hbm.at[pl.ds(pl.multiple_of(step * tm, tm), tm), 0], rsem)
    row_copy.start()

    hi = x.astype(BF16)
    lo = (x - hi.astype(F32)).astype(BF16)
    whi = whi_ref[...]
    logits = _dot_nt(whi, hi) + _dot_nt(whi, lo) + _dot_nt(wlo_ref[...], hi) + rb_ref[...]
    el = logits[0:N_EXPERTS, :]
    gl = logits[N_EXPERTS:N_EXPERTS + N_GROUPS, :]

    gi = lax.broadcasted_iota(jnp.int32, gl.shape, 0)
    gmax = jnp.max(gl, axis=0, keepdims=True)
    gidx = jnp.min(jnp.where(gl == gmax, gi, N_GROUPS), axis=0, keepdims=True)
    p_group = 1.0 / jnp.sum(jnp.exp(gl - gmax), axis=0, keepdims=True)

    ei = lax.broadcasted_iota(jnp.int32, el.shape, 0)
    first = gidx * EXPERTS_PER_GROUP
    in_group = (ei >= first) & (ei < first + EXPERTS_PER_GROUP)
    m1 = jnp.where(in_group, el, -jnp.inf)
    v1 = jnp.max(m1, axis=0, keepdims=True)
    i1 = jnp.min(jnp.where(in_group & (m1 == v1), ei, N_EXPERTS), axis=0, keepdims=True)
    rest = in_group & (ei != i1)
    m2 = jnp.where(rest, el, -jnp.inf)
    v2 = jnp.max(m2, axis=0, keepdims=True)
    i2 = jnp.min(jnp.where(rest & (m2 == v2), ei, N_EXPERTS), axis=0, keepdims=True)
    t = jnp.exp(v2 - v1)
    p1 = 1.0 / (1.0 + t)
    p2 = t * p1

    sel1, sel2 = ei == i1, ei == i2
    onehot = jnp.where(sel1 | sel2, 1.0, 0.0)
    before = _dot(onehot.astype(BF16), tri_ref[...]) + carry_ref[:, 0:1]
    r1 = jnp.sum(jnp.where(sel1, before, 0.0), axis=0, keepdims=True)
    r2 = jnp.sum(jnp.where(sel2, before, 0.0), axis=0, keepdims=True)
    carry_ref[...] = carry_ref[...] + jnp.sum(onehot, axis=1, keepdims=True)

    ids_ref[0:1, :] = i1
    ids_ref[1:2, :] = i2
    ids_ref[2:3, :] = r1.astype(jnp.int32)
    ids_ref[3:4, :] = r2.astype(jnp.int32)
    cw_ref[0:1, :] = p_group * p1
    cw_ref[1:2, :] = p_group * p2
    cnt_ref[...] = carry_ref[...]
    row_copy.wait()


def _out_proj_router(merged, x2, w_out, ln_g, ln_b, w_hi, w_lo, bias_col):
    n, d = x2.shape
    tm = min(512, n)
    tile = pl.BlockSpec((tm, d), lambda i: (i, 0))
    row = pl.BlockSpec((1, d), lambda i: (0, 0))
    full = pl.BlockSpec((ROUTER_ROWS, d), lambda i: (0, 0))
    return pl.pallas_call(
        functools.partial(_outproj_router_body, tm=tm), grid=(n // tm,),
        in_specs=[tile, tile, pl.BlockSpec((d, d), lambda i: (0, 0), pipeline_mode=pl.Buffered(1)), row, row,
                  full, full, pl.BlockSpec((ROUTER_ROWS, 1), lambda i: (0, 0))],
        out_specs=[tile, pl.BlockSpec(memory_space=pl.ANY),
                   pl.BlockSpec((4, tm), lambda i: (0, i)),
                   pl.BlockSpec((2, tm), lambda i: (0, i)),
                   pl.BlockSpec((N_EXPERTS, LANES), lambda i: (0, 0))],
        out_shape=[jax.ShapeDtypeStruct((n, d), F32), jax.ShapeDtypeStruct((n, 1, d), F32),
                   jax.ShapeDtypeStruct((4, n), jnp.int32),
                   jax.ShapeDtypeStruct((2, n), F32),
                   jax.ShapeDtypeStruct((N_EXPERTS, LANES), F32)],
        scratch_shapes=[pltpu.VMEM((N_EXPERTS, LANES), F32), pltpu.VMEM((tm, tm), BF16),
                        pltpu.VMEM((d, d), BF16), pltpu.SemaphoreType.DMA(())],
        compiler_params=_params(("arbitrary",)), name="out_proj_ln1_router",
    )(merged, x2, w_out, ln_g, ln_b, w_hi, w_lo, bias_col)


def _routing_tables(ids, counts, n, max_pairs):
    te = EXPERT_TILE
    n_tiles = 2 * n // te
    i32 = jnp.int32
    row_end = jnp.cumsum(counts)
    row_start = row_end - counts
    dest1 = row_start[ids[0]] + ids[2]
    dest2 = row_start[ids[1]] + ids[3]
    ne_cum = jnp.cumsum((counts > 0).astype(i32))
    tile_lo = jnp.arange(n_tiles, dtype=i32) * te
    first_e = jnp.sum((row_end[None, :] <= tile_lo[:, None]).astype(i32), axis=1)
    last_e = jnp.sum((row_end[None, :] <= (tile_lo + te - 1)[:, None]).astype(i32), axis=1)
    pairs_t = ne_cum[last_e] - ne_cum[first_e] + 1
    pair_end = jnp.cumsum(pairs_t)
    pair_start = pair_end - pairs_t
    n_pairs = pair_end[-1]
    p = jnp.minimum(jnp.arange(max_pairs, dtype=i32), n_pairs - 1)
    p_tile = jnp.sum((pair_end[None, :] <= p[:, None]).astype(i32), axis=1)
    rank = ne_cum[first_e[p_tile]] - 1 + (p - pair_start[p_tile])
    p_exp = jnp.sum((ne_cum[None, :] <= rank[:, None]).astype(i32), axis=1)
    p_lo = jnp.maximum(row_start[p_exp], p_tile * te) - p_tile * te
    p_hi = jnp.minimum(row_end[p_exp], (p_tile + 1) * te) - p_tile * te
    p_next = jnp.where(rank + 1 < ne_cum[-1],
                       jnp.sum((ne_cum[None, :] <= (rank + 1)[:, None]).astype(i32), axis=1), -1)
    p_slot = rank & 1
    return (dest1, dest2, p_tile.astype(i32), p_exp.astype(i32), p_lo.astype(i32), p_hi.astype(i32),
            n_pairs.reshape(1).astype(i32), p_next.astype(i32), p_slot.astype(i32))


def _row_copy(src, dst, sem):
    return pltpu.make_async_copy(src, dst, sem)


def _expert_body(ptile_ref, pexp_ref, plo_ref, phi_ref, np_ref, pnext_ref, pslot_ref, d1_ref, d2_ref,
                 x1_hbm, w1_hbm, w3_hbm, w2_hbm, y2_hbm,
                 w1b, w3b, w2b, wf1, wf3, wf2, xbuf0, xbuf1, xbuf2, ybuf0, ybuf1, ybuf2, inv_s, gsem, ssem, wsem,
                 *, n, n_tiles):
    te = EXPERT_TILE
    p = pl.program_id(0)
    valid = p < np_ref[0]
    t = ptile_ref[p]
    pm1 = jnp.maximum(p - 1, 0)
    first = (p == 0) | (ptile_ref[pm1] != t)
    new_expert = (p == 0) | (pexp_ref[pm1] != pexp_ref[p])
    lo, hi = plo_ref[p], phi_ref[p]
    xbufs, ybufs = (xbuf0, xbuf1, xbuf2), (ybuf0, ybuf1, ybuf2)

    def gather(tile, slot, j, priority=0):
        v = inv_s[tile * te + j]
        tok = jnp.where(v >= n, v - n, v)
        _row_copy(x1_hbm.at[tok], xbufs[slot].at[pl.ds(j, 1)], gsem.at[slot]).start(priority=priority)

    def scatter(tile, slot, j, priority=0):
        _row_copy(ybufs[slot].at[pl.ds(j, 1)], y2_hbm.at[inv_s[tile * te + j]], ssem).start(priority=priority)

    def wait_gather(slot):
        _row_copy(x1_hbm.at[pl.ds(0, te), 0], xbufs[slot], gsem.at[slot]).wait()

    def wait_scatter(slot):
        _row_copy(ybufs[slot], y2_hbm.at[pl.ds(0, te), 0], ssem).wait()

    def issue_loop(fn, tile, slot):
        def one(j, c):
            fn(tile, slot, j)
            return c
        lax.fori_loop(0, te, one, 0, unroll=8)

    def weight_copies(e, slot):
        return [_row_copy(w1_hbm.at[e], wf1.at[slot], wsem.at[slot]),
                _row_copy(w3_hbm.at[e], wf3.at[slot], wsem.at[slot]),
                _row_copy(w2_hbm.at[e], wf2.at[slot], wsem.at[slot])]

    @pl.when(p == 0)
    def _():
        for c in weight_copies(pexp_ref[0], pslot_ref[0]):
            c.start()

        def invert(tk, c):
            inv_s[d1_ref[tk]] = tk
            inv_s[d2_ref[tk]] = tk + n
            return c
        lax.fori_loop(0, n, invert, 0, unroll=8)
        for yb in ybufs:
            yb[...] = jnp.zeros_like(yb)
        issue_loop(gather, 0, 0)
        issue_loop(gather, 1, 1)

    @pl.when(valid & new_expert)
    def _():
        slot = pslot_ref[p]
        for c in weight_copies(pexp_ref[p], slot):
            c.wait()
        w1b[...] = wf1[slot].astype(BF16)
        w3b[...] = wf3[slot].astype(BF16)
        w2b[...] = wf2[slot].astype(BF16)

        @pl.when(pnext_ref[p] >= 0)
        def _():
            for c in weight_copies(pnext_ref[p], 1 - slot):
                c.start()

    def ffn(slot):
        xb = xbufs[slot][...].astype(BF16)
        a = _dot(xb, w1b[...])
        b = _dot(xb, w3b[...])
        y = _dot((a * jax.nn.sigmoid(a) * b).astype(BF16), w2b[...])
        rows = lax.broadcasted_iota(jnp.int32, (te, 1), 0)
        ybufs[slot][...] = jnp.where((rows >= lo) & (rows < hi), y, ybufs[slot][...])

    for r in range(3):
        ahead, behind = (r + 2) % 3, (r + 1) % 3
        mine = valid & (lax.rem(t, 3) == r)
        interior = first & (t >= 1) & (t < n_tiles - 2)

        @pl.when(mine & first)
        def _():
            wait_gather(r)

            @pl.when(t >= 2)
            def _():
                wait_scatter(behind)

        @pl.when(mine & interior)
        def _():
            for j in range(te):
                gather(t + 2, ahead, j, priority=j % 2)
                scatter(t - 1, ahead, j, priority=j % 2)
            ffn(r)

        @pl.when(mine & jnp.logical_not(interior))
        def _():
            @pl.when(first & (t == 0))
            def _():
                issue_loop(gather, 2, ahead)

            @pl.when(first & (t >= n_tiles - 2))
            def _():
                issue_loop(scatter, t - 1, ahead)

            ffn(r)

    @pl.when(p == pl.num_programs(0) - 1)
    def _():
        last = n_tiles - 1
        wait_scatter((last - 1) % 3)
        issue_loop(scatter, last, last % 3)
        wait_scatter(last % 3)


def _experts(x1_rows, tables, w1, w3, w2):
    n, _, d = x1_rows.shape
    de = w1.shape[2]
    te = EXPERT_TILE
    n_tiles = 2 * n // te
    assert n_tiles >= 4
    max_pairs = n_tiles + N_EXPERTS - 1
    hbm = pl.BlockSpec(memory_space=pl.ANY)
    return pl.pallas_call(
        functools.partial(_expert_body, n=n, n_tiles=n_tiles),
        grid_spec=pltpu.PrefetchScalarGridSpec(
            num_scalar_prefetch=9, grid=(max_pairs,),
            in_specs=[hbm, hbm, hbm, hbm],
            out_specs=hbm,
            scratch_shapes=[pltpu.VMEM((d, de), BF16), pltpu.VMEM((d, de), BF16), pltpu.VMEM((de, d), BF16),
                            pltpu.VMEM((2, d, de), F32), pltpu.VMEM((2, d, de), F32), pltpu.VMEM((2, de, d), F32),
                            *[pltpu.VMEM((te, d), F32)] * 6,
                            pltpu.SMEM((2 * n,), jnp.int32),
                            pltpu.SemaphoreType.DMA((3,)), pltpu.SemaphoreType.DMA(()),
                            pltpu.SemaphoreType.DMA((2,))]),
        out_shape=jax.ShapeDtypeStruct((2 * n, 1, d), F32),
        compiler_params=_params(("arbitrary",)), name="experts",
    )(*tables, x1_rows, w1, w3, w2)


def _combine_body(x_ref, cw_ref, g_ref, b_ref, y2_hbm, o_ref, ybuf, sem, *, tm, n):
    i = pl.program_id(0)

    def copies(tile, buf):
        r0 = pl.multiple_of(tile * tm, tm)
        return [_row_copy(y2_hbm.at[pl.ds(k * n + r0, tm), 0], ybuf.at[buf, k], sem.at[buf]) for k in range(2)]

    @pl.when(i == 0)
    def _():
        for c in copies(0, 0):
            c.start()

    @pl.when(i + 1 < pl.num_programs(0))
    def _():
        for c in copies(i + 1, (i + 1) % 2):
            c.start()

    buf = i % 2
    for c in copies(i, buf):
        c.wait()
    y = cw_ref[:, 0:1] * ybuf[buf, 0] + cw_ref[:, 1:2] * ybuf[buf, 1]
    o_ref[...] = _layer_norm(DN_ALPHA * x_ref[...] + y, g_ref[...], b_ref[...])


def _combine(y2, x1, cw_col, ln_g, ln_b):
    n, d = x1.shape
    tm = min(512, n)
    tile = pl.BlockSpec((tm, d), lambda i: (i, 0))
    row = pl.BlockSpec((1, d), lambda i: (0, 0))
    return pl.pallas_call(
        functools.partial(_combine_body, tm=tm, n=n), grid=(n // tm,),
        in_specs=[tile, pl.BlockSpec((tm, 2), lambda i: (i, 0)), row, row, pl.BlockSpec(memory_space=pl.ANY)],
        out_specs=tile,
        out_shape=jax.ShapeDtypeStruct((n, d), F32),
        scratch_shapes=[pltpu.VMEM((2, 2, tm, d), F32), pltpu.SemaphoreType.DMA((2,))],
        compiler_params=_params(("arbitrary",)), name="combine_ln2",
    )(x1, cw_col, ln_g, ln_b, y2)


def _retention_consts():
    h, c = RET_HEADS, CHUNK
    log_gamma = jnp.log1p(-jnp.exp2(-5.0 - jnp.arange(h, dtype=F32)))
    idx = jnp.arange(c, dtype=F32)
    rel = idx[:, None] - idx[None, :]
    mask = jnp.where(rel >= 0, jnp.exp(log_gamma[:, None, None] * jnp.maximum(rel, 0.0)), 0.0)
    xi = jnp.exp(log_gamma[:, None] * (idx + 1.0))
    zeta = jnp.exp(log_gamma[:, None] * (c - 1.0 - idx))
    chunk_decay = jnp.exp(log_gamma * c)
    xi_b = jnp.broadcast_to(xi[:, :, None], (h, c, RET_QK_DIM))
    zeta_b = jnp.broadcast_to(zeta[:, :, None], (h, c, RET_QK_DIM))
    cd_b = jnp.broadcast_to(chunk_decay[:, None, None], (h, 1, RET_V_DIM))
    return mask, xi_b, zeta_b, cd_b


def _rope_consts():
    half = RET_QK_DIM // 2
    freq = ROPE_THETA ** (-jnp.arange(half, dtype=F32) / half)
    freq2 = jnp.concatenate([freq, freq])[None, :]
    sign2 = jnp.concatenate([-jnp.ones((half,), F32), jnp.ones((half,), F32)])[None, :]
    return freq2, sign2


def kernel(x, positions, w_in, b_gate, ret_gn_g, sgu_ln_g, sgu_ln_b, sgu_w, sgu_b, w_proj_ret, w_proj_sgu,
           w_out, ln1_g, ln1_b, w_group, b_group, w_er, b_er, w1, w3, w2, ln2_g, ln2_b):
    batch, seq, d = x.shape
    n = batch * seq
    qk_w, v_w = RET_HEADS * RET_QK_DIM, RET_HEADS * RET_V_DIM
    assert d == v_w == SGU_GROUPS * CHUNK and seq % CHUNK == 0
    ret_cols = 2 * qk_w + 2 * v_w
    max_pairs = (2 * n) // EXPERT_TILE + N_EXPERTS - 1

    freq2, sign2 = _rope_consts()
    pos_col = positions.reshape(n, 1)
    ret_consts = _retention_consts()

    xc = x.reshape(n, d)
    for l in range(w_in.shape[0]):
        wa = _regroup_retention_weights(w_in[l])
        y_a, xb = _retention_branch(xc, pos_col, freq2, sign2, wa, w_proj_ret[l].astype(BF16), ret_consts,
                                    ret_gn_g[l][None, :], batch, seq)
        gate_bias = jnp.concatenate([jnp.zeros((2 * d,), F32), b_gate[l].reshape(-1)])[None, :]
        h = _in_proj(xb, w_in[l], ret_cols, gate_bias)
        bias_b = jnp.broadcast_to(sgu_b[l][:, :, None], (SGU_GROUPS, CHUNK, CHUNK))
        merged = _sgu_merge(h, y_a, sgu_ln_g[l][None, :], sgu_ln_b[l][None, :], sgu_w[l], bias_b, w_proj_sgu[l])

        w_r = jnp.concatenate([w_er[l], w_group[l]], axis=1).T
        w_r = jnp.pad(w_r, ((0, ROUTER_ROWS - w_r.shape[0]), (0, 0)))
        w_hi = w_r.astype(BF16)
        w_lo = (w_r - w_hi.astype(F32)).astype(BF16)
        bias = jnp.pad(jnp.concatenate([b_er[l], b_group[l]]), (0, ROUTER_ROWS - N_EXPERTS - N_GROUPS))[:, None]
        x1, x1_rows, ids, cw, cnt = _out_proj_router(merged, xc, w_out[l], ln1_g[l][None, :],
                                                     ln1_b[l][None, :], w_hi, w_lo, bias)

        counts = cnt[:, 0].astype(jnp.int32)
        dest1, dest2, p_tile, p_exp, p_lo, p_hi, n_pairs, p_next, p_slot = _routing_tables(ids, counts, n, max_pairs)
        y2 = _experts(x1_rows, (p_tile, p_exp, p_lo, p_hi, n_pairs, p_next, p_slot, dest1, dest2),
                      w1[l], w3[l], w2[l])
        xc = _combine(y2, x1, cw.T, ln2_g[l][None, :], ln2_b[l][None, :])
    return xc.reshape(batch, seq, d)
```

```python
import functools

import numpy as np
import jax
import jax.numpy as jnp
from jax import lax
from jax.experimental import pallas as pl
from jax.experimental.pallas import tpu as pltpu

F32 = jnp.float32
BF16 = jnp.bfloat16

RET_HEADS = 8
RET_QK_DIM = 128
RET_V_DIM = 256
CHUNK = 128
ROPE_THETA = 10000.0
SGU_GROUPS = 16
N_GROUPS = 4
EXPERTS_PER_GROUP = 8
N_EXPERTS = N_GROUPS * EXPERTS_PER_GROUP
ROUTER_ROWS = 40
LN_EPS = 1e-5
DEPTH = 1
DN_ALPHA = (2 * DEPTH) ** 0.25
SQRT_HALF = np.sqrt(0.5).astype(np.float32)

LANES = 128
VMEM_LIMIT = 52 * 1024 * 1024
EXPERT_TILE = 256


def _params(sem, vmem=VMEM_LIMIT, flags=None):
    return pltpu.CompilerParams(dimension_semantics=sem, vmem_limit_bytes=vmem, flags=flags)


def _dot(a, b):
    return jnp.dot(a, b, preferred_element_type=F32)


def _dot_nt(a, b):
    return lax.dot_general(a, b, (((1,), (1,)), ((), ())), preferred_element_type=F32)


def _dot_tn(a, b):
    return lax.dot_general(a, b, (((0,), (0,)), ((), ())), preferred_element_type=F32)


def _layer_norm(r, g, b):
    mu = jnp.mean(r, axis=-1, keepdims=True)
    d = r - mu
    var = jnp.mean(d * d, axis=-1, keepdims=True)
    return d * lax.rsqrt(var + LN_EPS) * g + b


def _gelu(x):
    return 0.5 * x * (1.0 + lax.erf(x * SQRT_HALF))


def _inproj_body(x_ref, w_ref, b_ref, o_ref, wb_ref, *, gelu_blocks):
    @pl.when(pl.program_id(1) == 0)
    def _():
        wb_ref[...] = w_ref[...].astype(BF16)

    @pl.when(pl.program_id(0) < gelu_blocks)
    def _():
        o_ref[...] = _gelu(_dot(x_ref[...], wb_ref[...])).astype(BF16)

    @pl.when(pl.program_id(0) >= gelu_blocks)
    def _():
        o_ref[...] = jax.nn.sigmoid(_dot(x_ref[...], wb_ref[...]) + b_ref[...]).astype(BF16)


def _in_proj(xb, w, col0, gate_bias):
    n, d = xb.shape
    width = w.shape[1] - col0
    tm, tn = min(1024, n), 1024
    assert col0 % tn == 0 and width % (2 * tn) == 0
    return pl.pallas_call(
        functools.partial(_inproj_body, gelu_blocks=width // (2 * tn)),
        grid=(width // tn, n // tm),
        in_specs=[pl.BlockSpec((tm, d), lambda j, i: (i, 0)),
                  pl.BlockSpec((d, tn), lambda j, i: (0, j + col0 // tn)),
                  pl.BlockSpec((1, tn), lambda j, i: (0, j))],
        out_specs=pl.BlockSpec((tm, tn), lambda j, i: (i, j)),
        out_shape=jax.ShapeDtypeStruct((n, width), BF16),
        scratch_shapes=[pltpu.VMEM((d, tn), BF16)],
        compiler_params=_params(("arbitrary", "arbitrary")), name="in_proj",
    )(xb, w, gate_bias)


HEAD_COLS = 2 * RET_QK_DIM + 2 * RET_V_DIM


def _regroup_body(q_ref, k_ref, v_ref, g_ref, o_ref):
    dk, dv = RET_QK_DIM, RET_V_DIM
    o_ref[0, :, 0:dk] = q_ref[...].astype(BF16)
    o_ref[0, :, dk:2 * dk] = k_ref[...].astype(BF16)
    o_ref[0, :, 2 * dk:2 * dk + dv] = v_ref[...].astype(BF16)
    o_ref[0, :, 2 * dk + dv:] = g_ref[...].astype(BF16)


def _regroup_retention_weights(w):
    d = w.shape[0]
    k_blk = RET_HEADS
    v_blk = 2 * RET_HEADS * RET_QK_DIM // RET_V_DIM
    return pl.pallas_call(
        _regroup_body, grid=(RET_HEADS,),
        in_specs=[pl.BlockSpec((d, RET_QK_DIM), lambda h: (0, h)),
                  pl.BlockSpec((d, RET_QK_DIM), lambda h: (0, k_blk + h)),
                  pl.BlockSpec((d, RET_V_DIM), lambda h: (0, v_blk + h)),
                  pl.BlockSpec((d, RET_V_DIM), lambda h: (0, v_blk + RET_HEADS + h))],
        out_specs=pl.BlockSpec((1, d, HEAD_COLS), lambda h: (h, 0, 0)),
        out_shape=jax.ShapeDtypeStruct((RET_HEADS, d, HEAD_COLS), BF16),
        compiler_params=_params(("arbitrary",)), name="regroup_retention_weights",
    )(w, w, w, w)


def _retention_body(x_ref, pos_ref, freq_ref, sign_ref, wa_ref, wp_ref, mask_ref, xi_ref, zeta_ref, cd_ref,
                    gain_ref, o_ref, xb_ref, state_ref, yg_ref, cos_ref, sin_ref, *, n_chunks):
    @pl.when(pl.program_id(1) == 0)
    def _():
        state_ref[...] = jnp.zeros_like(state_ref)

    ang = pos_ref[...].astype(F32) * freq_ref[...]
    cos_ref[...] = jnp.cos(ang)
    sin_ref[...] = jnp.sin(ang) * sign_ref[...]

    scale = RET_QK_DIM ** -0.5
    half = RET_QK_DIM // 2
    dk, dv = RET_QK_DIM, RET_V_DIM
    xb = x_ref[...].astype(BF16)
    xb_ref[...] = xb
    ahead = 1
    projs = [_dot(xb, wa_ref[h]) for h in range(ahead)]
    for hd in range(RET_HEADS):
        proj = projs[hd]
        if hd + ahead < RET_HEADS:
            projs.append(_dot(xb, wa_ref[hd + ahead]))
        gain = gain_ref[:, hd * dv:(hd + 1) * dv]
        for c in range(n_chunks):
            lo, hi = c * CHUNK, (c + 1) * CHUNK
            cos, sin = cos_ref[lo:hi, :], sin_ref[lo:hi, :]
            q, k = proj[lo:hi, 0:dk], proj[lo:hi, dk:2 * dk]
            vb = proj[lo:hi, 2 * dk:2 * dk + dv].astype(BF16)
            g = proj[lo:hi, 2 * dk + dv:]
            qr = q * cos + pltpu.roll(q, half, 1) * sin
            kr = (k * cos + pltpu.roll(k, half, 1) * sin) * scale
            scores = _dot_nt(qr.astype(BF16), kr.astype(BF16)) * mask_ref[hd]
            state = state_ref[hd]
            lhs = jnp.concatenate([scores.astype(BF16), (qr * xi_ref[hd]).astype(BF16)], axis=1)
            rhs = jnp.concatenate([vb, state.astype(BF16)], axis=0)
            out = _dot(lhs, rhs)
            state_ref[hd] = state * cd_ref[hd] + _dot_tn((kr * zeta_ref[hd]).astype(BF16), vb)
            mu = jnp.mean(out, axis=-1, keepdims=True)
            d = out - mu
            var = jnp.mean(d * d, axis=-1, keepdims=True)
            yn = d * lax.rsqrt(var + LN_EPS) * gain
            yg_ref[lo:hi, hd * dv:(hd + 1) * dv] = (g * jax.nn.sigmoid(g) * yn).astype(BF16)
    o_ref[...] = _dot(yg_ref[...], wp_ref[...]).astype(BF16)


def _retention_branch(x2, pos_col, freq2, sign2, wa_bf, wp_bf, consts, gain, batch, seq):
    n, d = x2.shape
    tr = min(256, seq)
    nr = seq // tr
    mask, xi_b, zeta_b, cd_b = consts
    vw = RET_HEADS * RET_V_DIM

    def once(shape):
        return pl.BlockSpec(shape, lambda b, r: (0,) * len(shape), pipeline_mode=pl.Buffered(1))

    rows = lambda w: pl.BlockSpec((tr, w), lambda b, r: (b * nr + r, 0))
    return pl.pallas_call(
        functools.partial(_retention_body, n_chunks=tr // CHUNK),
        grid=(batch, nr),
        in_specs=[rows(d), rows(1), once((1, LANES)), once((1, LANES)),
                  once((RET_HEADS, d, HEAD_COLS)), once((vw, d)),
                  once((RET_HEADS, CHUNK, CHUNK)), once((RET_HEADS, CHUNK, RET_QK_DIM)),
                  once((RET_HEADS, CHUNK, RET_QK_DIM)), once((RET_HEADS, 1, RET_V_DIM)), once((1, vw))],
        out_specs=[rows(d), rows(d)],
        out_shape=[jax.ShapeDtypeStruct((n, d), BF16), jax.ShapeDtypeStruct((n, d), BF16)],
        scratch_shapes=[pltpu.VMEM((RET_HEADS, RET_QK_DIM, RET_V_DIM), F32), pltpu.VMEM((tr, vw), BF16),
                        pltpu.VMEM((tr, LANES), F32), pltpu.VMEM((tr, LANES), F32)],
        compiler_params=_params(("arbitrary", "arbitrary")), name="retention_branch",
    )(x2, pos_col, freq2, sign2, wa_bf, wp_bf, mask, xi_b, zeta_b, cd_b, gain)


def _sgu_merge_body(u_ref, v_ref, sa_ref, sb_ref, ya_ref, lng_ref, lnb_ref, ws_ref, bs_ref, wp_ref,
                    o_ref, wtril_ref, ys_ref, wpb_ref, *, n_chunks):
    @pl.when(pl.program_id(0) == 0)
    def _():
        r = lax.broadcasted_iota(jnp.int32, (CHUNK, CHUNK), 0)
        c = lax.broadcasted_iota(jnp.int32, (CHUNK, CHUNK), 1)
        for g in range(SGU_GROUPS):
            wtril_ref[g] = jnp.where(r >= c, ws_ref[g], 0.0).astype(BF16)
        wpb_ref[...] = wp_ref[...].astype(BF16)

    vn = _layer_norm(v_ref[...].astype(F32), lng_ref[...], lnb_ref[...]).astype(BF16)
    for g in range(SGU_GROUPS):
        cols = pl.ds(g * CHUNK, CHUNK)
        lo = g * CHUNK
        rhs = jnp.concatenate([vn[c * CHUNK:(c + 1) * CHUNK, lo:lo + CHUNK] for c in range(n_chunks)], axis=1)
        mix = _dot(wtril_ref[g], rhs)
        bias = bs_ref[g]
        for c in range(n_chunks):
            rows = pl.ds(c * CHUNK, CHUNK)
            u = u_ref[rows, cols].astype(F32)
            ys_ref[rows, cols] = (u * (mix[:, c * CHUNK:(c + 1) * CHUNK] + bias)).astype(BF16)

    yb = _dot(ys_ref[...], wpb_ref[...])
    o_ref[...] = (sa_ref[...].astype(F32) * ya_ref[...].astype(F32) + sb_ref[...].astype(F32) * yb).astype(BF16)


def _sgu_merge(h, y_a, ln_g, ln_b, w_s, b_s_b, wp):
    n, d = y_a.shape
    ts = min(256, n)
    tile = lambda blk: pl.BlockSpec((ts, d), lambda i: (i, blk))
    row = pl.BlockSpec((1, d), lambda i: (0, 0))
    full3 = pl.BlockSpec((SGU_GROUPS, CHUNK, CHUNK), lambda i: (0, 0, 0))
    return pl.pallas_call(
        functools.partial(_sgu_merge_body, n_chunks=ts // CHUNK),
        grid=(n // ts,),
        in_specs=[tile(0), tile(1), tile(2), tile(3), tile(0), row, row, full3, full3,
                  pl.BlockSpec((d, d), lambda i: (0, 0), pipeline_mode=pl.Buffered(1))],
        out_specs=tile(0),
        out_shape=jax.ShapeDtypeStruct((n, d), BF16),
        scratch_shapes=[pltpu.VMEM((SGU_GROUPS, CHUNK, CHUNK), BF16), pltpu.VMEM((ts, d), BF16),
                        pltpu.VMEM((d, d), BF16)],
        compiler_params=_params(("arbitrary",)), name="sgu_merge",
    )(h, h, h, h, y_a, ln_g, ln_b, w_s, b_s_b, wp)


def _outproj_router_body(m_ref, x_ref, w_ref, g_ref, b_ref, whi_ref, wlo_ref, rb_ref,
                         o_ref, orow_hbm, ids_ref, cw_ref, cnt_ref, carry_ref, tri_ref, wb_ref, rsem, *, tm):
    step = pl.program_id(0)

    @pl.when(step == 0)
    def _():
        carry_ref[...] = jnp.zeros_like(carry_ref)
        r = lax.broadcasted_iota(jnp.int32, (tm, tm), 0)
        c = lax.broadcasted_iota(jnp.int32, (tm, tm), 1)
        tri_ref[...] = jnp.where(r < c, 1.0, 0.0).astype(BF16)
        wb_ref[...] = w_ref[...].astype(BF16)

    r = DN_ALPHA * x_ref[...] + _dot(m_ref[...], wb_ref[...])
    x = _layer_norm(r, g_ref[...], b_ref[...])
    o_ref[...] = x

    row_copy = _row_copy(o_ref, orow_hbm.at[pl.ds(pl.multiple_of(step * tm, tm), tm), 0], rsem)
    row_copy.start()

    hi = x.astype(BF16)
    lo = (x - hi.astype(F32)).astype(BF16)
    whi = whi_ref[...]
    logits = _dot_nt(whi, hi) + _dot_nt(whi, lo) + _dot_nt(wlo_ref[...], hi) + rb_ref[...]
    el = logits[0:N_EXPERTS, :]
    gl = logits[N_EXPERTS:N_EXPERTS + N_GROUPS, :]

    gi = lax.broadcasted_iota(jnp.int32, gl.shape, 0)
    gmax = jnp.max(gl, axis=0, keepdims=True)
    gidx = jnp.min(jnp.where(gl == gmax, gi, N_GROUPS), axis=0, keepdims=True)
    p_group = 1.0 / jnp.sum(jnp.exp(gl - gmax), axis=0, keepdims=True)

    ei = lax.broadcasted_iota(jnp.int32, el.shape, 0)
    first = gidx * EXPERTS_PER_GROUP
    in_group = (ei >= first) & (ei < first + EXPERTS_PER_GROUP)
    m1 = jnp.where(in_group, el, -jnp.inf)
    v1 = jnp.max(m1, axis=0, keepdims=True)
    i1 = jnp.min(jnp.where(in_group & (m1 == v1), ei, N_EXPERTS), axis=0, keepdims=True)
    rest = in_group & (ei != i1)
    m2 = jnp.where(rest, el, -jnp.inf)
    v2 = jnp.max(m2, axis=0, keepdims=True)
    i2 = jnp.min(jnp.where(rest & (m2 == v2), ei, N_EXPERTS), axis=0, keepdims=True)
    t = jnp.exp(v2 - v1)
    p1 = 1.0 / (1.0 + t)
    p2 = t * p1

    sel1, sel2 = ei == i1, ei == i2
    onehot = jnp.where(sel1 | sel2, 1.0, 0.0)
    before = _dot(onehot.astype(BF16), tri_ref[...]) + carry_ref[:, 0:1]
    r1 = jnp.sum(jnp.where(sel1, before, 0.0), axis=0, keepdims=True)
    r2 = jnp.sum(jnp.where(sel2, before, 0.0), axis=0, keepdims=True)
    carry_ref[...] = carry_ref[...] + jnp.sum(onehot, axis=1, keepdims=True)

    ids_ref[0:1, :] = i1
    ids_ref[1:2, :] = i2
    ids_ref[2:3, :] = r1.astype(jnp.int32)
    ids_ref[3:4, :] = r2.astype(jnp.int32)
    cw_ref[0:1, :] = p_group * p1
    cw_ref[1:2, :] = p_group * p2
    cnt_ref[...] = carry_ref[...]
    row_copy.wait()


def _out_proj_router(merged, x2, w_out, ln_g, ln_b, w_hi, w_lo, bias_col):
    n, d = x2.shape
    tm = min(512, n)
    tile = pl.BlockSpec((tm, d), lambda i: (i, 0))
    row = pl.BlockSpec((1, d), lambda i: (0, 0))
    full = pl.BlockSpec((ROUTER_ROWS, d), lambda i: (0, 0))
    return pl.pallas_call(
        functools.partial(_outproj_router_body, tm=tm), grid=(n // tm,),
        in_specs=[tile, tile, pl.BlockSpec((d, d), lambda i: (0, 0), pipeline_mode=pl.Buffered(1)), row, row,
                  full, full, pl.BlockSpec((ROUTER_ROWS, 1), lambda i: (0, 0))],
        out_specs=[tile, pl.BlockSpec(memory_space=pl.ANY),
                   pl.BlockSpec((4, tm), lambda i: (0, i)),
                   pl.BlockSpec((2, tm), lambda i: (0, i)),
                   pl.BlockSpec((N_EXPERTS, LANES), lambda i: (0, 0))],
        out_shape=[jax.ShapeDtypeStruct((n, d), F32), jax.ShapeDtypeStruct((n, 1, d), F32),
                   jax.ShapeDtypeStruct((4, n), jnp.int32),
                   jax.ShapeDtypeStruct((2, n), F32),
                   jax.ShapeDtypeStruct((N_EXPERTS, LANES), F32)],
        scratch_shapes=[pltpu.VMEM((N_EXPERTS, LANES), F32), pltpu.VMEM((tm, tm), BF16),
                        pltpu.VMEM((d, d), BF16), pltpu.SemaphoreType.DMA(())],
        compiler_params=_params(("arbitrary",)), name="out_proj_ln1_router",
    )(merged, x2, w_out, ln_g, ln_b, w_hi, w_lo, bias_col)


def _dest_body(ids_ref, rs_ref, o_ref):
    ids = ids_ref[...]
    e = lax.broadcasted_iota(jnp.int32, (N_EXPERTS, ids.shape[1]), 0)
    for k in range(2):
        start = jnp.sum(jnp.where(e == ids[k:k + 1, :], rs_ref[...], 0), axis=0, keepdims=True)
        o_ref[k:k + 1, :] = start + ids[k + 2:k + 3, :]


def _dest_rows(ids, row_start):
    n = ids.shape[1]
    tn = min(4096, n)
    return pl.pallas_call(
        _dest_body, grid=(n // tn,),
        in_specs=[pl.BlockSpec((4, tn), lambda i: (0, i)), pl.BlockSpec((N_EXPERTS, 1), lambda i: (0, 0))],
        out_specs=pl.BlockSpec((2, tn), lambda i: (0, i)),
        out_shape=jax.ShapeDtypeStruct((2, n), jnp.int32),
        compiler_params=_params(("arbitrary",)), name="dest_rows",
    )(ids, row_start[:, None])


def _routing_tables(ids, counts, n, max_pairs):
    te = EXPERT_TILE
    n_tiles = 2 * n // te
    i32 = jnp.int32
    row_end = jnp.cumsum(counts)
    row_start = row_end - counts
    dest = _dest_rows(ids, row_start)
    dest1, dest2 = dest[0], dest[1]
    ne_cum = jnp.cumsum((counts > 0).astype(i32))
    tile_lo = jnp.arange(n_tiles, dtype=i32) * te
    first_e = jnp.sum((row_end[None, :] <= tile_lo[:, None]).astype(i32), axis=1)
    last_e = jnp.sum((row_end[None, :] <= (tile_lo + te - 1)[:, None]).astype(i32), axis=1)
    pairs_t = ne_cum[last_e] - ne_cum[first_e] + 1
    pair_end = jnp.cumsum(pairs_t)
    pair_start = pair_end - pairs_t
    n_pairs = pair_end[-1]
    p = jnp.minimum(jnp.arange(max_pairs, dtype=i32), n_pairs - 1)
    p_tile = jnp.sum((pair_end[None, :] <= p[:, None]).astype(i32), axis=1)
    rank = ne_cum[first_e[p_tile]] - 1 + (p - pair_start[p_tile])
    p_exp = jnp.sum((ne_cum[None, :] <= rank[:, None]).astype(i32), axis=1)
    p_lo = jnp.maximum(row_start[p_exp], p_tile * te) - p_tile * te
    p_hi = jnp.minimum(row_end[p_exp], (p_tile + 1) * te) - p_tile * te
    p_next = jnp.where(rank + 1 < ne_cum[-1],
                       jnp.sum((ne_cum[None, :] <= (rank + 1)[:, None]).astype(i32), axis=1), -1)
    p_slot = rank & 1
    return (dest1, dest2, p_tile.astype(i32), p_exp.astype(i32), p_lo.astype(i32), p_hi.astype(i32),
            n_pairs.reshape(1).astype(i32), p_next.astype(i32), p_slot.astype(i32))


def _row_copy(src, dst, sem):
    return pltpu.make_async_copy(src, dst, sem)


def _expert_body(ptile_ref, pexp_ref, plo_ref, phi_ref, np_ref, pnext_ref, pslot_ref, d1_ref, d2_ref,
                 x1_hbm, w1_hbm, w3_hbm, w2_hbm, y2_hbm,
                 w1b, w3b, w2b, wf1, wf3, wf2, xbuf0, xbuf1, xbuf2, ybuf0, ybuf1, ybuf2, inv_s, gsem, ssem, wsem,
                 *, n, n_tiles):
    te = EXPERT_TILE
    p = pl.program_id(0)
    valid = p < np_ref[0]
    t = ptile_ref[p]
    pm1 = jnp.maximum(p - 1, 0)
    first = (p == 0) | (ptile_ref[pm1] != t)
    new_expert = (p == 0) | (pexp_ref[pm1] != pexp_ref[p])
    lo, hi = plo_ref[p], phi_ref[p]
    xbufs, ybufs = (xbuf0, xbuf1, xbuf2), (ybuf0, ybuf1, ybuf2)

    def gather(tile, slot, j, priority=0):
        v = inv_s[tile * te + j]
        tok = jnp.where(v >= n, v - n, v)
        _row_copy(x1_hbm.at[tok], xbufs[slot].at[pl.ds(j, 1)], gsem.at[slot]).start(priority=priority)

    def scatter(tile, slot, j, priority=0):
        _row_copy(ybufs[slot].at[pl.ds(j, 1)], y2_hbm.at[inv_s[tile * te + j]], ssem).start(priority=priority)

    def wait_gather(slot):
        _row_copy(x1_hbm.at[pl.ds(0, te), 0], xbufs[slot], gsem.at[slot]).wait()

    def wait_scatter(slot):
        _row_copy(ybufs[slot], y2_hbm.at[pl.ds(0, te), 0], ssem).wait()

    def issue_loop(fn, tile, slot):
        def one(j, c):
            fn(tile, slot, j)
            return c
        lax.fori_loop(0, te, one, 0, unroll=8)

    def weight_copies(e, slot):
        return [_row_copy(w1_hbm.at[e], wf1.at[slot], wsem.at[slot]),
                _row_copy(w3_hbm.at[e], wf3.at[slot], wsem.at[slot]),
                _row_copy(w2_hbm.at[e], wf2.at[slot], wsem.at[slot])]

    @pl.when(p == 0)
    def _():
        for c in weight_copies(pexp_ref[0], pslot_ref[0]):
            c.start()

        def invert(tk, c):
            inv_s[d1_ref[tk]] = tk
            inv_s[d2_ref[tk]] = tk + n
            return c
        lax.fori_loop(0, n, invert, 0, unroll=8)
        for yb in ybufs:
            yb[...] = jnp.zeros_like(yb)
        issue_loop(gather, 0, 0)
        issue_loop(gather, 1, 1)

    @pl.when(valid & new_expert)
    def _():
        slot = pslot_ref[p]
        for c in weight_copies(pexp_ref[p], slot):
            c.wait()
        w1b[...] = wf1[slot].astype(BF16)
        w3b[...] = wf3[slot].astype(BF16)
        w2b[...] = wf2[slot].astype(BF16)

        @pl.when(pnext_ref[p] >= 0)
        def _():
            for c in weight_copies(pnext_ref[p], 1 - slot):
                c.start()

    def ffn(slot):
        xb = xbufs[slot][...].astype(BF16)
        a = _dot(xb, w1b[...])
        b = _dot(xb, w3b[...])
        y = _dot((a * jax.nn.sigmoid(a) * b).astype(BF16), w2b[...])
        rows = lax.broadcasted_iota(jnp.int32, (te, 1), 0)
        ybufs[slot][...] = jnp.where((rows >= lo) & (rows < hi), y, ybufs[slot][...])

    for r in range(3):
        ahead, behind = (r + 2) % 3, (r + 1) % 3
        mine = valid & (lax.rem(t, 3) == r)
        interior = first & (t >= 1) & (t < n_tiles - 2)

        @pl.when(mine & first)
        def _():
            wait_gather(r)

            @pl.when(t >= 2)
            def _():
                wait_scatter(behind)

        @pl.when(mine & interior)
        def _():
            for j in range(te):
                gather(t + 2, ahead, j, priority=j % 2)
                scatter(t - 1, ahead, j, priority=j % 2)
            ffn(r)

        @pl.when(mine & jnp.logical_not(interior))
        def _():
            @pl.when(first & (t == 0))
            def _():
                issue_loop(gather, 2, ahead)

            @pl.when(first & (t >= n_tiles - 2))
            def _():
                issue_loop(scatter, t - 1, ahead)

            ffn(r)

    @pl.when(p == pl.num_programs(0) - 1)
    def _():
        last = n_tiles - 1
        wait_scatter((last - 1) % 3)
        issue_loop(scatter, last, last % 3)
        wait_scatter(last % 3)


def _experts(x1_rows, tables, w1, w3, w2):
    n, _, d = x1_rows.shape
    de = w1.shape[2]
    te = EXPERT_TILE
    n_tiles = 2 * n // te
    assert n_tiles >= 4
    max_pairs = n_tiles + N_EXPERTS - 1
    hbm = pl.BlockSpec(memory_space=pl.ANY)
    return pl.pallas_call(
        functools.partial(_expert_body, n=n, n_tiles=n_tiles),
        grid_spec=pltpu.PrefetchScalarGridSpec(
            num_scalar_prefetch=9, grid=(max_pairs,),
            in_specs=[hbm, hbm, hbm, hbm],
            out_specs=hbm,
            scratch_shapes=[pltpu.VMEM((d, de), BF16), pltpu.VMEM((d, de), BF16), pltpu.VMEM((de, d), BF16),
                            pltpu.VMEM((2, d, de), F32), pltpu.VMEM((2, d, de), F32), pltpu.VMEM((2, de, d), F32),
                            *[pltpu.VMEM((te, d), F32)] * 6,
                            pltpu.SMEM((2 * n,), jnp.int32),
                            pltpu.SemaphoreType.DMA((3,)), pltpu.SemaphoreType.DMA(()),
                            pltpu.SemaphoreType.DMA((2,))]),
        out_shape=jax.ShapeDtypeStruct((2 * n, 1, d), F32),
        compiler_params=_params(("arbitrary",)), name="experts",
    )(*tables, x1_rows, w1, w3, w2)


def _combine_body(x_ref, cw_ref, g_ref, b_ref, y2_hbm, o_ref, ybuf, sem, *, tm, n):
    i = pl.program_id(0)

    def copies(tile, buf):
        r0 = pl.multiple_of(tile * tm, tm)
        return [_row_copy(y2_hbm.at[pl.ds(k * n + r0, tm), 0], ybuf.at[buf, k], sem.at[buf]) for k in range(2)]

    @pl.when(i == 0)
    def _():
        for c in copies(0, 0):
            c.start()

    @pl.when(i + 1 < pl.num_programs(0))
    def _():
        for c in copies(i + 1, (i + 1) % 2):
            c.start()

    buf = i % 2
    for c in copies(i, buf):
        c.wait()
    y = cw_ref[:, 0:1] * ybuf[buf, 0] + cw_ref[:, 1:2] * ybuf[buf, 1]
    o_ref[...] = _layer_norm(DN_ALPHA * x_ref[...] + y, g_ref[...], b_ref[...])


def _combine(y2, x1, cw_col, ln_g, ln_b):
    n, d = x1.shape
    tm = min(512, n)
    tile = pl.BlockSpec((tm, d), lambda i: (i, 0))
    row = pl.BlockSpec((1, d), lambda i: (0, 0))
    return pl.pallas_call(
        functools.partial(_combine_body, tm=tm, n=n), grid=(n // tm,),
        in_specs=[tile, pl.BlockSpec((tm, 2), lambda i: (i, 0)), row, row, pl.BlockSpec(memory_space=pl.ANY)],
        out_specs=tile,
        out_shape=jax.ShapeDtypeStruct((n, d), F32),
        scratch_shapes=[pltpu.VMEM((2, 2, tm, d), F32), pltpu.SemaphoreType.DMA((2,))],
        compiler_params=_params(("arbitrary",)), name="combine_ln2",
    )(x1, cw_col, ln_g, ln_b, y2)


def _retention_consts():
    h, c = RET_HEADS, CHUNK
    log_gamma = jnp.log1p(-jnp.exp2(-5.0 - jnp.arange(h, dtype=F32)))
    idx = jnp.arange(c, dtype=F32)
    rel = idx[:, None] - idx[None, :]
    mask = jnp.where(rel >= 0, jnp.exp(log_gamma[:, None, None] * jnp.maximum(rel, 0.0)), 0.0)
    xi = jnp.exp(log_gamma[:, None] * (idx + 1.0))
    zeta = jnp.exp(log_gamma[:, None] * (c - 1.0 - idx))
    chunk_decay = jnp.exp(log_gamma * c)
    xi_b = jnp.broadcast_to(xi[:, :, None], (h, c, RET_QK_DIM))
    zeta_b = jnp.broadcast_to(zeta[:, :, None], (h, c, RET_QK_DIM))
    cd_b = jnp.broadcast_to(chunk_decay[:, None, None], (h, 1, RET_V_DIM))
    return mask, xi_b, zeta_b, cd_b


def _rope_consts():
    half = RET_QK_DIM // 2
    freq = ROPE_THETA ** (-jnp.arange(half, dtype=F32) / half)
    freq2 = jnp.concatenate([freq, freq])[None, :]
    sign2 = jnp.concatenate([-jnp.ones((half,), F32), jnp.ones((half,), F32)])[None, :]
    return freq2, sign2


def kernel(x, positions, w_in, b_gate, ret_gn_g, sgu_ln_g, sgu_ln_b, sgu_w, sgu_b, w_proj_ret, w_proj_sgu,
           w_out, ln1_g, ln1_b, w_group, b_group, w_er, b_er, w1, w3, w2, ln2_g, ln2_b):
    batch, seq, d = x.shape
    n = batch * seq
    qk_w, v_w = RET_HEADS * RET_QK_DIM, RET_HEADS * RET_V_DIM
    assert d == v_w == SGU_GROUPS * CHUNK and seq % CHUNK == 0
    ret_cols = 2 * qk_w + 2 * v_w
    max_pairs = (2 * n) // EXPERT_TILE + N_EXPERTS - 1

    freq2, sign2 = _rope_consts()
    pos_col = positions.reshape(n, 1)
    ret_consts = _retention_consts()

    xc = x.reshape(n, d)
    for l in range(w_in.shape[0]):
        wa = _regroup_retention_weights(w_in[l])
        y_a, xb = _retention_branch(xc, pos_col, freq2, sign2, wa, w_proj_ret[l].astype(BF16), ret_consts,
                                    ret_gn_g[l][None, :], batch, seq)
        gate_bias = jnp.concatenate([jnp.zeros((2 * d,), F32), b_gate[l].reshape(-1)])[None, :]
        h = _in_proj(xb, w_in[l], ret_cols, gate_bias)
        bias_b = jnp.broadcast_to(sgu_b[l][:, :, None], (SGU_GROUPS, CHUNK, CHUNK))
        merged = _sgu_merge(h, y_a, sgu_ln_g[l][None, :], sgu_ln_b[l][None, :], sgu_w[l], bias_b, w_proj_sgu[l])

        w_r = jnp.concatenate([w_er[l], w_group[l]], axis=1).T
        w_r = jnp.pad(w_r, ((0, ROUTER_ROWS - w_r.shape[0]), (0, 0)))
        w_hi = w_r.astype(BF16)
        w_lo = (w_r - w_hi.astype(F32)).astype(BF16)
        bias = jnp.pad(jnp.concatenate([b_er[l], b_group[l]]), (0, ROUTER_ROWS - N_EXPERTS - N_GROUPS))[:, None]
        x1, x1_rows, ids, cw, cnt = _out_proj_router(merged, xc, w_out[l], ln1_g[l][None, :],
                                                     ln1_b[l][None, :], w_hi, w_lo, bias)

        counts = cnt[:, 0].astype(jnp.int32)
        dest1, dest2, p_tile, p_exp, p_lo, p_hi, n_pairs, p_next, p_slot = _routing_tables(ids, counts, n, max_pairs)
        y2 = _experts(x1_rows, (p_tile, p_exp, p_lo, p_hi, n_pairs, p_next, p_slot, dest1, dest2),
                      w1[l], w3[l], w2[l])
        xc = _combine(y2, x1, cw.T, ln2_g[l][None, :], ln2_b[l][None, :])
    return xc.reshape(batch, seq, d)
```

```python
import functools

import numpy as np
import jax
import jax.numpy as jnp
from jax import lax
from jax.experimental import pallas as pl
from jax.experimental.pallas import tpu as pltpu

F32 = jnp.float32
BF16 = jnp.bfloat16

RET_HEADS = 8
RET_QK_DIM = 128
RET_V_DIM = 256
CHUNK = 128
ROPE_THETA = 10000.0
SGU_GROUPS = 16
N_GROUPS = 4
EXPERTS_PER_GROUP = 8
N_EXPERTS = N_GROUPS * EXPERTS_PER_GROUP
ROUTER_ROWS = 40
LN_EPS = 1e-5
DEPTH = 1
DN_ALPHA = (2 * DEPTH) ** 0.25
SQRT_HALF = np.sqrt(0.5).astype(np.float32)

LANES = 128
VMEM_LIMIT = 52 * 1024 * 1024
EXPERT_TILE = 256


def _params(sem, vmem=VMEM_LIMIT, flags=None):
    return pltpu.CompilerParams(dimension_semantics=sem, vmem_limit_bytes=vmem, flags=flags)


def _dot(a, b):
    return jnp.dot(a, b, preferred_element_type=F32)


def _dot_nt(a, b):
    return lax.dot_general(a, b, (((1,), (1,)), ((), ())), preferred_element_type=F32)


def _dot_tn(a, b):
    return lax.dot_general(a, b, (((0,), (0,)), ((), ())), preferred_element_type=F32)


def _layer_norm(r, g, b):
    mu = jnp.mean(r, axis=-1, keepdims=True)
    d = r - mu
    var = jnp.mean(d * d, axis=-1, keepdims=True)
    return d * lax.rsqrt(var + LN_EPS) * g + b


def _gelu(x):
    return 0.5 * x * (1.0 + lax.erf(x * SQRT_HALF))


def _inproj_body(x_ref, w_ref, b_ref, o_ref, wb_ref, *, gelu_blocks):
    @pl.when(pl.program_id(1) == 0)
    def _():
        wb_ref[...] = w_ref[...].astype(BF16)

    @pl.when(pl.program_id(0) < gelu_blocks)
    def _():
        o_ref[...] = _gelu(_dot(x_ref[...], wb_ref[...])).astype(BF16)

    @pl.when(pl.program_id(0) >= gelu_blocks)
    def _():
        o_ref[...] = jax.nn.sigmoid(_dot(x_ref[...], wb_ref[...]) + b_ref[...]).astype(BF16)


def _in_proj(xb, w, col0, gate_bias):
    n, d = xb.shape
    width = w.shape[1] - col0
    tm, tn = min(1024, n), 1024
    assert col0 % tn == 0 and width % (2 * tn) == 0
    return pl.pallas_call(
        functools.partial(_inproj_body, gelu_blocks=width // (2 * tn)),
        grid=(width // tn, n // tm),
        in_specs=[pl.BlockSpec((tm, d), lambda j, i: (i, 0)),
                  pl.BlockSpec((d, tn), lambda j, i: (0, j + col0 // tn)),
                  pl.BlockSpec((1, tn), lambda j, i: (0, j))],
        out_specs=pl.BlockSpec((tm, tn), lambda j, i: (i, j)),
        out_shape=jax.ShapeDtypeStruct((n, width), BF16),
        scratch_shapes=[pltpu.VMEM((d, tn), BF16)],
        compiler_params=_params(("arbitrary", "arbitrary")), name="in_proj",
    )(xb, w, gate_bias)


HEAD_COLS = 2 * RET_QK_DIM + 2 * RET_V_DIM


def _regroup_body(q_ref, k_ref, v_ref, g_ref, p_ref, o_ref, pb_ref):
    dk, dv = RET_QK_DIM, RET_V_DIM
    o_ref[0, :, 0:dk] = q_ref[...].astype(BF16)
    o_ref[0, :, dk:2 * dk] = k_ref[...].astype(BF16)
    o_ref[0, :, 2 * dk:2 * dk + dv] = v_ref[...].astype(BF16)
    o_ref[0, :, 2 * dk + dv:] = g_ref[...].astype(BF16)
    pb_ref[...] = p_ref[...].astype(BF16)


def _regroup_retention_weights(w, w_proj):
    d = w.shape[0]
    rows = w_proj.shape[0] // RET_HEADS
    k_blk = RET_HEADS
    v_blk = 2 * RET_HEADS * RET_QK_DIM // RET_V_DIM
    return pl.pallas_call(
        _regroup_body, grid=(RET_HEADS,),
        in_specs=[pl.BlockSpec((d, RET_QK_DIM), lambda h: (0, h)),
                  pl.BlockSpec((d, RET_QK_DIM), lambda h: (0, k_blk + h)),
                  pl.BlockSpec((d, RET_V_DIM), lambda h: (0, v_blk + h)),
                  pl.BlockSpec((d, RET_V_DIM), lambda h: (0, v_blk + RET_HEADS + h)),
                  pl.BlockSpec((rows, w_proj.shape[1]), lambda h: (h, 0))],
        out_specs=[pl.BlockSpec((1, d, HEAD_COLS), lambda h: (h, 0, 0)),
                   pl.BlockSpec((rows, w_proj.shape[1]), lambda h: (h, 0))],
        out_shape=[jax.ShapeDtypeStruct((RET_HEADS, d, HEAD_COLS), BF16),
                   jax.ShapeDtypeStruct(w_proj.shape, BF16)],
        compiler_params=_params(("arbitrary",)), name="regroup_retention_weights",
    )(w, w, w, w, w_proj)


def _retention_body(x_ref, pos_ref, freq_ref, sign_ref, wa_ref, wp_ref, mask_ref, xi_ref, zeta_ref, cd_ref,
                    gain_ref, o_ref, xb_ref, state_ref, yg_ref, cos_ref, sin_ref, *, n_chunks):
    @pl.when(pl.program_id(1) == 0)
    def _():
        state_ref[...] = jnp.zeros_like(state_ref)

    ang = pos_ref[...].astype(F32) * freq_ref[...]
    cos_ref[...] = jnp.cos(ang)
    sin_ref[...] = jnp.sin(ang) * sign_ref[...]

    scale = RET_QK_DIM ** -0.5
    half = RET_QK_DIM // 2
    dk, dv = RET_QK_DIM, RET_V_DIM
    xb = x_ref[...].astype(BF16)
    xb_ref[...] = xb
    ahead = 1
    projs = [_dot(xb, wa_ref[h]) for h in range(ahead)]
    for hd in range(RET_HEADS):
        proj = projs[hd]
        if hd + ahead < RET_HEADS:
            projs.append(_dot(xb, wa_ref[hd + ahead]))
        gain = gain_ref[:, hd * dv:(hd + 1) * dv]
        for c in range(n_chunks):
            lo, hi = c * CHUNK, (c + 1) * CHUNK
            cos, sin = cos_ref[lo:hi, :], sin_ref[lo:hi, :]
            q, k = proj[lo:hi, 0:dk], proj[lo:hi, dk:2 * dk]
            vb = proj[lo:hi, 2 * dk:2 * dk + dv].astype(BF16)
            g = proj[lo:hi, 2 * dk + dv:]
            qr = q * cos + pltpu.roll(q, half, 1) * sin
            kr = (k * cos + pltpu.roll(k, half, 1) * sin) * scale
            scores = _dot_nt(qr.astype(BF16), kr.astype(BF16)) * mask_ref[hd]
            state = state_ref[hd]
            lhs = jnp.concatenate([scores.astype(BF16), (qr * xi_ref[hd]).astype(BF16)], axis=1)
            rhs = jnp.concatenate([vb, state.astype(BF16)], axis=0)
            out = _dot(lhs, rhs)
            state_ref[hd] = state * cd_ref[hd] + _dot_tn((kr * zeta_ref[hd]).astype(BF16), vb)
            mu = jnp.mean(out, axis=-1, keepdims=True)
            d = out - mu
            var = jnp.mean(d * d, axis=-1, keepdims=True)
            yn = d * lax.rsqrt(var + LN_EPS) * gain
            yg_ref[lo:hi, hd * dv:(hd + 1) * dv] = (g * jax.nn.sigmoid(g) * yn).astype(BF16)
    o_ref[...] = _dot(yg_ref[...], wp_ref[...]).astype(BF16)


def _retention_branch(x2, pos_col, freq2, sign2, wa_bf, wp_bf, consts, gain, batch, seq):
    n, d = x2.shape
    tr = min(256, seq)
    nr = seq // tr
    mask, xi_b, zeta_b, cd_b = consts
    vw = RET_HEADS * RET_V_DIM

    def once(shape):
        return pl.BlockSpec(shape, lambda b, r: (0,) * len(shape), pipeline_mode=pl.Buffered(1))

    rows = lambda w: pl.BlockSpec((tr, w), lambda b, r: (b * nr + r, 0))
    return pl.pallas_call(
        functools.partial(_retention_body, n_chunks=tr // CHUNK),
        grid=(batch, nr),
        in_specs=[rows(d), rows(1), once((1, LANES)), once((1, LANES)),
                  once((RET_HEADS, d, HEAD_COLS)), once((vw, d)),
                  once((RET_HEADS, CHUNK, CHUNK)), once((RET_HEADS, CHUNK, RET_QK_DIM)),
                  once((RET_HEADS, CHUNK, RET_QK_DIM)), once((RET_HEADS, 1, RET_V_DIM)), once((1, vw))],
        out_specs=[rows(d), rows(d)],
        out_shape=[jax.ShapeDtypeStruct((n, d), BF16), jax.ShapeDtypeStruct((n, d), BF16)],
        scratch_shapes=[pltpu.VMEM((RET_HEADS, RET_QK_DIM, RET_V_DIM), F32), pltpu.VMEM((tr, vw), BF16),
                        pltpu.VMEM((tr, LANES), F32), pltpu.VMEM((tr, LANES), F32)],
        compiler_params=_params(("arbitrary", "arbitrary")), name="retention_branch",
    )(x2, pos_col, freq2, sign2, wa_bf, wp_bf, mask, xi_b, zeta_b, cd_b, gain)


def _sgu_merge_body(u_ref, v_ref, sa_ref, sb_ref, ya_ref, lng_ref, lnb_ref, ws_ref, bs_ref, wp_ref,
                    o_ref, wtril_ref, ys_ref, wpb_ref, *, n_chunks):
    @pl.when(pl.program_id(0) == 0)
    def _():
        r = lax.broadcasted_iota(jnp.int32, (CHUNK, CHUNK), 0)
        c = lax.broadcasted_iota(jnp.int32, (CHUNK, CHUNK), 1)
        for g in range(SGU_GROUPS):
            wtril_ref[g] = jnp.where(r >= c, ws_ref[g], 0.0).astype(BF16)
        wpb_ref[...] = wp_ref[...].astype(BF16)

    vn = _layer_norm(v_ref[...].astype(F32), lng_ref[...], lnb_ref[...]).astype(BF16)
    for g in range(SGU_GROUPS):
        cols = pl.ds(g * CHUNK, CHUNK)
        lo = g * CHUNK
        rhs = jnp.concatenate([vn[c * CHUNK:(c + 1) * CHUNK, lo:lo + CHUNK] for c in range(n_chunks)], axis=1)
        mix = _dot(wtril_ref[g], rhs)
        bias = bs_ref[g]
        for c in range(n_chunks):
            rows = pl.ds(c * CHUNK, CHUNK)
            u = u_ref[rows, cols].astype(F32)
            ys_ref[rows, cols] = (u * (mix[:, c * CHUNK:(c + 1) * CHUNK] + bias)).astype(BF16)

    yb = _dot(ys_ref[...], wpb_ref[...])
    o_ref[...] = (sa_ref[...].astype(F32) * ya_ref[...].astype(F32) + sb_ref[...].astype(F32) * yb).astype(BF16)


def _sgu_merge(h, y_a, ln_g, ln_b, w_s, b_s_b, wp):
    n, d = y_a.shape
    ts = min(256, n)
    tile = lambda blk: pl.BlockSpec((ts, d), lambda i: (i, blk))
    row = pl.BlockSpec((1, d), lambda i: (0, 0))
    full3 = pl.BlockSpec((SGU_GROUPS, CHUNK, CHUNK), lambda i: (0, 0, 0))
    return pl.pallas_call(
        functools.partial(_sgu_merge_body, n_chunks=ts // CHUNK),
        grid=(n // ts,),
        in_specs=[tile(0), tile(1), tile(2), tile(3), tile(0), row, row, full3, full3,
                  pl.BlockSpec((d, d), lambda i: (0, 0), pipeline_mode=pl.Buffered(1))],
        out_specs=tile(0),
        out_shape=jax.ShapeDtypeStruct((n, d), BF16),
        scratch_shapes=[pltpu.VMEM((SGU_GROUPS, CHUNK, CHUNK), BF16), pltpu.VMEM((ts, d), BF16),
                        pltpu.VMEM((d, d), BF16)],
        compiler_params=_params(("arbitrary",)), name="sgu_merge",
    )(h, h, h, h, y_a, ln_g, ln_b, w_s, b_s_b, wp)


def _outproj_router_body(m_ref, x_ref, w_ref, g_ref, b_ref, whi_ref, wlo_ref, rb_ref,
                         o_ref, orow_hbm, ids_ref, cw_ref, cnt_ref, carry_ref, tri_ref, wb_ref, rsem, *, tm):
    step = pl.program_id(0)

    @pl.when(step == 0)
    def _():
        carry_ref[...] = jnp.zeros_like(carry_ref)
        r = lax.broadcasted_iota(jnp.int32, (tm, tm), 0)
        c = lax.broadcasted_iota(jnp.int32, (tm, tm), 1)
        tri_ref[...] = jnp.where(r < c, 1.0, 0.0).astype(BF16)
        wb_ref[...] = w_ref[...].astype(BF16)

    r = DN_ALPHA * x_ref[...] + _dot(m_ref[...], wb_ref[...])
    x = _layer_norm(r, g_ref[...], b_ref[...])
    o_ref[...] = x

    row_copy = _row_copy(o_ref, orow_hbm.at[pl.ds(pl.multiple_of(step * tm, tm), tm), 0], rsem)
    row_copy.start()

    hi = x.astype(BF16)
    lo = (x - hi.astype(F32)).astype(BF16)
    whi = whi_ref[...]
    logits = _dot_nt(whi, hi) + _dot_nt(whi, lo) + _dot_nt(wlo_ref[...], hi) + rb_ref[...]
    el = logits[0:N_EXPERTS, :]
    gl = logits[N_EXPERTS:N_EXPERTS + N_GROUPS, :]

    gi = lax.broadcasted_iota(jnp.int32, gl.shape, 0)
    gmax = jnp.max(gl, axis=0, keepdims=True)
    gidx = jnp.min(jnp.where(gl == gmax, gi, N_GROUPS), axis=0, keepdims=True)
    p_group = 1.0 / jnp.sum(jnp.exp(gl - gmax), axis=0, keepdims=True)

    ei = lax.broadcasted_iota(jnp.int32, el.shape, 0)
    first = gidx * EXPERTS_PER_GROUP
    in_group = (ei >= first) & (ei < first + EXPERTS_PER_GROUP)
    m1 = jnp.where(in_group, el, -jnp.inf)
    v1 = jnp.max(m1, axis=0, keepdims=True)
    i1 = jnp.min(jnp.where(in_group & (m1 == v1), ei, N_EXPERTS), axis=0, keepdims=True)
    rest = in_group & (ei != i1)
    m2 = jnp.where(rest, el, -jnp.inf)
    v2 = jnp.max(m2, axis=0, keepdims=True)
    i2 = jnp.min(jnp.where(rest & (m2 == v2), ei, N_EXPERTS), axis=0, keepdims=True)
    t = jnp.exp(v2 - v1)
    p1 = 1.0 / (1.0 + t)
    p2 = t * p1

    sel1, sel2 = ei == i1, ei == i2
    onehot = jnp.where(sel1 | sel2, 1.0, 0.0)
    before = _dot(onehot.astype(BF16), tri_ref[...]) + carry_ref[:, 0:1]
    r1 = jnp.sum(jnp.where(sel1, before, 0.0), axis=0, keepdims=True)
    r2 = jnp.sum(jnp.where(sel2, before, 0.0), axis=0, keepdims=True)
    carry_ref[...] = carry_ref[...] + jnp.sum(onehot, axis=1, keepdims=True)

    ids_ref[0:1, :] = i1
    ids_ref[1:2, :] = i2
    ids_ref[2:3, :] = r1.astype(jnp.int32)
    ids_ref[3:4, :] = r2.astype(jnp.int32)
    cw_ref[0:1, :] = p_group * p1
    cw_ref[1:2, :] = p_group * p2
    cnt_ref[...] = carry_ref[...]
    row_copy.wait()


def _out_proj_router(merged, x2, w_out, ln_g, ln_b, w_hi, w_lo, bias_col):
    n, d = x2.shape
    tm = min(512, n)
    tile = pl.BlockSpec((tm, d), lambda i: (i, 0))
    row = pl.BlockSpec((1, d), lambda i: (0, 0))
    full = pl.BlockSpec((ROUTER_ROWS, d), lambda i: (0, 0))
    return pl.pallas_call(
        functools.partial(_outproj_router_body, tm=tm), grid=(n // tm,),
        in_specs=[tile, tile, pl.BlockSpec((d, d), lambda i: (0, 0), pipeline_mode=pl.Buffered(1)), row, row,
                  full, full, pl.BlockSpec((ROUTER_ROWS, 1), lambda i: (0, 0))],
        out_specs=[tile, pl.BlockSpec(memory_space=pl.ANY),
                   pl.BlockSpec((4, tm), lambda i: (0, i)),
                   pl.BlockSpec((2, tm), lambda i: (0, i)),
                   pl.BlockSpec((N_EXPERTS, LANES), lambda i: (0, 0))],
        out_shape=[jax.ShapeDtypeStruct((n, d), F32), jax.ShapeDtypeStruct((n, 1, d), F32),
                   jax.ShapeDtypeStruct((4, n), jnp.int32),
                   jax.ShapeDtypeStruct((2, n), F32),
                   jax.ShapeDtypeStruct((N_EXPERTS, LANES), F32)],
        scratch_shapes=[pltpu.VMEM((N_EXPERTS, LANES), F32), pltpu.VMEM((tm, tm), BF16),
                        pltpu.VMEM((d, d), BF16), pltpu.SemaphoreType.DMA(())],
        compiler_params=_params(("arbitrary",)), name="out_proj_ln1_router",
    )(merged, x2, w_out, ln_g, ln_b, w_hi, w_lo, bias_col)


def _dest_body(ids_ref, rs_ref, o_ref):
    ids = ids_ref[...]
    e = lax.broadcasted_iota(jnp.int32, (N_EXPERTS, ids.shape[1]), 0)
    for k in range(2):
        start = jnp.sum(jnp.where(e == ids[k:k + 1, :], rs_ref[...], 0), axis=0, keepdims=True)
        o_ref[k:k + 1, :] = start + ids[k + 2:k + 3, :]


def _dest_rows(ids, row_start):
    n = ids.shape[1]
    tn = min(4096, n)
    return pl.pallas_call(
        _dest_body, grid=(n // tn,),
        in_specs=[pl.BlockSpec((4, tn), lambda i: (0, i)), pl.BlockSpec((N_EXPERTS, 1), lambda i: (0, 0))],
        out_specs=pl.BlockSpec((2, tn), lambda i: (0, i)),
        out_shape=jax.ShapeDtypeStruct((2, n), jnp.int32),
        compiler_params=_params(("arbitrary",)), name="dest_rows",
    )(ids, row_start[:, None])


def _routing_tables(ids, counts, n, max_pairs):
    te = EXPERT_TILE
    n_tiles = 2 * n // te
    i32 = jnp.int32
    row_end = jnp.cumsum(counts)
    row_start = row_end - counts
    dest = _dest_rows(ids, row_start)
    dest1, dest2 = dest[0], dest[1]
    ne_cum = jnp.cumsum((counts > 0).astype(i32))
    tile_lo = jnp.arange(n_tiles, dtype=i32) * te
    first_e = jnp.sum((row_end[None, :] <= tile_lo[:, None]).astype(i32), axis=1)
    last_e = jnp.sum((row_end[None, :] <= (tile_lo + te - 1)[:, None]).astype(i32), axis=1)
    pairs_t = ne_cum[last_e] - ne_cum[first_e] + 1
    pair_end = jnp.cumsum(pairs_t)
    pair_start = pair_end - pairs_t
    n_pairs = pair_end[-1]
    p = jnp.minimum(jnp.arange(max_pairs, dtype=i32), n_pairs - 1)
    p_tile = jnp.sum((pair_end[None, :] <= p[:, None]).astype(i32), axis=1)
    rank = ne_cum[first_e[p_tile]] - 1 + (p - pair_start[p_tile])
    p_exp = jnp.sum((ne_cum[None, :] <= rank[:, None]).astype(i32), axis=1)
    p_lo = jnp.maximum(row_start[p_exp], p_tile * te) - p_tile * te
    p_hi = jnp.minimum(row_end[p_exp], (p_tile + 1) * te) - p_tile * te
    p_next = jnp.where(rank + 1 < ne_cum[-1],
                       jnp.sum((ne_cum[None, :] <= (rank + 1)[:, None]).astype(i32), axis=1), -1)
    p_slot = rank & 1
    return (dest1, dest2, p_tile.astype(i32), p_exp.astype(i32), p_lo.astype(i32), p_hi.astype(i32),
            n_pairs.reshape(1).astype(i32), p_next.astype(i32), p_slot.astype(i32))


def _row_copy(src, dst, sem):
    return pltpu.make_async_copy(src, dst, sem)


def _expert_body(ptile_ref, pexp_ref, plo_ref, phi_ref, np_ref, pnext_ref, pslot_ref, d1_ref, d2_ref,
                 x1_hbm, w1_hbm, w3_hbm, w2_hbm, y2_hbm,
                 w1b, w3b, w2b, wf1, wf3, wf2, xbuf0, xbuf1, xbuf2, ybuf0, ybuf1, ybuf2, inv_s, gsem, ssem, wsem,
                 *, n, n_tiles):
    te = EXPERT_TILE
    p = pl.program_id(0)
    valid = p < np_ref[0]
    t = ptile_ref[p]
    pm1 = jnp.maximum(p - 1, 0)
    first = (p == 0) | (ptile_ref[pm1] != t)
    new_expert = (p == 0) | (pexp_ref[pm1] != pexp_ref[p])
    lo, hi = plo_ref[p], phi_ref[p]
    xbufs, ybufs = (xbuf0, xbuf1, xbuf2), (ybuf0, ybuf1, ybuf2)

    def gather(tile, slot, j, priority=0):
        v = inv_s[tile * te + j]
        tok = jnp.where(v >= n, v - n, v)
        _row_copy(x1_hbm.at[tok], xbufs[slot].at[pl.ds(j, 1)], gsem.at[slot]).start(priority=priority)

    def scatter(tile, slot, j, priority=0):
        _row_copy(ybufs[slot].at[pl.ds(j, 1)], y2_hbm.at[inv_s[tile * te + j]], ssem).start(priority=priority)

    def wait_gather(slot):
        _row_copy(x1_hbm.at[pl.ds(0, te), 0], xbufs[slot], gsem.at[slot]).wait()

    def wait_scatter(slot):
        _row_copy(ybufs[slot], y2_hbm.at[pl.ds(0, te), 0], ssem).wait()

    def issue_loop(fn, tile, slot):
        def one(j, c):
            fn(tile, slot, j)
            return c
        lax.fori_loop(0, te, one, 0, unroll=8)

    def weight_copies(e, slot):
        return [_row_copy(w1_hbm.at[e], wf1.at[slot], wsem.at[slot]),
                _row_copy(w3_hbm.at[e], wf3.at[slot], wsem.at[slot]),
                _row_copy(w2_hbm.at[e], wf2.at[slot], wsem.at[slot])]

    @pl.when(p == 0)
    def _():
        for c in weight_copies(pexp_ref[0], pslot_ref[0]):
            c.start()

        def invert(tk, c):
            inv_s[d1_ref[tk]] = tk
            inv_s[d2_ref[tk]] = tk + n
            return c
        lax.fori_loop(0, n, invert, 0, unroll=8)
        for yb in ybufs:
            yb[...] = jnp.zeros_like(yb)
        issue_loop(gather, 0, 0)
        issue_loop(gather, 1, 1)

    @pl.when(valid & new_expert)
    def _():
        slot = pslot_ref[p]
        for c in weight_copies(pexp_ref[p], slot):
            c.wait()
        w1b[...] = wf1[slot].astype(BF16)
        w3b[...] = wf3[slot].astype(BF16)
        w2b[...] = wf2[slot].astype(BF16)

        @pl.when(pnext_ref[p] >= 0)
        def _():
            for c in weight_copies(pnext_ref[p], 1 - slot):
                c.start()

    def ffn(slot):
        xb = xbufs[slot][...].astype(BF16)
        a = _dot(xb, w1b[...])
        b = _dot(xb, w3b[...])
        y = _dot((a * jax.nn.sigmoid(a) * b).astype(BF16), w2b[...])
        rows = lax.broadcasted_iota(jnp.int32, (te, 1), 0)
        ybufs[slot][...] = jnp.where((rows >= lo) & (rows < hi), y, ybufs[slot][...])

    for r in range(3):
        ahead, behind = (r + 2) % 3, (r + 1) % 3
        mine = valid & (lax.rem(t, 3) == r)
        interior = first & (t >= 1) & (t < n_tiles - 2)

        @pl.when(mine & first)
        def _():
            wait_gather(r)

            @pl.when(t >= 2)
            def _():
                wait_scatter(behind)

        @pl.when(mine & interior)
        def _():
            for j in range(te):
                gather(t + 2, ahead, j, priority=j % 2)
                scatter(t - 1, ahead, j, priority=j % 2)
            ffn(r)

        @pl.when(mine & jnp.logical_not(interior))
        def _():
            @pl.when(first & (t == 0))
            def _():
                issue_loop(gather, 2, ahead)

            @pl.when(first & (t >= n_tiles - 2))
            def _():
                issue_loop(scatter, t - 1, ahead)

            ffn(r)

    @pl.when(p == pl.num_programs(0) - 1)
    def _():
        last = n_tiles - 1
        wait_scatter((last - 1) % 3)
        issue_loop(scatter, last, last % 3)
        wait_scatter(last % 3)


def _experts(x1_rows, tables, w1, w3, w2):
    n, _, d = x1_rows.shape
    de = w1.shape[2]
    te = EXPERT_TILE
    n_tiles = 2 * n // te
    assert n_tiles >= 4
    max_pairs = n_tiles + N_EXPERTS - 1
    hbm = pl.BlockSpec(memory_space=pl.ANY)
    return pl.pallas_call(
        functools.partial(_expert_body, n=n, n_tiles=n_tiles),
        grid_spec=pltpu.PrefetchScalarGridSpec(
            num_scalar_prefetch=9, grid=(max_pairs,),
            in_specs=[hbm, hbm, hbm, hbm],
            out_specs=hbm,
            scratch_shapes=[pltpu.VMEM((d, de), BF16), pltpu.VMEM((d, de), BF16), pltpu.VMEM((de, d), BF16),
                            pltpu.VMEM((2, d, de), F32), pltpu.VMEM((2, d, de), F32), pltpu.VMEM((2, de, d), F32),
                            *[pltpu.VMEM((te, d), F32)] * 6,
                            pltpu.SMEM((2 * n,), jnp.int32),
                            pltpu.SemaphoreType.DMA((3,)), pltpu.SemaphoreType.DMA(()),
                            pltpu.SemaphoreType.DMA((2,))]),
        out_shape=jax.ShapeDtypeStruct((2 * n, 1, d), F32),
        compiler_params=_params(("arbitrary",)), name="experts",
    )(*tables, x1_rows, w1, w3, w2)


def _combine_body(x_ref, cw_ref, g_ref, b_ref, y2_hbm, o_ref, ybuf, sem, *, tm, n):
    i = pl.program_id(0)

    def copies(tile, buf):
        r0 = pl.multiple_of(tile * tm, tm)
        return [_row_copy(y2_hbm.at[pl.ds(k * n + r0, tm), 0], ybuf.at[buf, k], sem.at[buf]) for k in range(2)]

    @pl.when(i == 0)
    def _():
        for c in copies(0, 0):
            c.start()

    @pl.when(i + 1 < pl.num_programs(0))
    def _():
        for c in copies(i + 1, (i + 1) % 2):
            c.start()

    buf = i % 2
    for c in copies(i, buf):
        c.wait()
    y = cw_ref[:, 0:1] * ybuf[buf, 0] + cw_ref[:, 1:2] * ybuf[buf, 1]
    o_ref[...] = _layer_norm(DN_ALPHA * x_ref[...] + y, g_ref[...], b_ref[...])


def _combine(y2, x1, cw_col, ln_g, ln_b):
    n, d = x1.shape
    tm = min(512, n)
    tile = pl.BlockSpec((tm, d), lambda i: (i, 0))
    row = pl.BlockSpec((1, d), lambda i: (0, 0))
    return pl.pallas_call(
        functools.partial(_combine_body, tm=tm, n=n), grid=(n // tm,),
        in_specs=[tile, pl.BlockSpec((tm, 2), lambda i: (i, 0)), row, row, pl.BlockSpec(memory_space=pl.ANY)],
        out_specs=tile,
        out_shape=jax.ShapeDtypeStruct((n, d), F32),
        scratch_shapes=[pltpu.VMEM((2, 2, tm, d), F32), pltpu.SemaphoreType.DMA((2,))],
        compiler_params=_params(("arbitrary",)), name="combine_ln2",
    )(x1, cw_col, ln_g, ln_b, y2)


def _retention_consts():
    h, c = RET_HEADS, CHUNK
    log_gamma = jnp.log1p(-jnp.exp2(-5.0 - jnp.arange(h, dtype=F32)))
    idx = jnp.arange(c, dtype=F32)
    rel = idx[:, None] - idx[None, :]
    mask = jnp.where(rel >= 0, jnp.exp(log_gamma[:, None, None] * jnp.maximum(rel, 0.0)), 0.0)
    xi = jnp.exp(log_gamma[:, None] * (idx + 1.0))
    zeta = jnp.exp(log_gamma[:, None] * (c - 1.0 - idx))
    chunk_decay = jnp.exp(log_gamma * c)
    xi_b = jnp.broadcast_to(xi[:, :, None], (h, c, RET_QK_DIM))
    zeta_b = jnp.broadcast_to(zeta[:, :, None], (h, c, RET_QK_DIM))
    cd_b = jnp.broadcast_to(chunk_decay[:, None, None], (h, 1, RET_V_DIM))
    return mask, xi_b, zeta_b, cd_b


def _rope_consts():
    half = RET_QK_DIM // 2
    freq = ROPE_THETA ** (-jnp.arange(half, dtype=F32) / half)
    freq2 = jnp.concatenate([freq, freq])[None, :]
    sign2 = jnp.concatenate([-jnp.ones((half,), F32), jnp.ones((half,), F32)])[None, :]
    return freq2, sign2


def kernel(x, positions, w_in, b_gate, ret_gn_g, sgu_ln_g, sgu_ln_b, sgu_w, sgu_b, w_proj_ret, w_proj_sgu,
           w_out, ln1_g, ln1_b, w_group, b_group, w_er, b_er, w1, w3, w2, ln2_g, ln2_b):
    batch, seq, d = x.shape
    n = batch * seq
    qk_w, v_w = RET_HEADS * RET_QK_DIM, RET_HEADS * RET_V_DIM
    assert d == v_w == SGU_GROUPS * CHUNK and seq % CHUNK == 0
    ret_cols = 2 * qk_w + 2 * v_w
    max_pairs = (2 * n) // EXPERT_TILE + N_EXPERTS - 1

    freq2, sign2 = _rope_consts()
    pos_col = positions.reshape(n, 1)
    ret_consts = _retention_consts()

    xc = x.reshape(n, d)
    for l in range(w_in.shape[0]):
        wa, wp_ret = _regroup_retention_weights(w_in[l], w_proj_ret[l])
        y_a, xb = _retention_branch(xc, pos_col, freq2, sign2, wa, wp_ret, ret_consts,
                                    ret_gn_g[l][None, :], batch, seq)
        gate_bias = jnp.concatenate([jnp.zeros((2 * d,), F32), b_gate[l].reshape(-1)])[None, :]
        h = _in_proj(xb, w_in[l], ret_cols, gate_bias)
        bias_b = jnp.broadcast_to(sgu_b[l][:, :, None], (SGU_GROUPS, CHUNK, CHUNK))
        merged = _sgu_merge(h, y_a, sgu_ln_g[l][None, :], sgu_ln_b[l][None, :], sgu_w[l], bias_b, w_proj_sgu[l])

        w_r = jnp.concatenate([w_er[l], w_group[l]], axis=1).T
        w_r = jnp.pad(w_r, ((0, ROUTER_ROWS - w_r.shape[0]), (0, 0)))
        w_hi = w_r.astype(BF16)
        w_lo = (w_r - w_hi.astype(F32)).astype(BF16)
        bias = jnp.pad(jnp.concatenate([b_er[l], b_group[l]]), (0, ROUTER_ROWS - N_EXPERTS - N_GROUPS))[:, None]
        x1, x1_rows, ids, cw, cnt = _out_proj_router(merged, xc, w_out[l], ln1_g[l][None, :],
                                                     ln1_b[l][None, :], w_hi, w_lo, bias)

        counts = cnt[:, 0].astype(jnp.int32)
        dest1, dest2, p_tile, p_exp, p_lo, p_hi, n_pairs, p_next, p_slot = _routing_tables(ids, counts, n, max_pairs)
        y2 = _experts(x1_rows, (p_tile, p_exp, p_lo, p_hi, n_pairs, p_next, p_slot, dest1, dest2),
                      w1[l], w3[l], w2[l])
        xc = _combine(y2, x1, cw.T, ln2_g[l][None, :], ln2_b[l][None, :])
    return xc.reshape(batch, seq, d)
```

```python
import functools

import numpy as np
import jax
import jax.numpy as jnp
from jax import lax
from jax.experimental import pallas as pl
from jax.experimental.pallas import tpu as pltpu

F32 = jnp.float32
BF16 = jnp.bfloat16

RET_HEADS = 8
RET_QK_DIM = 128
RET_V_DIM = 256
CHUNK = 128
ROPE_THETA = 10000.0
SGU_GROUPS = 16
N_GROUPS = 4
EXPERTS_PER_GROUP = 8
N_EXPERTS = N_GROUPS * EXPERTS_PER_GROUP
ROUTER_ROWS = 40
LN_EPS = 1e-5
DEPTH = 1
DN_ALPHA = (2 * DEPTH) ** 0.25
SQRT_HALF = np.sqrt(0.5).astype(np.float32)

LANES = 128
VMEM_LIMIT = 52 * 1024 * 1024
EXPERT_TILE = 256


def _params(sem, vmem=VMEM_LIMIT, flags=None):
    return pltpu.CompilerParams(dimension_semantics=sem, vmem_limit_bytes=vmem, flags=flags)


def _dot(a, b):
    return jnp.dot(a, b, preferred_element_type=F32)


def _dot_nt(a, b):
    return lax.dot_general(a, b, (((1,), (1,)), ((), ())), preferred_element_type=F32)


def _dot_tn(a, b):
    return lax.dot_general(a, b, (((0,), (0,)), ((), ())), preferred_element_type=F32)


def _layer_norm(r, g, b):
    mu = jnp.mean(r, axis=-1, keepdims=True)
    d = r - mu
    var = jnp.mean(d * d, axis=-1, keepdims=True)
    return d * lax.rsqrt(var + LN_EPS) * g + b


def _gelu(x):
    return 0.5 * x * (1.0 + lax.erf(x * SQRT_HALF))


def _inproj_body(x_ref, w_ref, b_ref, o_ref, wb_ref, *, gelu_blocks):
    @pl.when(pl.program_id(1) == 0)
    def _():
        wb_ref[...] = w_ref[...].astype(BF16)

    @pl.when(pl.program_id(0) < gelu_blocks)
    def _():
        o_ref[...] = _gelu(_dot(x_ref[...], wb_ref[...])).astype(BF16)

    @pl.when(pl.program_id(0) >= gelu_blocks)
    def _():
        o_ref[...] = jax.nn.sigmoid(_dot(x_ref[...], wb_ref[...]) + b_ref[...]).astype(BF16)


def _in_proj(xb, w, col0, gate_bias):
    n, d = xb.shape
    width = w.shape[1] - col0
    tm, tn = min(1024, n), 1024
    assert col0 % tn == 0 and width % (2 * tn) == 0
    return pl.pallas_call(
        functools.partial(_inproj_body, gelu_blocks=width // (2 * tn)),
        grid=(width // tn, n // tm),
        in_specs=[pl.BlockSpec((tm, d), lambda j, i: (i, 0)),
                  pl.BlockSpec((d, tn), lambda j, i: (0, j + col0 // tn)),
                  pl.BlockSpec((1, tn), lambda j, i: (0, j))],
        out_specs=pl.BlockSpec((tm, tn), lambda j, i: (i, j)),
        out_shape=jax.ShapeDtypeStruct((n, width), BF16),
        scratch_shapes=[pltpu.VMEM((d, tn), BF16)],
        compiler_params=_params(("arbitrary", "arbitrary")), name="in_proj",
    )(xb, w, gate_bias)


HEAD_COLS = 2 * RET_QK_DIM + 2 * RET_V_DIM


def _regroup_body(q_ref, k_ref, v_ref, g_ref, p_ref, o_ref, pb_ref):
    dk, dv = RET_QK_DIM, RET_V_DIM
    o_ref[0, :, 0:dk] = q_ref[...].astype(BF16)
    o_ref[0, :, dk:2 * dk] = k_ref[...].astype(BF16)
    o_ref[0, :, 2 * dk:2 * dk + dv] = v_ref[...].astype(BF16)
    o_ref[0, :, 2 * dk + dv:] = g_ref[...].astype(BF16)
    pb_ref[...] = p_ref[...].astype(BF16)


def _regroup_retention_weights(w, w_proj):
    d = w.shape[0]
    rows = w_proj.shape[0] // RET_HEADS
    k_blk = RET_HEADS
    v_blk = 2 * RET_HEADS * RET_QK_DIM // RET_V_DIM
    return pl.pallas_call(
        _regroup_body, grid=(RET_HEADS,),
        in_specs=[pl.BlockSpec((d, RET_QK_DIM), lambda h: (0, h)),
                  pl.BlockSpec((d, RET_QK_DIM), lambda h: (0, k_blk + h)),
                  pl.BlockSpec((d, RET_V_DIM), lambda h: (0, v_blk + h)),
                  pl.BlockSpec((d, RET_V_DIM), lambda h: (0, v_blk + RET_HEADS + h)),
                  pl.BlockSpec((rows, w_proj.shape[1]), lambda h: (h, 0))],
        out_specs=[pl.BlockSpec((1, d, HEAD_COLS), lambda h: (h, 0, 0)),
                   pl.BlockSpec((rows, w_proj.shape[1]), lambda h: (h, 0))],
        out_shape=[jax.ShapeDtypeStruct((RET_HEADS, d, HEAD_COLS), BF16),
                   jax.ShapeDtypeStruct(w_proj.shape, BF16)],
        compiler_params=_params(("arbitrary",)), name="regroup_retention_weights",
    )(w, w, w, w, w_proj)


def _retention_body(x_ref, pos_ref, freq_ref, sign_ref, wa_ref, wp_ref, mask_ref, xi_ref, zeta_ref, cd_ref,
                    gain_ref, o_ref, xb_ref, state_ref, yg_ref, cos_ref, sin_ref, *, n_chunks):
    @pl.when(pl.program_id(1) == 0)
    def _():
        state_ref[...] = jnp.zeros_like(state_ref)

    ang = pos_ref[...].astype(F32) * freq_ref[...]
    cos_ref[...] = jnp.cos(ang)
    sin_ref[...] = jnp.sin(ang) * sign_ref[...]

    scale = RET_QK_DIM ** -0.5
    half = RET_QK_DIM // 2
    dk, dv = RET_QK_DIM, RET_V_DIM
    xb = x_ref[...].astype(BF16)
    xb_ref[...] = xb
    ahead = 1
    projs = [_dot(xb, wa_ref[h]) for h in range(ahead)]
    for hd in range(RET_HEADS):
        proj = projs[hd]
        if hd + ahead < RET_HEADS:
            projs.append(_dot(xb, wa_ref[hd + ahead]))
        gain = gain_ref[:, hd * dv:(hd + 1) * dv]
        for c in range(n_chunks):
            lo, hi = c * CHUNK, (c + 1) * CHUNK
            cos, sin = cos_ref[lo:hi, :], sin_ref[lo:hi, :]
            q, k = proj[lo:hi, 0:dk], proj[lo:hi, dk:2 * dk]
            vb = proj[lo:hi, 2 * dk:2 * dk + dv].astype(BF16)
            g = proj[lo:hi, 2 * dk + dv:]
            qr = q * cos + pltpu.roll(q, half, 1) * sin
            kr = (k * cos + pltpu.roll(k, half, 1) * sin) * scale
            scores = _dot_nt(qr.astype(BF16), kr.astype(BF16)) * mask_ref[hd]
            state = state_ref[hd]
            lhs = jnp.concatenate([scores.astype(BF16), (qr * xi_ref[hd]).astype(BF16)], axis=1)
            rhs = jnp.concatenate([vb, state.astype(BF16)], axis=0)
            out = _dot(lhs, rhs)
            state_ref[hd] = state * cd_ref[hd] + _dot_tn((kr * zeta_ref[hd]).astype(BF16), vb)
            mu = jnp.mean(out, axis=-1, keepdims=True)
            d = out - mu
            var = jnp.mean(d * d, axis=-1, keepdims=True)
            yn = d * lax.rsqrt(var + LN_EPS) * gain
            yg_ref[lo:hi, hd * dv:(hd + 1) * dv] = (g * jax.nn.sigmoid(g) * yn).astype(BF16)
    o_ref[...] = _dot(yg_ref[...], wp_ref[...]).astype(BF16)


def _retention_branch(x2, pos_col, freq2, sign2, wa_bf, wp_bf, consts, gain, batch, seq):
    n, d = x2.shape
    tr = min(256, seq)
    nr = seq // tr
    mask, xi_b, zeta_b, cd_b = consts
    vw = RET_HEADS * RET_V_DIM

    def once(shape):
        return pl.BlockSpec(shape, lambda b, r: (0,) * len(shape), pipeline_mode=pl.Buffered(1))

    rows = lambda w: pl.BlockSpec((tr, w), lambda b, r: (b * nr + r, 0))
    return pl.pallas_call(
        functools.partial(_retention_body, n_chunks=tr // CHUNK),
        grid=(batch, nr),
        in_specs=[rows(d), rows(1), once((1, LANES)), once((1, LANES)),
                  once((RET_HEADS, d, HEAD_COLS)), once((vw, d)),
                  once((RET_HEADS, CHUNK, CHUNK)), once((RET_HEADS, CHUNK, RET_QK_DIM)),
                  once((RET_HEADS, CHUNK, RET_QK_DIM)), once((RET_HEADS, 1, RET_V_DIM)), once((1, vw))],
        out_specs=[rows(d), rows(d)],
        out_shape=[jax.ShapeDtypeStruct((n, d), BF16), jax.ShapeDtypeStruct((n, d), BF16)],
        scratch_shapes=[pltpu.VMEM((RET_HEADS, RET_QK_DIM, RET_V_DIM), F32), pltpu.VMEM((tr, vw), BF16),
                        pltpu.VMEM((tr, LANES), F32), pltpu.VMEM((tr, LANES), F32)],
        compiler_params=_params(("arbitrary", "arbitrary")), name="retention_branch",
    )(x2, pos_col, freq2, sign2, wa_bf, wp_bf, mask, xi_b, zeta_b, cd_b, gain)


def _sgu_merge_body(u_ref, v_ref, sa_ref, sb_ref, ya_ref, lng_ref, lnb_ref, ws_ref, bs_ref, wp_ref,
                    o_ref, wtril_ref, ys_ref, wpb_ref, *, n_chunks):
    @pl.when(pl.program_id(0) == 0)
    def _():
        r = lax.broadcasted_iota(jnp.int32, (CHUNK, CHUNK), 0)
        c = lax.broadcasted_iota(jnp.int32, (CHUNK, CHUNK), 1)
        for g in range(SGU_GROUPS):
            wtril_ref[g] = jnp.where(r >= c, ws_ref[g], 0.0).astype(BF16)
        wpb_ref[...] = wp_ref[...].astype(BF16)

    vn = _layer_norm(v_ref[...].astype(F32), lng_ref[...], lnb_ref[...]).astype(BF16)
    for g in range(SGU_GROUPS):
        cols = pl.ds(g * CHUNK, CHUNK)
        lo = g * CHUNK
        rhs = jnp.concatenate([vn[c * CHUNK:(c + 1) * CHUNK, lo:lo + CHUNK] for c in range(n_chunks)], axis=1)
        mix = _dot(wtril_ref[g], rhs)
        bias = bs_ref[g]
        for c in range(n_chunks):
            rows = pl.ds(c * CHUNK, CHUNK)
            u = u_ref[rows, cols].astype(F32)
            ys_ref[rows, cols] = (u * (mix[:, c * CHUNK:(c + 1) * CHUNK] + bias)).astype(BF16)

    yb = _dot(ys_ref[...], wpb_ref[...])
    o_ref[...] = (sa_ref[...].astype(F32) * ya_ref[...].astype(F32) + sb_ref[...].astype(F32) * yb).astype(BF16)


def _sgu_merge(h, y_a, ln_g, ln_b, w_s, b_s_b, wp):
    n, d = y_a.shape
    ts = min(256, n)
    tile = lambda blk: pl.BlockSpec((ts, d), lambda i: (i, blk))
    row = pl.BlockSpec((1, d), lambda i: (0, 0))
    full3 = pl.BlockSpec((SGU_GROUPS, CHUNK, CHUNK), lambda i: (0, 0, 0))
    return pl.pallas_call(
        functools.partial(_sgu_merge_body, n_chunks=ts // CHUNK),
        grid=(n // ts,),
        in_specs=[tile(0), tile(1), tile(2), tile(3), tile(0), row, row, full3, full3,
                  pl.BlockSpec((d, d), lambda i: (0, 0), pipeline_mode=pl.Buffered(1))],
        out_specs=tile(0),
        out_shape=jax.ShapeDtypeStruct((n, d), BF16),
        scratch_shapes=[pltpu.VMEM((SGU_GROUPS, CHUNK, CHUNK), BF16), pltpu.VMEM((ts, d), BF16),
                        pltpu.VMEM((d, d), BF16)],
        compiler_params=_params(("arbitrary",)), name="sgu_merge",
    )(h, h, h, h, y_a, ln_g, ln_b, w_s, b_s_b, wp)


def _outproj_router_body(m_ref, x_ref, w_ref, g_ref, b_ref, whi_ref, wlo_ref, rb_ref,
                         o_ref, orow_hbm, ids_ref, cw_ref, cnt_ref, carry_ref, tri_ref, wb_ref, rsem, *, tm):
    step = pl.program_id(0)

    @pl.when(step == 0)
    def _():
        carry_ref[...] = jnp.zeros_like(carry_ref)
        r = lax.broadcasted_iota(jnp.int32, (tm, tm), 0)
        c = lax.broadcasted_iota(jnp.int32, (tm, tm), 1)
        tri_ref[...] = jnp.where(r < c, 1.0, 0.0).astype(BF16)
        wb_ref[...] = w_ref[...].astype(BF16)

    r = DN_ALPHA * x_ref[...] + _dot(m_ref[...], wb_ref[...])
    x = _layer_norm(r, g_ref[...], b_ref[...])
    o_ref[...] = x

    row_copy = _row_copy(o_ref, orow_hbm.at[pl.ds(pl.multiple_of(step * tm, tm), tm), 0], rsem)
    row_copy.start()

    hi = x.astype(BF16)
    lo = (x - hi.astype(F32)).astype(BF16)
    whi = whi_ref[...]
    logits = _dot_nt(whi, hi) + _dot_nt(whi, lo) + _dot_nt(wlo_ref[...], hi) + rb_ref[...]
    el = logits[0:N_EXPERTS, :]
    gl = logits[N_EXPERTS:N_EXPERTS + N_GROUPS, :]

    gi = lax.broadcasted_iota(jnp.int32, gl.shape, 0)
    gmax = jnp.max(gl, axis=0, keepdims=True)
    gidx = jnp.min(jnp.where(gl == gmax, gi, N_GROUPS), axis=0, keepdims=True)
    p_group = 1.0 / jnp.sum(jnp.exp(gl - gmax), axis=0, keepdims=True)

    ei = lax.broadcasted_iota(jnp.int32, el.shape, 0)
    first = gidx * EXPERTS_PER_GROUP
    in_group = (ei >= first) & (ei < first + EXPERTS_PER_GROUP)
    m1 = jnp.where(in_group, el, -jnp.inf)
    v1 = jnp.max(m1, axis=0, keepdims=True)
    i1 = jnp.min(jnp.where(in_group & (m1 == v1), ei, N_EXPERTS), axis=0, keepdims=True)
    rest = in_group & (ei != i1)
    m2 = jnp.where(rest, el, -jnp.inf)
    v2 = jnp.max(m2, axis=0, keepdims=True)
    i2 = jnp.min(jnp.where(rest & (m2 == v2), ei, N_EXPERTS), axis=0, keepdims=True)
    t = jnp.exp(v2 - v1)
    p1 = 1.0 / (1.0 + t)
    p2 = t * p1

    sel1, sel2 = ei == i1, ei == i2
    onehot = jnp.where(sel1 | sel2, 1.0, 0.0)
    before = _dot(onehot.astype(BF16), tri_ref[...]) + carry_ref[:, 0:1]
    r1 = jnp.sum(jnp.where(sel1, before, 0.0), axis=0, keepdims=True)
    r2 = jnp.sum(jnp.where(sel2, before, 0.0), axis=0, keepdims=True)
    carry_ref[...] = carry_ref[...] + jnp.sum(onehot, axis=1, keepdims=True)

    ids_ref[0:1, :] = i1
    ids_ref[1:2, :] = i2
    ids_ref[2:3, :] = r1.astype(jnp.int32)
    ids_ref[3:4, :] = r2.astype(jnp.int32)
    cw_ref[0:1, :] = p_group * p1
    cw_ref[1:2, :] = p_group * p2
    cnt_ref[...] = carry_ref[...]
    row_copy.wait()


def _out_proj_router(merged, x2, w_out, ln_g, ln_b, w_hi, w_lo, bias_col):
    n, d = x2.shape
    tm = min(512, n)
    tile = pl.BlockSpec((tm, d), lambda i: (i, 0))
    row = pl.BlockSpec((1, d), lambda i: (0, 0))
    full = pl.BlockSpec((ROUTER_ROWS, d), lambda i: (0, 0))
    return pl.pallas_call(
        functools.partial(_outproj_router_body, tm=tm), grid=(n // tm,),
        in_specs=[tile, tile, pl.BlockSpec((d, d), lambda i: (0, 0), pipeline_mode=pl.Buffered(1)), row, row,
                  full, full, pl.BlockSpec((ROUTER_ROWS, 1), lambda i: (0, 0))],
        out_specs=[tile, pl.BlockSpec(memory_space=pl.ANY),
                   pl.BlockSpec((4, tm), lambda i: (0, i)),
                   pl.BlockSpec((2, tm), lambda i: (0, i)),
                   pl.BlockSpec((N_EXPERTS, LANES), lambda i: (0, 0))],
        out_shape=[jax.ShapeDtypeStruct((n, d), F32), jax.ShapeDtypeStruct((n, 1, d), F32),
                   jax.ShapeDtypeStruct((4, n), jnp.int32),
                   jax.ShapeDtypeStruct((2, n), F32),
                   jax.ShapeDtypeStruct((N_EXPERTS, LANES), F32)],
        scratch_shapes=[pltpu.VMEM((N_EXPERTS, LANES), F32), pltpu.VMEM((tm, tm), BF16),
                        pltpu.VMEM((d, d), BF16), pltpu.SemaphoreType.DMA(())],
        compiler_params=_params(("arbitrary",)), name="out_proj_ln1_router",
    )(merged, x2, w_out, ln_g, ln_b, w_hi, w_lo, bias_col)


def _dest_body(ids_ref, rs_ref, o_ref):
    ids = ids_ref[...]
    e = lax.broadcasted_iota(jnp.int32, (N_EXPERTS, ids.shape[1]), 0)
    for k in range(2):
        start = jnp.sum(jnp.where(e == ids[k:k + 1, :], rs_ref[...], 0), axis=0, keepdims=True)
        o_ref[k:k + 1, :] = start + ids[k + 2:k + 3, :]


def _dest_rows(ids, row_start):
    n = ids.shape[1]
    tn = min(4096, n)
    return pl.pallas_call(
        _dest_body, grid=(n // tn,),
        in_specs=[pl.BlockSpec((4, tn), lambda i: (0, i)), pl.BlockSpec((N_EXPERTS, 1), lambda i: (0, 0))],
        out_specs=pl.BlockSpec((2, tn), lambda i: (0, i)),
        out_shape=jax.ShapeDtypeStruct((2, n), jnp.int32),
        compiler_params=_params(("arbitrary",)), name="dest_rows",
    )(ids, row_start[:, None])


def _routing_tables(ids, counts, n, max_pairs):
    te = EXPERT_TILE
    n_tiles = 2 * n // te
    i32 = jnp.int32
    row_end = jnp.cumsum(counts)
    row_start = row_end - counts
    dest = _dest_rows(ids, row_start)
    dest1, dest2 = dest[0], dest[1]
    ne_cum = jnp.cumsum((counts > 0).astype(i32))
    tile_lo = jnp.arange(n_tiles, dtype=i32) * te
    first_e = jnp.sum((row_end[None, :] <= tile_lo[:, None]).astype(i32), axis=1)
    last_e = jnp.sum((row_end[None, :] <= (tile_lo + te - 1)[:, None]).astype(i32), axis=1)
    pairs_t = ne_cum[last_e] - ne_cum[first_e] + 1
    pair_end = jnp.cumsum(pairs_t)
    pair_start = pair_end - pairs_t
    n_pairs = pair_end[-1]
    p = jnp.minimum(jnp.arange(max_pairs, dtype=i32), n_pairs - 1)
    p_tile = jnp.sum((pair_end[None, :] <= p[:, None]).astype(i32), axis=1)
    rank = ne_cum[first_e[p_tile]] - 1 + (p - pair_start[p_tile])
    p_exp = jnp.sum((ne_cum[None, :] <= rank[:, None]).astype(i32), axis=1)
    p_lo = jnp.maximum(row_start[p_exp], p_tile * te) - p_tile * te
    p_hi = jnp.minimum(row_end[p_exp], (p_tile + 1) * te) - p_tile * te
    p_next = jnp.where(rank + 1 < ne_cum[-1],
                       jnp.sum((ne_cum[None, :] <= (rank + 1)[:, None]).astype(i32), axis=1), -1)
    p_slot = rank & 1
    return (dest1, dest2, p_tile.astype(i32), p_exp.astype(i32), p_lo.astype(i32), p_hi.astype(i32),
            n_pairs.reshape(1).astype(i32), p_next.astype(i32), p_slot.astype(i32))


def _row_copy(src, dst, sem):
    return pltpu.make_async_copy(src, dst, sem)


def _expert_body(ptile_ref, pexp_ref, plo_ref, phi_ref, np_ref, pnext_ref, pslot_ref, d1_ref, d2_ref,
                 x1_hbm, w1_hbm, w3_hbm, w2_hbm, y2_hbm,
                 w1b, w3b, w2b, wf1, wf3, wf2, xbuf0, xbuf1, xbuf2, ybuf0, ybuf1, ybuf2, inv_s, gsem, ssem, wsem,
                 *, n, n_tiles):
    te = EXPERT_TILE
    p = pl.program_id(0)
    valid = p < np_ref[0]
    t = ptile_ref[p]
    pm1 = jnp.maximum(p - 1, 0)
    first = (p == 0) | (ptile_ref[pm1] != t)
    new_expert = (p == 0) | (pexp_ref[pm1] != pexp_ref[p])
    lo, hi = plo_ref[p], phi_ref[p]
    xbufs, ybufs = (xbuf0, xbuf1, xbuf2), (ybuf0, ybuf1, ybuf2)

    def gather(tile, slot, j, priority=0):
        v = inv_s[tile * te + j]
        tok = jnp.where(v >= n, v - n, v)
        _row_copy(x1_hbm.at[tok], xbufs[slot].at[pl.ds(j, 1)], gsem.at[slot]).start(priority=priority)

    def scatter(tile, slot, j, priority=0):
        _row_copy(ybufs[slot].at[pl.ds(j, 1)], y2_hbm.at[inv_s[tile * te + j]], ssem).start(priority=priority)

    def wait_gather(slot):
        _row_copy(x1_hbm.at[pl.ds(0, te), 0], xbufs[slot], gsem.at[slot]).wait()

    def wait_scatter(slot):
        _row_copy(ybufs[slot], y2_hbm.at[pl.ds(0, te), 0], ssem).wait()

    def issue_loop(fn, tile, slot):
        def one(j, c):
            fn(tile, slot, j)
            return c
        lax.fori_loop(0, te, one, 0, unroll=8)

    def weight_copies(e, slot):
        return [_row_copy(w1_hbm.at[e], wf1.at[slot], wsem.at[slot]),
                _row_copy(w3_hbm.at[e], wf3.at[slot], wsem.at[slot]),
                _row_copy(w2_hbm.at[e], wf2.at[slot], wsem.at[slot])]

    @pl.when(p == 0)
    def _():
        for c in weight_copies(pexp_ref[0], pslot_ref[0]):
            c.start()

        def invert(tk, c):
            inv_s[d1_ref[tk]] = tk
            inv_s[d2_ref[tk]] = tk + n
            return c
        lax.fori_loop(0, n, invert, 0, unroll=8)
        for yb in ybufs:
            yb[...] = jnp.zeros_like(yb)
        issue_loop(gather, 0, 0)
        issue_loop(gather, 1, 1)

    @pl.when(valid & new_expert)
    def _():
        slot = pslot_ref[p]
        for c in weight_copies(pexp_ref[p], slot):
            c.wait()
        w1b[...] = wf1[slot].astype(BF16)
        w3b[...] = wf3[slot].astype(BF16)
        w2b[...] = wf2[slot].astype(BF16)

        @pl.when(pnext_ref[p] >= 0)
        def _():
            for c in weight_copies(pnext_ref[p], 1 - slot):
                c.start()

    def ffn(slot):
        xb = xbufs[slot][...].astype(BF16)
        a = _dot(xb, w1b[...])
        b = _dot(xb, w3b[...])
        y = _dot((a * jax.nn.sigmoid(a) * b).astype(BF16), w2b[...])
        rows = lax.broadcasted_iota(jnp.int32, (te, 1), 0)
        ybufs[slot][...] = jnp.where((rows >= lo) & (rows < hi), y, ybufs[slot][...])

    for r in range(3):
        ahead, behind = (r + 2) % 3, (r + 1) % 3
        mine = valid & (lax.rem(t, 3) == r)
        interior = first & (t >= 1) & (t < n_tiles - 2)

        @pl.when(mine & first)
        def _():
            wait_gather(r)

            @pl.when(t >= 2)
            def _():
                wait_scatter(behind)

        @pl.when(mine & interior)
        def _():
            for j in range(te):
                gather(t + 2, ahead, j, priority=j % 2)
                scatter(t - 1, ahead, j, priority=j % 2)
            ffn(r)

        @pl.when(mine & jnp.logical_not(interior))
        def _():
            @pl.when(first & (t == 0))
            def _():
                issue_loop(gather, 2, ahead)

            @pl.when(first & (t >= n_tiles - 2))
            def _():
                issue_loop(scatter, t - 1, ahead)

            ffn(r)

    @pl.when(p == pl.num_programs(0) - 1)
    def _():
        last = n_tiles - 1
        wait_scatter((last - 1) % 3)
        issue_loop(scatter, last, last % 3)
        wait_scatter(last % 3)


def _experts(x1_rows, tables, w1, w3, w2):
    n, _, d = x1_rows.shape
    de = w1.shape[2]
    te = EXPERT_TILE
    n_tiles = 2 * n // te
    assert n_tiles >= 4
    max_pairs = n_tiles + N_EXPERTS - 1
    hbm = pl.BlockSpec(memory_space=pl.ANY)
    return pl.pallas_call(
        functools.partial(_expert_body, n=n, n_tiles=n_tiles),
        grid_spec=pltpu.PrefetchScalarGridSpec(
            num_scalar_prefetch=9, grid=(max_pairs,),
            in_specs=[hbm, hbm, hbm, hbm],
            out_specs=hbm,
            scratch_shapes=[pltpu.VMEM((d, de), BF16), pltpu.VMEM((d, de), BF16), pltpu.VMEM((de, d), BF16),
                            pltpu.VMEM((2, d, de), F32), pltpu.VMEM((2, d, de), F32), pltpu.VMEM((2, de, d), F32),
                            *[pltpu.VMEM((te, d), F32)] * 6,
                            pltpu.SMEM((2 * n,), jnp.int32),
                            pltpu.SemaphoreType.DMA((3,)), pltpu.SemaphoreType.DMA(()),
                            pltpu.SemaphoreType.DMA((2,))]),
        out_shape=jax.ShapeDtypeStruct((2 * n, 1, d), F32),
        compiler_params=_params(("arbitrary",)), name="experts",
    )(*tables, x1_rows, w1, w3, w2)


def _combine_body(x_ref, cw_ref, g_ref, b_ref, y2_hbm, o_ref, ybuf, sem, *, tm, n):
    i = pl.program_id(0)

    def copies(tile, buf):
        r0 = pl.multiple_of(tile * tm, tm)
        return [_row_copy(y2_hbm.at[pl.ds(k * n + r0, tm), 0], ybuf.at[buf, k], sem.at[buf]) for k in range(2)]

    @pl.when(i == 0)
    def _():
        for c in copies(0, 0):
            c.start()

    @pl.when(i + 1 < pl.num_programs(0))
    def _():
        for c in copies(i + 1, (i + 1) % 2):
            c.start()

    buf = i % 2
    for c in copies(i, buf):
        c.wait()
    cw_cols = jnp.concatenate([cw_ref[...], jnp.zeros((LANES - 2, tm), F32)], axis=0).T
    y = cw_cols[:, 0:1] * ybuf[buf, 0] + cw_cols[:, 1:2] * ybuf[buf, 1]
    o_ref[...] = _layer_norm(DN_ALPHA * x_ref[...] + y, g_ref[...], b_ref[...])


def _combine(y2, x1, cw_col, ln_g, ln_b):
    n, d = x1.shape
    tm = min(512, n)
    tile = pl.BlockSpec((tm, d), lambda i: (i, 0))
    row = pl.BlockSpec((1, d), lambda i: (0, 0))
    return pl.pallas_call(
        functools.partial(_combine_body, tm=tm, n=n), grid=(n // tm,),
        in_specs=[tile, pl.BlockSpec((2, tm), lambda i: (0, i)), row, row, pl.BlockSpec(memory_space=pl.ANY)],
        out_specs=tile,
        out_shape=jax.ShapeDtypeStruct((n, d), F32),
        scratch_shapes=[pltpu.VMEM((2, 2, tm, d), F32), pltpu.SemaphoreType.DMA((2,))],
        compiler_params=_params(("arbitrary",)), name="combine_ln2",
    )(x1, cw_col, ln_g, ln_b, y2)


def _retention_consts():
    h, c = RET_HEADS, CHUNK
    log_gamma = jnp.log1p(-jnp.exp2(-5.0 - jnp.arange(h, dtype=F32)))
    idx = jnp.arange(c, dtype=F32)
    rel = idx[:, None] - idx[None, :]
    mask = jnp.where(rel >= 0, jnp.exp(log_gamma[:, None, None] * jnp.maximum(rel, 0.0)), 0.0)
    xi = jnp.exp(log_gamma[:, None] * (idx + 1.0))
    zeta = jnp.exp(log_gamma[:, None] * (c - 1.0 - idx))
    chunk_decay = jnp.exp(log_gamma * c)
    xi_b = jnp.broadcast_to(xi[:, :, None], (h, c, RET_QK_DIM))
    zeta_b = jnp.broadcast_to(zeta[:, :, None], (h, c, RET_QK_DIM))
    cd_b = jnp.broadcast_to(chunk_decay[:, None, None], (h, 1, RET_V_DIM))
    return mask, xi_b, zeta_b, cd_b


def _rope_consts():
    half = RET_QK_DIM // 2
    freq = ROPE_THETA ** (-jnp.arange(half, dtype=F32) / half)
    freq2 = jnp.concatenate([freq, freq])[None, :]
    sign2 = jnp.concatenate([-jnp.ones((half,), F32), jnp.ones((half,), F32)])[None, :]
    return freq2, sign2


def kernel(x, positions, w_in, b_gate, ret_gn_g, sgu_ln_g, sgu_ln_b, sgu_w, sgu_b, w_proj_ret, w_proj_sgu,
           w_out, ln1_g, ln1_b, w_group, b_group, w_er, b_er, w1, w3, w2, ln2_g, ln2_b):
    batch, seq, d = x.shape
    n = batch * seq
    qk_w, v_w = RET_HEADS * RET_QK_DIM, RET_HEADS * RET_V_DIM
    assert d == v_w == SGU_GROUPS * CHUNK and seq % CHUNK == 0
    ret_cols = 2 * qk_w + 2 * v_w
    max_pairs = (2 * n) // EXPERT_TILE + N_EXPERTS - 1

    freq2, sign2 = _rope_consts()
    pos_col = positions.reshape(n, 1)
    ret_consts = _retention_consts()

    xc = x.reshape(n, d)
    for l in range(w_in.shape[0]):
        wa, wp_ret = _regroup_retention_weights(w_in[l], w_proj_ret[l])
        y_a, xb = _retention_branch(xc, pos_col, freq2, sign2, wa, wp_ret, ret_consts,
                                    ret_gn_g[l][None, :], batch, seq)
        gate_bias = jnp.concatenate([jnp.zeros((2 * d,), F32), b_gate[l].reshape(-1)])[None, :]
        h = _in_proj(xb, w_in[l], ret_cols, gate_bias)
        bias_b = jnp.broadcast_to(sgu_b[l][:, :, None], (SGU_GROUPS, CHUNK, CHUNK))
        merged = _sgu_merge(h, y_a, sgu_ln_g[l][None, :], sgu_ln_b[l][None, :], sgu_w[l], bias_b, w_proj_sgu[l])

        w_r = jnp.concatenate([w_er[l], w_group[l]], axis=1).T
        w_r = jnp.pad(w_r, ((0, ROUTER_ROWS - w_r.shape[0]), (0, 0)))
        w_hi = w_r.astype(BF16)
        w_lo = (w_r - w_hi.astype(F32)).astype(BF16)
        bias = jnp.pad(jnp.concatenate([b_er[l], b_group[l]]), (0, ROUTER_ROWS - N_EXPERTS - N_GROUPS))[:, None]
        x1, x1_rows, ids, cw, cnt = _out_proj_router(merged, xc, w_out[l], ln1_g[l][None, :],
                                                     ln1_b[l][None, :], w_hi, w_lo, bias)

        counts = cnt[:, 0].astype(jnp.int32)
        dest1, dest2, p_tile, p_exp, p_lo, p_hi, n_pairs, p_next, p_slot = _routing_tables(ids, counts, n, max_pairs)
        y2 = _experts(x1_rows, (p_tile, p_exp, p_lo, p_hi, n_pairs, p_next, p_slot, dest1, dest2),
                      w1[l], w3[l], w2[l])
        xc = _combine(y2, x1, cw, ln2_g[l][None, :], ln2_b[l][None, :])
    return xc.reshape(batch, seq, d)
```
